```python
import jax, jax.numpy as jnp
from jax import lax
import numpy as np

D_MODEL = 1024
BATCH = 16
SEQ = 2048
DEPTH = 2

GRID_W = 64
ROPE_THETA = 10000.0
EPS = 1e-6
HEAD_DIM = 64
BLOCK = 128
WINDOW = 128

A_HEADS = 6
A_KV_HEADS = 2
B_HEADS = 6
B_KV_HEADS = 2
C_HEADS = 4
C_NOPE = 64
C_ROPE = 32
C_V = 64
C_Q_RANK = 192
C_KV_RANK = 128

A_W = A_HEADS * HEAD_DIM
A_KV_W = A_KV_HEADS * HEAD_DIM
B_W = B_HEADS * HEAD_DIM
B_KV_W = B_KV_HEADS * HEAD_DIM
C_W = C_HEADS * C_V
MIX_W = A_W + B_W + C_W

IN_SPLITS = (A_W, A_KV_W, A_KV_W, A_W,
             B_W, B_KV_W, B_KV_W, B_W,
             C_Q_RANK, C_KV_RANK, C_ROPE, C_W)
IN_COLS = 2656

kernel_name = "hybrid_parallel_heads_encoder"


def rms_norm(x, g):
    xf = x.astype(jnp.float32)
    y = xf * lax.rsqrt(jnp.mean(xf * xf, axis=-1, keepdims=True) + EPS)
    return (y * g.astype(jnp.float32)).astype(x.dtype)


def rope_tables(pos, dim):
    inv = ROPE_THETA ** (-jnp.arange(0, dim, 2, dtype=jnp.float32) / dim)
    ang = pos.astype(jnp.float32)[:, None] * inv[None, :]
    ang = jnp.concatenate([ang, ang], axis=-1)
    return jnp.cos(ang), jnp.sin(ang)


def apply_rope(x, cos, sin):
    x1, x2 = jnp.split(x, 2, axis=-1)
    rot = jnp.concatenate([-x2, x1], axis=-1)
    c = cos[None, :, None, :].astype(x.dtype)
    s = sin[None, :, None, :].astype(x.dtype)
    return x * c + rot * s


def apply_axial_rope(x, row_cs, col_cs):
    half = x.shape[-1] // 2
    xr = apply_rope(x[..., :half], row_cs[0], row_cs[1])
    xc = apply_rope(x[..., half:], col_cs[0], col_cs[1])
    return jnp.concatenate([xr, xc], axis=-1)


def dense_gqa_blocks(q, k, v, scale):
    bsz, s_len, n_h, d = q.shape
    n_kv = k.shape[2]
    grp = n_h // n_kv
    nb = s_len // BLOCK
    qb = q.reshape(bsz, nb, BLOCK, n_kv, grp, d).transpose(1, 0, 2, 3, 4, 5)

    def one_block(q_blk):
        s = jnp.einsum('bqhgd,bkhd->bhgqk', q_blk, k).astype(jnp.float32) * scale
        p = jax.nn.softmax(s, axis=-1).astype(v.dtype)
        return jnp.einsum('bhgqk,bkhd->bqhgd', p, v)

    o = lax.map(one_block, qb)
    return o.transpose(1, 0, 2, 3, 4, 5).reshape(bsz, s_len, n_h * d)


def windowed_sink_gqa(q, k, v, sink, scale):
    bsz, s_len, n_h, d = q.shape
    n_kv = k.shape[2]
    grp = n_h // n_kv
    nb = s_len // BLOCK
    qb = q.reshape(bsz, nb, BLOCK, n_kv, grp, d)
    pad = ((0, 0), (BLOCK, BLOCK), (0, 0), (0, 0))
    kp = jnp.pad(k, pad).reshape(bsz, nb + 2, BLOCK, n_kv, d)
    vp = jnp.pad(v, pad).reshape(bsz, nb + 2, BLOCK, n_kv, d)
    kb = jnp.concatenate([kp[:, :-2], kp[:, 1:-1], kp[:, 2:]], axis=2)
    vb = jnp.concatenate([vp[:, :-2], vp[:, 1:-1], vp[:, 2:]], axis=2)
    s = jnp.einsum('bnqhgd,bnkhd->bnhgqk', qb, kb).astype(jnp.float32) * scale
    blk = jnp.arange(nb)[:, None]
    qpos = blk * BLOCK + jnp.arange(BLOCK)[None, :]
    kpos = (blk - 1) * BLOCK + jnp.arange(3 * BLOCK)[None, :]
    valid = ((jnp.abs(qpos[:, :, None] - kpos[:, None, :]) <= WINDOW)
             & (kpos[:, None, :] >= 0) & (kpos[:, None, :] < s_len))
    s = jnp.where(valid[None, :, None, None], s, -1e30)
    sk = sink.astype(jnp.float32).reshape(1, 1, n_kv, grp, 1, 1)
    m = jnp.maximum(jnp.max(s, axis=-1, keepdims=True), sk)
    p = jnp.exp(s - m)
    denom = jnp.sum(p, axis=-1, keepdims=True) + jnp.exp(sk - m)
    p = (p / denom).astype(v.dtype)
    o = jnp.einsum('bnhgqk,bnkhd->bnqhgd', p, vb)
    return o.reshape(bsz, s_len, n_h * d)


def mla_blocks(q_nope, q_rope, k_nope, k_rope, v, scale):
    bsz, s_len, n_h, dn = q_nope.shape
    dr = q_rope.shape[-1]
    dv = v.shape[-1]
    nb = s_len // BLOCK
    qn = q_nope.reshape(bsz, nb, BLOCK, n_h, dn).transpose(1, 0, 2, 3, 4)
    qr = q_rope.reshape(bsz, nb, BLOCK, n_h, dr).transpose(1, 0, 2, 3, 4)

    def one_block(args):
        qn_b, qr_b = args
        s = (jnp.einsum('bqhd,bkhd->bhqk', qn_b, k_nope)
             + jnp.einsum('bqhr,bkr->bhqk', qr_b, k_rope)).astype(jnp.float32) * scale
        p = jax.nn.softmax(s, axis=-1).astype(v.dtype)
        return jnp.einsum('bhqk,bkhd->bqhd', p, v)

    o = lax.map(one_block, (qn, qr))
    return o.transpose(1, 0, 2, 3, 4).reshape(bsz, s_len, n_h * dv)


def hybrid_layer(x, g_norm, w_in, a_qn, a_kn, b_sink, c_qn, c_kvn, c_wuq, c_wukv, w_out,
                 axial_row, axial_col, rope_1d, rope_mla):
    bsz, s_len, _ = x.shape
    h = rms_norm(x, g_norm)
    z = jnp.einsum('bsd,dc->bsc', h, w_in)
    idx = np.cumsum(IN_SPLITS)[:-1].tolist()
    aq, ak, av, ag, bq, bk, bv, bg, cq, ckv, ckr, cg = jnp.split(z, idx, axis=-1)

    qa = rms_norm(aq.reshape(bsz, s_len, A_HEADS, HEAD_DIM), a_qn)
    ka = rms_norm(ak.reshape(bsz, s_len, A_KV_HEADS, HEAD_DIM), a_kn)
    qa = apply_axial_rope(qa, axial_row, axial_col)
    ka = apply_axial_rope(ka, axial_row, axial_col)
    va = av.reshape(bsz, s_len, A_KV_HEADS, HEAD_DIM)
    oa = dense_gqa_blocks(qa, ka, va, HEAD_DIM ** -0.5) * jax.nn.silu(ag)

    qb = apply_rope(bq.reshape(bsz, s_len, B_HEADS, HEAD_DIM), rope_1d[0], rope_1d[1])
    kb = apply_rope(bk.reshape(bsz, s_len, B_KV_HEADS, HEAD_DIM), rope_1d[0], rope_1d[1])
    vb = bv.reshape(bsz, s_len, B_KV_HEADS, HEAD_DIM)
    ob = windowed_sink_gqa(qb, kb, vb, b_sink, HEAD_DIM ** -0.5) * jax.nn.silu(bg)

    qc = jnp.einsum('bsr,rc->bsc', rms_norm(cq, c_qn), c_wuq).reshape(bsz, s_len, C_HEADS, C_NOPE + C_ROPE)
    qc_nope = qc[..., :C_NOPE]
    qc_rope = apply_rope(qc[..., C_NOPE:], rope_mla[0], rope_mla[1])
    kv = jnp.einsum('bsr,rc->bsc', rms_norm(ckv, c_kvn), c_wukv).reshape(bsz, s_len, C_HEADS, C_NOPE + C_V)
    kc_nope = kv[..., :C_NOPE]
    vc = kv[..., C_NOPE:]
    kc_rope = apply_rope(ckr[:, :, None, :], rope_mla[0], rope_mla[1])[:, :, 0, :]
    oc = mla_blocks(qc_nope, qc_rope, kc_nope, kc_rope, vc, (C_NOPE + C_ROPE) ** -0.5) * jax.nn.silu(cg)

    o = jnp.concatenate([oa, ob, oc], axis=-1)
    return x + jnp.einsum('bsm,md->bsd', o, w_out)


def setup_inputs(seed: int = 0) -> dict:
    key = jax.random.key(seed)
    ks = jax.random.split(key, 13)
    f32 = jnp.float32

    def nrm(k, shape, scale):
        return jax.random.normal(k, shape, f32) * scale

    return {
        "x": nrm(ks[0], (BATCH, SEQ, D_MODEL), 1.0),
        "norm_g": 1.0 + nrm(ks[1], (DEPTH, D_MODEL), 0.02),
        "w_in": nrm(ks[2], (DEPTH, D_MODEL, IN_COLS), D_MODEL ** -0.5),
        "a_q_norm": 1.0 + nrm(ks[3], (DEPTH, HEAD_DIM), 0.02),
        "a_k_norm": 1.0 + nrm(ks[4], (DEPTH, HEAD_DIM), 0.02),
        "b_sink": nrm(ks[5], (DEPTH, B_HEADS), 0.5),
        "c_q_norm": 1.0 + nrm(ks[6], (DEPTH, C_Q_RANK), 0.02),
        "c_kv_norm": 1.0 + nrm(ks[7], (DEPTH, C_KV_RANK), 0.02),
        "c_w_uq": nrm(ks[8], (DEPTH, C_Q_RANK, C_HEADS * (C_NOPE + C_ROPE)), C_Q_RANK ** -0.5),
        "c_w_ukv": nrm(ks[9], (DEPTH, C_KV_RANK, C_HEADS * (C_NOPE + C_V)), C_KV_RANK ** -0.5),
        "w_out": nrm(ks[10], (DEPTH, MIX_W, D_MODEL), MIX_W ** -0.5),
        "final_g": 1.0 + nrm(ks[11], (D_MODEL,), 0.02),
    }


def reference(x, norm_g, w_in, a_q_norm, a_k_norm, b_sink, c_q_norm, c_kv_norm, c_w_uq, c_w_ukv, w_out, final_g):
    s_len = x.shape[1]
    rows = s_len // GRID_W
    t = jnp.arange(s_len)
    row_idx = jnp.repeat(jnp.arange(rows), GRID_W)
    col_idx = jnp.tile(jnp.arange(GRID_W), rows)
    axial_row = rope_tables(row_idx, HEAD_DIM // 2)
    axial_col = rope_tables(col_idx, HEAD_DIM // 2)
    rope_1d = rope_tables(t, HEAD_DIM)
    rope_mla = rope_tables(t, C_ROPE)

    h = x
    for l in range(DEPTH):
        h = hybrid_layer(h, norm_g[l], w_in[l], a_q_norm[l], a_k_norm[l], b_sink[l],
                         c_q_norm[l], c_kv_norm[l], c_w_uq[l], c_w_ukv[l], w_out[l],
                         axial_row, axial_col, rope_1d, rope_mla)
    return rms_norm(h, final_g)
```

```python
import functools

import jax
import jax.numpy as jnp
from jax import lax
from jax.experimental import pallas as pl
from jax.experimental.pallas import tpu as pltpu

F32 = jnp.float32
BF16 = jnp.bfloat16

GRID_W = 64
ROPE_THETA = 10000.0
EPS = 1e-6
HEAD_DIM = 64
WINDOW = 128
A_HEADS = 6
B_HEADS = 6
C_HEADS = 4
C_NOPE = 64
C_ROPE = 32
C_V = 64
C_Q_RANK = 192
C_KV_RANK = 128

LANES = 128
HALF = LANES // 2
C_Q_PAD = 256
NEG = -1e30

OFF_QA, OFF_QB = 0, 384
OFF_KA, OFF_KB = 768, 896
OFF_VA, OFF_VB = 1024, 1152
OFF_G = 1280
OFF_CQ = 2304
OFF_CKV = 2560
OFF_CKR = 2688
PACKED_W = 2816
MIX_W = 1024

PROJ_TM = 512
ATTN_TQ = 256
OUT_TM = 512
VMEM_LIMIT = 56 * 1024 * 1024


def _rope_tables(s_len):
    t = jnp.arange(s_len)
    lane = jnp.arange(LANES)

    def cs(pos, dim):
        inv = ROPE_THETA ** (-jnp.arange(0, dim, 2, dtype=F32) / dim)
        ang = pos.astype(F32)[:, None] * inv[None, :]
        ang = jnp.concatenate([ang, ang], axis=-1)
        return jnp.cos(ang), jnp.sin(ang)

    def split(cos, sin, low):
        return jnp.stack([cos, jnp.where(low[None, :], -sin, 0.0), jnp.where(low[None, :], 0.0, sin)])

    rc, rs = cs(t // GRID_W, HEAD_DIM // 2)
    cc, csn = cs(t % GRID_W, HEAD_DIM // 2)
    cos_a = jnp.tile(jnp.concatenate([rc, cc], -1), (1, 2))
    sin_a = jnp.tile(jnp.concatenate([rs, csn], -1), (1, 2))
    tab_a = split(cos_a, sin_a, (lane % 32) < 16)
    c1, s1 = cs(t, HEAD_DIM)
    tab_b = split(jnp.tile(c1, (1, 2)), jnp.tile(s1, (1, 2)), (lane % 64) < 32)
    cm, sm = cs(t, C_ROPE)
    ones = jnp.ones((s_len, C_NOPE), F32)
    zeros = jnp.zeros((s_len, C_NOPE), F32)
    pad = jnp.zeros((s_len, LANES - C_NOPE - C_ROPE), F32)
    cos_c = jnp.concatenate([ones, cm, pad + 1.0], -1)
    sin_c = jnp.concatenate([zeros, sm, pad], -1)
    tab_c = split(cos_c, sin_c, ((lane - C_NOPE) % 32) < 16)
    return jnp.concatenate([tab_a, tab_b, tab_c], 0)


def _pack_w_in(w):
    d = w.shape[0]
    aq, ak, av, ag = w[:, 0:384], w[:, 384:512], w[:, 512:640], w[:, 640:1024]
    bq, bk, bv, bg = w[:, 1024:1408], w[:, 1408:1536], w[:, 1536:1664], w[:, 1664:2048]
    cq, ckv, ckr, cg = w[:, 2048:2240], w[:, 2240:2368], w[:, 2368:2400], w[:, 2400:2656]
    z = lambda n: jnp.zeros((d, n), w.dtype)
    packed = jnp.concatenate(
        [aq, bq, ak, bk, av, bv, ag, bg, cg,
         cq, z(C_Q_PAD - C_Q_RANK),
         ckv,
         z(C_NOPE), ckr, z(LANES - C_NOPE - C_ROPE)], axis=1)
    assert packed.shape[1] == PACKED_W
    return packed.astype(BF16)


def _pack_w_uq(w):
    w = w.reshape(C_Q_RANK, C_HEADS, C_NOPE + C_ROPE)
    w = jnp.pad(w, ((0, C_Q_PAD - C_Q_RANK), (0, 0), (0, LANES - C_NOPE - C_ROPE)))
    return w.reshape(C_Q_PAD, C_HEADS * LANES).astype(BF16)


def _pack_w_ukv(w):
    w = w.reshape(C_KV_RANK, C_HEADS, C_NOPE + C_V)
    wk = jnp.pad(w[:, :, :C_NOPE], ((0, 0), (0, 0), (0, LANES - C_NOPE)))
    wv = w[:, :, C_NOPE:]
    return jnp.concatenate([wk.reshape(C_KV_RANK, C_HEADS * LANES),
                            wv.reshape(C_KV_RANK, C_HEADS * C_V)], axis=1).astype(BF16)


def _group_sum_matrix():
    i = jnp.arange(LANES)
    return ((i[:, None] // HALF) == (i[None, :] // HALF)).astype(BF16)


def _proj_body(x_ref, g_ref, w_ref, aqn_ref, akn_ref, cqn_ref, ckvn_ref, wuq_ref, wukv_ref, e_ref, tab_ref,
               qab_ref, ka_ref, va_ref, kb_ref, vb_ref, sg_ref, qc_ref, kc_ref, vc_ref):
    tm = x_ref.shape[0]
    x = x_ref[...]
    h = x * lax.rsqrt(jnp.mean(x * x, axis=-1, keepdims=True) + EPS) * g_ref[...]
    z = jnp.dot(h.astype(BF16), w_ref[...], preferred_element_type=F32)

    lane = lax.broadcasted_iota(jnp.int32, (tm, LANES), 1)
    low = lane < HALF

    def rope(xs, mixer, shift):
        c, sn, sp = tab_ref[3 * mixer], tab_ref[3 * mixer + 1], tab_ref[3 * mixer + 2]
        return xs * c + pltpu.roll(xs, LANES - shift, 1) * sn + pltpu.roll(xs, shift, 1) * sp

    def head_norm(xs, gain):
        t = xs * xs
        t_hi = t.astype(BF16)
        t_lo = (t - t_hi.astype(F32)).astype(BF16)
        ss = (jnp.dot(t_hi, e_ref[...], preferred_element_type=F32)
              + jnp.dot(t_lo, e_ref[...], preferred_element_type=F32))
        return xs * lax.rsqrt(ss * (1.0 / HEAD_DIM) + EPS) * gain

    def slab(off):
        return z[:, off:off + LANES]

    def pair_variants(ref, s):
        sw = pltpu.roll(s, HALF, 1)
        ref[0] = jnp.where(low, s, sw).astype(BF16)
        ref[1] = s.astype(BF16)
        ref[2] = jnp.where(low, sw, s).astype(BF16)

    scale_ab = HEAD_DIM ** -0.5
    for j in range(A_HEADS // 2):
        q = rope(head_norm(slab(OFF_QA + j * LANES), aqn_ref[...]), 0, 16)
        qab_ref[:, j * LANES:(j + 1) * LANES] = (q * scale_ab).astype(BF16)
    for j in range(B_HEADS // 2):
        q = rope(slab(OFF_QB + j * LANES), 1, 32)
        qab_ref[:, OFF_QB + j * LANES:OFF_QB + (j + 1) * LANES] = (q * scale_ab).astype(BF16)
    pair_variants(ka_ref, rope(head_norm(slab(OFF_KA), akn_ref[...]), 0, 16))
    pair_variants(kb_ref, rope(slab(OFF_KB), 1, 32))
    pair_variants(va_ref, slab(OFF_VA))
    pair_variants(vb_ref, slab(OFF_VB))

    gz = z[:, OFF_G:OFF_G + MIX_W]
    sg_ref[...] = (gz / (1.0 + jnp.exp(-gz))).astype(BF16)

    cq = z[:, OFF_CQ:OFF_CQ + C_Q_PAD]
    cq = cq * lax.rsqrt(jnp.sum(cq * cq, axis=-1, keepdims=True) * (1.0 / C_Q_RANK) + EPS) * cqn_ref[...]
    qc = jnp.dot(cq.astype(BF16), wuq_ref[...], preferred_element_type=F32)
    scale_c = (C_NOPE + C_ROPE) ** -0.5
    for hh in range(C_HEADS):
        q = rope(qc[:, hh * LANES:(hh + 1) * LANES], 2, 16)
        qc_ref[:, hh * LANES:(hh + 1) * LANES] = (q * scale_c).astype(BF16)
    ckv = z[:, OFF_CKV:OFF_CKV + C_KV_RANK]
    ckv = ckv * lax.rsqrt(jnp.mean(ckv * ckv, axis=-1, keepdims=True) + EPS) * ckvn_ref[...]
    kv = jnp.dot(ckv.astype(BF16), wukv_ref[...], preferred_element_type=F32)
    kr = rope(slab(OFF_CKR), 2, 16)
    for hh in range(C_HEADS):
        kc_ref[hh] = (kv[:, hh * LANES:(hh + 1) * LANES] + kr).astype(BF16)
    vc_ref[...] = kv[:, C_HEADS * LANES:].astype(BF16)


def _proj_call(x, g, w_packed, aqn, akn, cqn, ckvn, wuq, wukv, e_mat, tabs):
    bsz, s_len, d = x.shape
    tm = min(PROJ_TM, s_len)
    grid = (bsz, s_len // tm)
    full = lambda a: pl.BlockSpec(a.shape, lambda b, s: (0,) * a.ndim)
    tok = lambda w: pl.BlockSpec((None, tm, w), lambda b, s: (b, s, 0))
    var = lambda n: pl.BlockSpec((None, n, tm, LANES), lambda b, s: (b, 0, s, 0))
    bf = lambda *shape: jax.ShapeDtypeStruct(shape, BF16)
    return pl.pallas_call(
        _proj_body,
        grid=grid,
        in_specs=[tok(d), full(g), full(w_packed), full(aqn), full(akn), full(cqn), full(ckvn),
                  full(wuq), full(wukv), full(e_mat),
                  pl.BlockSpec((9, tm, LANES), lambda b, s: (0, s, 0))],
        out_specs=[tok(768), var(3), var(3), var(3), var(3), tok(MIX_W), tok(C_HEADS * LANES),
                   var(C_HEADS), tok(C_HEADS * C_V)],
        out_shape=[bf(bsz, s_len, 768), bf(bsz, 3, s_len, LANES), bf(bsz, 3, s_len, LANES),
                   bf(bsz, 3, s_len, LANES), bf(bsz, 3, s_len, LANES), bf(bsz, s_len, MIX_W),
                   bf(bsz, s_len, C_HEADS * LANES), bf(bsz, C_HEADS, s_len, LANES),
                   bf(bsz, s_len, C_HEADS * C_V)],
        compiler_params=pltpu.CompilerParams(
            dimension_semantics=("arbitrary", "arbitrary"), vmem_limit_bytes=VMEM_LIMIT),
        name="proj",
    )(x, g, w_packed, aqn, akn, cqn, ckvn, wuq, wukv, e_mat, tabs)


def _fill_vaug(v_ref, vaug_ref):
    @pl.when(pl.program_id(2) == 0)
    def _():
        vaug_ref[:, :LANES] = v_ref[...]
        vaug_ref[:, LANES:] = jnp.ones((vaug_ref.shape[0], LANES), BF16)


def _split_heads(q_ref):
    q = q_ref[...].astype(F32)
    low = lax.broadcasted_iota(jnp.int32, q.shape, 1) < HALF
    return jnp.where(low, q, 0.0).astype(BF16), jnp.where(low, 0.0, q).astype(BF16), low


def _scores(q, k):
    return lax.dot_general(q, k, (((1,), (1,)), ((), ())), preferred_element_type=F32)


def _dense_head(q, k_ref, vaug_ref):
    s = _scores(q, k_ref[...])
    p = jnp.exp(s - jnp.max(s, axis=1, keepdims=True)).astype(BF16)
    a = jnp.dot(p, vaug_ref[...], preferred_element_type=F32)
    return a[:, :LANES] / a[:, LANES:]


def _attn_a_body(q_ref, k_ref, v_ref, sg_ref, o_ref, vaug_ref):
    _fill_vaug(v_ref, vaug_ref)
    q0, q1, low = _split_heads(q_ref)
    o = jnp.where(low, _dense_head(q0, k_ref, vaug_ref), _dense_head(q1, k_ref, vaug_ref))
    o_ref[...] = (o * sg_ref[...].astype(F32)).astype(BF16)


def _attn_c_body(q0_ref, q1_ref, k0_ref, k1_ref, v_ref, sg_ref, o_ref, vaug_ref):
    _fill_vaug(v_ref, vaug_ref)
    low = lax.broadcasted_iota(jnp.int32, o_ref.shape, 1) < HALF
    o = jnp.where(low, _dense_head(q0_ref[...], k0_ref, vaug_ref), _dense_head(q1_ref[...], k1_ref, vaug_ref))
    o_ref[...] = (o * sg_ref[...].astype(F32)).astype(BF16)


def _attn_b_body(sink_ref, q_ref, k_ref, v_ref, sg_ref, o_ref, vaug_ref):
    _fill_vaug(v_ref, vaug_ref)
    tq = q_ref.shape[0]
    s_len = k_ref.shape[0]
    kw = min(tq + 2 * WINDOW, s_len)
    pair = pl.program_id(1)
    qi = pl.program_id(2)
    ws = pl.multiple_of(jnp.clip(qi * tq - WINDOW, 0, s_len - kw), WINDOW)
    k_win = k_ref[pl.ds(ws, kw), :]
    v_win = vaug_ref[pl.ds(ws, kw), :]
    q_pos = qi * tq + lax.broadcasted_iota(jnp.int32, (tq, kw), 0)
    k_pos = ws + lax.broadcasted_iota(jnp.int32, (tq, kw), 1)
    valid = jnp.abs(q_pos - k_pos) <= WINDOW
    q0, q1, low = _split_heads(q_ref)

    def head(q, sink):
        s = jnp.where(valid, _scores(q, k_win), NEG)
        m = jnp.maximum(jnp.max(s, axis=1, keepdims=True), sink)
        p = jnp.exp(s - m).astype(BF16)
        a = jnp.dot(p, v_win, preferred_element_type=F32)
        return a[:, :LANES] / (a[:, LANES:] + jnp.exp(sink - m))

    o = jnp.where(low, head(q0, sink_ref[2 * pair]), head(q1, sink_ref[2 * pair + 1]))
    o_ref[...] = (o * sg_ref[...].astype(F32)).astype(BF16)


def _attn_params():
    return pltpu.CompilerParams(dimension_semantics=("arbitrary",) * 3, vmem_limit_bytes=VMEM_LIMIT)


def _attn_ab_call(body, name, q_all, q_off, k_var, v_var, sg, g_off, sink=None):
    bsz, s_len, _ = q_all.shape
    tq = min(ATTN_TQ, s_len)
    n_pairs = 3
    grid = (bsz, n_pairs, s_len // tq)
    in_specs = [
        pl.BlockSpec((None, tq, LANES), lambda b, p, i: (b, i, q_off + p)),
        pl.BlockSpec((None, None, s_len, LANES), lambda b, p, i: (b, p, 0, 0)),
        pl.BlockSpec((None, None, s_len, LANES), lambda b, p, i: (b, p, 0, 0)),
        pl.BlockSpec((None, tq, LANES), lambda b, p, i: (b, i, g_off + p)),
    ]
    args = [q_all, k_var, v_var, sg]
    if sink is not None:
        in_specs = [pl.BlockSpec(memory_space=pltpu.SMEM)] + in_specs
        args = [sink] + args
    return pl.pallas_call(
        body,
        grid=grid,
        in_specs=in_specs,
        out_specs=pl.BlockSpec((None, tq, LANES), lambda b, p, i: (b, i, p)),
        out_shape=jax.ShapeDtypeStruct((bsz, s_len, n_pairs * LANES), BF16),
        scratch_shapes=[pltpu.VMEM((s_len, 2 * LANES), BF16)],
        compiler_params=_attn_params(),
        name=name,
    )(*args)


def _attn_c_call(qc, kc, vc, sg, g_off):
    bsz, s_len, _ = qc.shape
    tq = min(ATTN_TQ, s_len)
    n_pairs = C_HEADS // 2
    grid = (bsz, n_pairs, s_len // tq)
    q_spec = lambda o: pl.BlockSpec((None, tq, LANES), lambda b, p, i: (b, i, 2 * p + o))
    k_spec = lambda o: pl.BlockSpec((None, None, s_len, LANES), lambda b, p, i: (b, 2 * p + o, 0, 0))
    return pl.pallas_call(
        _attn_c_body,
        grid=grid,
        in_specs=[q_spec(0), q_spec(1), k_spec(0), k_spec(1),
                  pl.BlockSpec((None, s_len, LANES), lambda b, p, i: (b, 0, p)),
                  pl.BlockSpec((None, tq, LANES), lambda b, p, i: (b, i, g_off + p))],
        out_specs=pl.BlockSpec((None, tq, LANES), lambda b, p, i: (b, i, p)),
        out_shape=jax.ShapeDtypeStruct((bsz, s_len, n_pairs * LANES), BF16),
        scratch_shapes=[pltpu.VMEM((s_len, 2 * LANES), BF16)],
        compiler_params=_attn_params(),
        name="attn_c",
    )(qc, qc, kc, kc, vc, sg)


def _out_body(x_ref, oa_ref, ob_ref, oc_ref, wa_ref, wb_ref, wc_ref, fg_ref, y_ref, *, final):
    y = (jnp.dot(oa_ref[...], wa_ref[...], preferred_element_type=F32)
         + jnp.dot(ob_ref[...], wb_ref[...], preferred_element_type=F32)
         + jnp.dot(oc_ref[...], wc_ref[...], preferred_element_type=F32))
    r = x_ref[...] + y
    if final:
        r = r * lax.rsqrt(jnp.mean(r * r, axis=-1, keepdims=True) + EPS) * fg_ref[...]
    y_ref[...] = r


def _out_call(x, oa, ob, oc, w_out, fg, final):
    bsz, s_len, d = x.shape
    tm = min(OUT_TM, s_len)
    wa, wb, wc = w_out[:384], w_out[384:768], w_out[768:]
    tok = lambda w: pl.BlockSpec((None, tm, w), lambda b, s: (b, s, 0))
    full = lambda a: pl.BlockSpec(a.shape, lambda b, s: (0,) * a.ndim)
    return pl.pallas_call(
        functools.partial(_out_body, final=final),
        grid=(bsz, s_len // tm),
        in_specs=[tok(d), tok(384), tok(384), tok(256), full(wa), full(wb), full(wc), full(fg)],
        out_specs=tok(d),
        out_shape=jax.ShapeDtypeStruct(x.shape, F32),
        compiler_params=pltpu.CompilerParams(
            dimension_semantics=("arbitrary", "arbitrary"), vmem_limit_bytes=VMEM_LIMIT),
        name="out_final" if final else "out",
    )(x, oa, ob, oc, wa, wb, wc, fg)


def _layer(x, g, w_in, aqn, akn, sink, cqn, ckvn, wuq, wukv, w_out, fg, e_mat, tabs, final):
    two = lambda v: jnp.tile(v.astype(F32), 2)[None, :]
    qab, ka, va, kb, vb, sg, qc, kc, vc = _proj_call(
        x, g.astype(F32)[None, :], _pack_w_in(w_in), two(aqn), two(akn),
        jnp.pad(cqn.astype(F32), (0, C_Q_PAD - C_Q_RANK))[None, :], ckvn.astype(F32)[None, :],
        _pack_w_uq(wuq), _pack_w_ukv(wukv), e_mat, tabs)
    oa = _attn_ab_call(_attn_a_body, "attn_a", qab, 0, ka, va, sg, 0)
    ob = _attn_ab_call(_attn_b_body, "attn_b", qab, 3, kb, vb, sg, 3, sink=sink.astype(F32))
    oc = _attn_c_call(qc, kc, vc, sg, 6)
    return _out_call(x, oa, ob, oc, w_out.astype(BF16), fg.astype(F32)[None, :], final)


def kernel(x, norm_g, w_in, a_q_norm, a_k_norm, b_sink, c_q_norm, c_kv_norm, c_w_uq, c_w_ukv, w_out, final_g):
    depth = w_in.shape[0]
    tabs = _rope_tables(x.shape[1])
    e_mat = _group_sum_matrix()
    h = x
    for l in range(depth):
        h = _layer(h, norm_g[l], w_in[l], a_q_norm[l], a_k_norm[l], b_sink[l], c_q_norm[l], c_kv_norm[l],
                   c_w_uq[l], c_w_ukv[l], w_out[l], final_g, e_mat, tabs, final=(l == depth - 1))
    return h
```

```python
import functools

import jax
import jax.numpy as jnp
from jax import lax
from jax.experimental import pallas as pl
from jax.experimental.pallas import tpu as pltpu

F32 = jnp.float32
BF16 = jnp.bfloat16

GRID_W = 64
ROPE_THETA = 10000.0
EPS = 1e-6
HEAD_DIM = 64
WINDOW = 128
A_HEADS = 6
B_HEADS = 6
C_HEADS = 4
C_NOPE = 64
C_ROPE = 32
C_V = 64
C_Q_RANK = 192
C_KV_RANK = 128

LANES = 128
HALF = LANES // 2
C_Q_PAD = 256
NEG = -1e30

OFF_QA, OFF_QB = 0, 384
OFF_KA, OFF_KB = 768, 896
OFF_VA, OFF_VB = 1024, 1152
OFF_G = 1280
OFF_CQ = 2304
OFF_CKV = 2560
OFF_CKR = 2688
PACKED_W = 2816
MIX_W = 1024

PROJ_TM = 512
ATTN_TQ = 256
OUT_TM = 512
VMEM_LIMIT = 56 * 1024 * 1024


def _rope_tables(s_len):
    t = jnp.arange(s_len)
    lane = jnp.arange(LANES)

    def cs(pos, dim):
        inv = ROPE_THETA ** (-jnp.arange(0, dim, 2, dtype=F32) / dim)
        ang = pos.astype(F32)[:, None] * inv[None, :]
        ang = jnp.concatenate([ang, ang], axis=-1)
        return jnp.cos(ang), jnp.sin(ang)

    def split(cos, sin, low):
        return jnp.stack([cos, jnp.where(low[None, :], -sin, 0.0), jnp.where(low[None, :], 0.0, sin)])

    rc, rs = cs(t // GRID_W, HEAD_DIM // 2)
    cc, csn = cs(t % GRID_W, HEAD_DIM // 2)
    cos_a = jnp.tile(jnp.concatenate([rc, cc], -1), (1, 2))
    sin_a = jnp.tile(jnp.concatenate([rs, csn], -1), (1, 2))
    tab_a = split(cos_a, sin_a, (lane % 32) < 16)
    c1, s1 = cs(t, HEAD_DIM)
    tab_b = split(jnp.tile(c1, (1, 2)), jnp.tile(s1, (1, 2)), (lane % 64) < 32)
    cm, sm = cs(t, C_ROPE)
    ones = jnp.ones((s_len, C_NOPE), F32)
    zeros = jnp.zeros((s_len, C_NOPE), F32)
    pad = jnp.zeros((s_len, LANES - C_NOPE - C_ROPE), F32)
    cos_c = jnp.concatenate([ones, cm, pad + 1.0], -1)
    sin_c = jnp.concatenate([zeros, sm, pad], -1)
    tab_c = split(cos_c, sin_c, ((lane - C_NOPE) % 32) < 16)
    return jnp.concatenate([tab_a, tab_b, tab_c], 0)


def _pack_w_in(w):
    d = w.shape[0]
    aq, ak, av, ag = w[:, 0:384], w[:, 384:512], w[:, 512:640], w[:, 640:1024]
    bq, bk, bv, bg = w[:, 1024:1408], w[:, 1408:1536], w[:, 1536:1664], w[:, 1664:2048]
    cq, ckv, ckr, cg = w[:, 2048:2240], w[:, 2240:2368], w[:, 2368:2400], w[:, 2400:2656]
    z = lambda n: jnp.zeros((d, n), w.dtype)
    packed = jnp.concatenate(
        [aq, bq, ak, bk, av, bv, ag, bg, cg,
         cq, z(C_Q_PAD - C_Q_RANK),
         ckv,
         z(C_NOPE), ckr, z(LANES - C_NOPE - C_ROPE)], axis=1)
    assert packed.shape[1] == PACKED_W
    return packed.astype(BF16)


def _pack_w_uq(w):
    w = w.reshape(C_Q_RANK, C_HEADS, C_NOPE + C_ROPE)
    w = jnp.pad(w, ((0, C_Q_PAD - C_Q_RANK), (0, 0), (0, LANES - C_NOPE - C_ROPE)))
    return w.reshape(C_Q_PAD, C_HEADS * LANES).astype(BF16)


def _pack_w_ukv(w):
    w = w.reshape(C_KV_RANK, C_HEADS, C_NOPE + C_V)
    wk = jnp.pad(w[:, :, :C_NOPE], ((0, 0), (0, 0), (0, LANES - C_NOPE)))
    wv = w[:, :, C_NOPE:]
    return jnp.concatenate([wk.reshape(C_KV_RANK, C_HEADS * LANES),
                            wv.reshape(C_KV_RANK, C_HEADS * C_V)], axis=1).astype(BF16)


def _group_sum_matrix():
    i = jnp.arange(LANES)
    return ((i[:, None] // HALF) == (i[None, :] // HALF)).astype(BF16)


def _proj_body(x_ref, g_ref, w_ref, aqn_ref, akn_ref, cqn_ref, ckvn_ref, wuq_ref, wukv_ref, e_ref, tab_ref,
               qab_ref, ka_ref, va_ref, kb_ref, vb_ref, sg_ref, qc_ref, kc_ref, vc_ref):
    tm = x_ref.shape[0]
    x = x_ref[...]
    h = x * lax.rsqrt(jnp.mean(x * x, axis=-1, keepdims=True) + EPS) * g_ref[...]
    z = jnp.dot(h.astype(BF16), w_ref[...], preferred_element_type=F32)

    lane = lax.broadcasted_iota(jnp.int32, (tm, LANES), 1)
    low = lane < HALF

    def rope(xs, mixer, shift):
        c, sn, sp = tab_ref[3 * mixer], tab_ref[3 * mixer + 1], tab_ref[3 * mixer + 2]
        return xs * c + pltpu.roll(xs, LANES - shift, 1) * sn + pltpu.roll(xs, shift, 1) * sp

    def head_norm(xs, gain):
        t = xs * xs
        t_hi = t.astype(BF16)
        t_lo = (t - t_hi.astype(F32)).astype(BF16)
        ss = (jnp.dot(t_hi, e_ref[...], preferred_element_type=F32)
              + jnp.dot(t_lo, e_ref[...], preferred_element_type=F32))
        return xs * lax.rsqrt(ss * (1.0 / HEAD_DIM) + EPS) * gain

    def slab(off):
        return z[:, off:off + LANES]

    def pair_variants(ref, s):
        sw = pltpu.roll(s, HALF, 1)
        ref[0] = jnp.where(low, s, sw).astype(BF16)
        ref[1] = s.astype(BF16)
        ref[2] = jnp.where(low, sw, s).astype(BF16)

    scale_ab = HEAD_DIM ** -0.5
    for j in range(A_HEADS // 2):
        q = rope(head_norm(slab(OFF_QA + j * LANES), aqn_ref[...]), 0, 16)
        qab_ref[:, j * LANES:(j + 1) * LANES] = (q * scale_ab).astype(BF16)
    for j in range(B_HEADS // 2):
        q = rope(slab(OFF_QB + j * LANES), 1, 32)
        qab_ref[:, OFF_QB + j * LANES:OFF_QB + (j + 1) * LANES] = (q * scale_ab).astype(BF16)
    pair_variants(ka_ref, rope(head_norm(slab(OFF_KA), akn_ref[...]), 0, 16))
    pair_variants(kb_ref, rope(slab(OFF_KB), 1, 32))
    pair_variants(va_ref, slab(OFF_VA))
    pair_variants(vb_ref, slab(OFF_VB))

    gz = z[:, OFF_G:OFF_G + MIX_W]
    sg_ref[...] = (gz / (1.0 + jnp.exp(-gz))).astype(BF16)

    cq = z[:, OFF_CQ:OFF_CQ + C_Q_PAD]
    cq = cq * lax.rsqrt(jnp.sum(cq * cq, axis=-1, keepdims=True) * (1.0 / C_Q_RANK) + EPS) * cqn_ref[...]
    qc = jnp.dot(cq.astype(BF16), wuq_ref[...], preferred_element_type=F32)
    scale_c = (C_NOPE + C_ROPE) ** -0.5
    for hh in range(C_HEADS):
        q = rope(qc[:, hh * LANES:(hh + 1) * LANES], 2, 16)
        qc_ref[:, hh * LANES:(hh + 1) * LANES] = (q * scale_c).astype(BF16)
    ckv = z[:, OFF_CKV:OFF_CKV + C_KV_RANK]
    ckv = ckv * lax.rsqrt(jnp.mean(ckv * ckv, axis=-1, keepdims=True) + EPS) * ckvn_ref[...]
    kv = jnp.dot(ckv.astype(BF16), wukv_ref[...], preferred_element_type=F32)
    kr = rope(slab(OFF_CKR), 2, 16)
    for hh in range(C_HEADS):
        kc_ref[hh] = (kv[:, hh * LANES:(hh + 1) * LANES] + kr).astype(BF16)
    vc_ref[...] = kv[:, C_HEADS * LANES:].astype(BF16)


def _proj_call(x, g, w_packed, aqn, akn, cqn, ckvn, wuq, wukv, e_mat, tabs):
    bsz, s_len, d = x.shape
    tm = min(PROJ_TM, s_len)
    grid = (bsz, s_len // tm)
    full = lambda a: pl.BlockSpec(a.shape, lambda b, s: (0,) * a.ndim)
    tok = lambda w: pl.BlockSpec((None, tm, w), lambda b, s: (b, s, 0))
    var = lambda n: pl.BlockSpec((None, n, tm, LANES), lambda b, s: (b, 0, s, 0))
    bf = lambda *shape: jax.ShapeDtypeStruct(shape, BF16)
    return pl.pallas_call(
        _proj_body,
        grid=grid,
        in_specs=[tok(d), full(g), full(w_packed), full(aqn), full(akn), full(cqn), full(ckvn),
                  full(wuq), full(wukv), full(e_mat),
                  pl.BlockSpec((9, tm, LANES), lambda b, s: (0, s, 0))],
        out_specs=[tok(768), var(3), var(3), var(3), var(3), tok(MIX_W), tok(C_HEADS * LANES),
                   var(C_HEADS), tok(C_HEADS * C_V)],
        out_shape=[bf(bsz, s_len, 768), bf(bsz, 3, s_len, LANES), bf(bsz, 3, s_len, LANES),
                   bf(bsz, 3, s_len, LANES), bf(bsz, 3, s_len, LANES), bf(bsz, s_len, MIX_W),
                   bf(bsz, s_len, C_HEADS * LANES), bf(bsz, C_HEADS, s_len, LANES),
                   bf(bsz, s_len, C_HEADS * C_V)],
        compiler_params=pltpu.CompilerParams(
            dimension_semantics=("arbitrary", "arbitrary"), vmem_limit_bytes=VMEM_LIMIT),
        name="proj",
    )(x, g, w_packed, aqn, akn, cqn, ckvn, wuq, wukv, e_mat, tabs)


def _fill_vaug(v_ref, vaug_ref):
    @pl.when(pl.program_id(2) == 0)
    def _():
        vaug_ref[:, :LANES] = v_ref[...]
        vaug_ref[:, LANES:] = jnp.ones((vaug_ref.shape[0], LANES), BF16)


def _split_heads(q_ref):
    q = q_ref[...].astype(F32)
    low = lax.broadcasted_iota(jnp.int32, q.shape, 1) < HALF
    return jnp.where(low, q, 0.0).astype(BF16), jnp.where(low, 0.0, q).astype(BF16), low


def _scores(q, k):
    return lax.dot_general(q, k, (((1,), (1,)), ((), ())), preferred_element_type=F32)


def _dense_pipe_body(*refs, split, n_q):
    if split:
        q_ref, k_ref, v_ref, sg_ref, o_ref, s_a, s_b, m_a, m_b, vaug_ref = refs
        q0, q1, _ = _split_heads(q_ref)
        qs, ks = (q0, q1), (k_ref, k_ref)
    else:
        q0_ref, q1_ref, k0_ref, k1_ref, v_ref, sg_ref, o_ref, s_a, s_b, m_a, m_b, vaug_ref = refs
        qs, ks = (q0_ref[...], q1_ref[...]), (k0_ref, k1_ref)
    t = pl.program_id(0)
    tq, s_len = s_a.shape[1], s_a.shape[2]

    @pl.when(lax.rem(jnp.maximum(t - 1, 0), n_q) == 0)
    def _():
        vaug_ref[:, :LANES] = v_ref[...]
        vaug_ref[:, LANES:] = jnp.ones((s_len, LANES), BF16)

    @pl.when(t == 0)
    def _():
        s_b[...] = jnp.zeros(s_b.shape, F32)
        m_b[...] = jnp.zeros(m_b.shape, F32)

    low = lax.broadcasted_iota(jnp.int32, (tq, LANES), 1) < HALF

    def step(s_w, m_w, s_r, m_r):
        for hd in range(2):
            s = _scores(qs[hd], ks[hd][...])
            s_w[hd] = s
            m_w[hd] = jnp.broadcast_to(jnp.max(s, axis=1, keepdims=True), (tq, LANES))
        outs = []
        for hd in range(2):
            m = m_r[hd]
            p = jnp.concatenate(
                [jnp.exp(s_r[hd, :, j * LANES:(j + 1) * LANES] - m).astype(BF16) for j in range(s_len // LANES)],
                axis=1)
            a = jnp.dot(p, vaug_ref[...], preferred_element_type=F32)
            outs.append(a[:, :LANES] / a[:, LANES:])
        o_ref[...] = (jnp.where(low, outs[0], outs[1]) * sg_ref[...].astype(F32)).astype(BF16)

    parity = lax.rem(t, 2)

    @pl.when(parity == 0)
    def _():
        step(s_a, m_a, s_b, m_b)

    @pl.when(parity == 1)
    def _():
        step(s_b, m_b, s_a, m_a)


def _dense_call(name, q_all, q_slabs, k_all, k_idx, v_all, v_spec_fn, sg, g_off, n_pairs):
    bsz, s_len, _ = q_all.shape
    tq = min(ATTN_TQ, s_len)
    n_q = s_len // tq
    n_items = bsz * n_pairs * n_q

    def item(t):
        return t // (n_pairs * n_q), (t // n_q) % n_pairs, t % n_q

    cur = lambda t: item(jnp.minimum(t, n_items - 1))
    prev = lambda t: item(jnp.maximum(t - 1, 0))

    def q_spec(f):
        def imap(t):
            b, p, i = cur(t)
            return b, i, f(p)
        return pl.BlockSpec((None, tq, LANES), imap)

    def k_spec(f):
        def imap(t):
            b, p, i = cur(t)
            return b, f(p), 0, 0
        return pl.BlockSpec((None, None, s_len, LANES), imap)

    def sg_map(t):
        b, p, i = prev(t)
        return b, i, g_off + p

    def o_map(t):
        b, p, i = prev(t)
        return b, i, p

    split = len(q_slabs) == 1
    in_specs = [q_spec(f) for f in q_slabs] + [k_spec(f) for f in k_idx]
    in_specs += [v_spec_fn(prev), pl.BlockSpec((None, tq, LANES), sg_map)]
    args = [q_all] * len(q_slabs) + [k_all] * len(k_idx) + [v_all, sg]
    return pl.pallas_call(
        functools.partial(_dense_pipe_body, split=split, n_q=n_q),
        grid=(n_items + 1,),
        in_specs=in_specs,
        out_specs=pl.BlockSpec((None, tq, LANES), o_map),
        out_shape=jax.ShapeDtypeStruct((bsz, s_len, n_pairs * LANES), BF16),
        scratch_shapes=[pltpu.VMEM((2, tq, s_len), F32), pltpu.VMEM((2, tq, s_len), F32),
                        pltpu.VMEM((2, tq, LANES), F32), pltpu.VMEM((2, tq, LANES), F32),
                        pltpu.VMEM((s_len, 2 * LANES), BF16)],
        compiler_params=pltpu.CompilerParams(dimension_semantics=("arbitrary",), vmem_limit_bytes=VMEM_LIMIT),
        name=name,
    )(*args)


def _attn_a_call(qab, ka, va, sg):
    s_len = qab.shape[1]

    def v_spec(prev):
        def imap(t):
            b, p, i = prev(t)
            return b, p, 0, 0
        return pl.BlockSpec((None, None, s_len, LANES), imap)

    return _dense_call("attn_a", qab, [lambda p: p], ka, [lambda p: p], va, v_spec, sg, 0, 3)


def _attn_c_call(qc, kc, vc, sg, g_off):
    s_len = qc.shape[1]

    def v_spec(prev):
        def imap(t):
            b, p, i = prev(t)
            return b, 0, p
        return pl.BlockSpec((None, s_len, LANES), imap)

    return _dense_call("attn_c", qc, [lambda p: 2 * p, lambda p: 2 * p + 1], kc,
                       [lambda p: 2 * p, lambda p: 2 * p + 1], vc, v_spec, sg, g_off, C_HEADS // 2)


def _attn_b_body(sink_ref, q_ref, k_ref, v_ref, sg_ref, o_ref, vaug_ref):
    _fill_vaug(v_ref, vaug_ref)
    tq = q_ref.shape[0]
    s_len = k_ref.shape[0]
    kw = min(tq + 2 * WINDOW, s_len)
    pair = pl.program_id(1)
    qi = pl.program_id(2)
    ws = pl.multiple_of(jnp.clip(qi * tq - WINDOW, 0, s_len - kw), WINDOW)
    k_win = k_ref[pl.ds(ws, kw), :]
    v_win = vaug_ref[pl.ds(ws, kw), :]
    q_pos = qi * tq + lax.broadcasted_iota(jnp.int32, (tq, kw), 0)
    k_pos = ws + lax.broadcasted_iota(jnp.int32, (tq, kw), 1)
    valid = jnp.abs(q_pos - k_pos) <= WINDOW
    q0, q1, low = _split_heads(q_ref)

    def head(q, sink):
        s = jnp.where(valid, _scores(q, k_win), NEG)
        m = jnp.maximum(jnp.max(s, axis=1, keepdims=True), sink)
        p = jnp.exp(s - m).astype(BF16)
        a = jnp.dot(p, v_win, preferred_element_type=F32)
        return a[:, :LANES] / (a[:, LANES:] + jnp.exp(sink - m))

    o = jnp.where(low, head(q0, sink_ref[2 * pair]), head(q1, sink_ref[2 * pair + 1]))
    o_ref[...] = (o * sg_ref[...].astype(F32)).astype(BF16)


def _attn_params():
    return pltpu.CompilerParams(dimension_semantics=("arbitrary",) * 3, vmem_limit_bytes=VMEM_LIMIT)


def _attn_ab_call(body, name, q_all, q_off, k_var, v_var, sg, g_off, sink=None):
    bsz, s_len, _ = q_all.shape
    tq = min(ATTN_TQ, s_len)
    n_pairs = 3
    grid = (bsz, n_pairs, s_len // tq)
    in_specs = [
        pl.BlockSpec((None, tq, LANES), lambda b, p, i: (b, i, q_off + p)),
        pl.BlockSpec((None, None, s_len, LANES), lambda b, p, i: (b, p, 0, 0)),
        pl.BlockSpec((None, None, s_len, LANES), lambda b, p, i: (b, p, 0, 0)),
        pl.BlockSpec((None, tq, LANES), lambda b, p, i: (b, i, g_off + p)),
    ]
    args = [q_all, k_var, v_var, sg]
    if sink is not None:
        in_specs = [pl.BlockSpec(memory_space=pltpu.SMEM)] + in_specs
        args = [sink] + args
    return pl.pallas_call(
        body,
        grid=grid,
        in_specs=in_specs,
        out_specs=pl.BlockSpec((None, tq, LANES), lambda b, p, i: (b, i, p)),
        out_shape=jax.ShapeDtypeStruct((bsz, s_len, n_pairs * LANES), BF16),
        scratch_shapes=[pltpu.VMEM((s_len, 2 * LANES), BF16)],
        compiler_params=_attn_params(),
        name=name,
    )(*args)


def _out_body(x_ref, oa_ref, ob_ref, oc_ref, wa_ref, wb_ref, wc_ref, fg_ref, y_ref, *, final):
    y = (jnp.dot(oa_ref[...], wa_ref[...], preferred_element_type=F32)
         + jnp.dot(ob_ref[...], wb_ref[...], preferred_element_type=F32)
         + jnp.dot(oc_ref[...], wc_ref[...], preferred_element_type=F32))
    r = x_ref[...] + y
    if final:
        r = r * lax.rsqrt(jnp.mean(r * r, axis=-1, keepdims=True) + EPS) * fg_ref[...]
    y_ref[...] = r


def _out_call(x, oa, ob, oc, w_out, fg, final):
    bsz, s_len, d = x.shape
    tm = min(OUT_TM, s_len)
    wa, wb, wc = w_out[:384], w_out[384:768], w_out[768:]
    tok = lambda w: pl.BlockSpec((None, tm, w), lambda b, s: (b, s, 0))
    full = lambda a: pl.BlockSpec(a.shape, lambda b, s: (0,) * a.ndim)
    return pl.pallas_call(
        functools.partial(_out_body, final=final),
        grid=(bsz, s_len // tm),
        in_specs=[tok(d), tok(384), tok(384), tok(256), full(wa), full(wb), full(wc), full(fg)],
        out_specs=tok(d),
        out_shape=jax.ShapeDtypeStruct(x.shape, F32),
        compiler_params=pltpu.CompilerParams(
            dimension_semantics=("arbitrary", "arbitrary"), vmem_limit_bytes=VMEM_LIMIT),
        name="out_final" if final else "out",
    )(x, oa, ob, oc, wa, wb, wc, fg)


def _layer(x, g, w_in, aqn, akn, sink, cqn, ckvn, wuq, wukv, w_out, fg, e_mat, tabs, final):
    two = lambda v: jnp.tile(v.astype(F32), 2)[None, :]
    qab, ka, va, kb, vb, sg, qc, kc, vc = _proj_call(
        x, g.astype(F32)[None, :], _pack_w_in(w_in), two(aqn), two(akn),
        jnp.pad(cqn.astype(F32), (0, C_Q_PAD - C_Q_RANK))[None, :], ckvn.astype(F32)[None, :],
        _pack_w_uq(wuq), _pack_w_ukv(wukv), e_mat, tabs)
    oa = _attn_a_call(qab, ka, va, sg)
    ob = _attn_ab_call(_attn_b_body, "attn_b", qab, 3, kb, vb, sg, 3, sink=sink.astype(F32))
    oc = _attn_c_call(qc, kc, vc, sg, 6)
    return _out_call(x, oa, ob, oc, w_out.astype(BF16), fg.astype(F32)[None, :], final)


def kernel(x, norm_g, w_in, a_q_norm, a_k_norm, b_sink, c_q_norm, c_kv_norm, c_w_uq, c_w_ukv, w_out, final_g):
    depth = w_in.shape[0]
    tabs = _rope_tables(x.shape[1])
    e_mat = _group_sum_matrix()
    h = x
    for l in range(depth):
        h = _layer(h, norm_g[l], w_in[l], a_q_norm[l], a_k_norm[l], b_sink[l], c_q_norm[l], c_kv_norm[l],
                   c_w_uq[l], c_w_ukv[l], w_out[l], final_g, e_mat, tabs, final=(l == depth - 1))
    return h
```

```python
import functools

import numpy as np
import jax
import jax.numpy as jnp
from jax import lax
from jax.experimental import pallas as pl
from jax.experimental.pallas import tpu as pltpu

F32 = jnp.float32
BF16 = jnp.bfloat16

GRID_W = 64
ROPE_THETA = 10000.0
EPS = 1e-6
HEAD_DIM = 64
WINDOW = 128
AB_HEADS = 6
C_HEADS = 4
C_NOPE = 64
C_ROPE = 32
C_V = 64
C_Q_RANK = 192
C_KV_RANK = 128

LANES = 128
HALF = LANES // 2
QUARTER = LANES // 4
N_SLABS = AB_HEADS // 2
C_Q_PAD = 256
NEG = -1e30

IN_AQ, IN_AK, IN_AV, IN_AG = 0, 384, 512, 640
IN_BQ, IN_BK, IN_BV, IN_BG = 1024, 1408, 1536, 1664
IN_CQ, IN_CKV, IN_CKR, IN_CG = 2048, 2240, 2368, 2400

OFF_QA, OFF_QB = 0, 384
OFF_KA, OFF_KB = 768, 896
OFF_VA, OFF_VB = 1024, 1152
OFF_G = 1280
OFF_CQ = 2304
OFF_CKV = 2560
OFF_CKR = 2688
PACKED_W = 2816
MIX_W = 1024

PROJ_TM = 512
ATTN_A_TQ = 128
ATTN_B_TQ = 256
ATTN_C_TQ = 512
OUT_TM = 512
VMEM_LIMIT = 56 * 1024 * 1024


_LANE = np.arange(LANES)
_HEAD_OF_LANE = (_LANE // QUARTER) % 2


def _dim_of_lane(first_half_dims, partner):
    f = np.asarray(first_half_dims)
    return np.where(_LANE < HALF, f[_LANE % QUARTER], f[_LANE % QUARTER] + partner)


_DIM_A = _dim_of_lane(list(range(16)) + list(range(32, 48)), 16)
_DIM_B = _dim_of_lane(list(range(32)), 32)

_C_ROPE_OF_LANE = np.full(LANES, -1)
_C_ROPE_OF_LANE[0:16] = np.arange(16)
_C_ROPE_OF_LANE[64:80] = 16 + np.arange(16)
_C_NOPE_OF_LANE = np.full(LANES, -1)
_C_NOPE_OF_LANE[16:64] = np.arange(48)
_C_NOPE_OF_LANE[80:96] = 48 + np.arange(16)


def _slab_heads(p):
    return np.where(_HEAD_OF_LANE == 0, p, p + N_SLABS)


def _w_in_columns():
    cols = []
    for base, dim in ((IN_AQ, _DIM_A), (IN_BQ, _DIM_B)):
        for p in range(N_SLABS):
            cols.append(base + _slab_heads(p) * HEAD_DIM + dim)
    cols.append(IN_AK + _HEAD_OF_LANE * HEAD_DIM + _DIM_A)
    cols.append(IN_BK + _HEAD_OF_LANE * HEAD_DIM + _DIM_B)
    cols.append(IN_AV + _LANE)
    cols.append(IN_BV + _LANE)
    for base in (IN_AG, IN_BG):
        for p in range(N_SLABS):
            cols.append(base + np.concatenate([p * HEAD_DIM + np.arange(HEAD_DIM),
                                               (p + N_SLABS) * HEAD_DIM + np.arange(HEAD_DIM)]))
    cols.append(IN_CG + np.arange(C_HEADS * C_V))
    cols.append(np.concatenate([IN_CQ + np.arange(C_Q_RANK), np.full(C_Q_PAD - C_Q_RANK, -1)]))
    cols.append(IN_CKV + np.arange(C_KV_RANK))
    cols.append(np.where(_C_ROPE_OF_LANE >= 0, IN_CKR + _C_ROPE_OF_LANE, -1))
    cols = np.concatenate(cols)
    assert cols.shape == (PACKED_W,)
    return cols


def _w_out_rows():
    rows = []
    for base in (0, AB_HEADS * HEAD_DIM):
        for p in range(N_SLABS):
            rows.append(base + np.concatenate([p * HEAD_DIM + np.arange(HEAD_DIM),
                                               (p + N_SLABS) * HEAD_DIM + np.arange(HEAD_DIM)]))
    rows.append(2 * AB_HEADS * HEAD_DIM + np.arange(C_HEADS * C_V))
    return np.concatenate(rows)


def _take_cols(w, cols):
    picked = jnp.take(w, jnp.asarray(np.maximum(cols, 0)), axis=1)
    return jnp.where(jnp.asarray(cols >= 0)[None, :], picked, 0.0)


def _pack_w_uq(w):
    per_head = np.where(_C_ROPE_OF_LANE >= 0, C_NOPE + _C_ROPE_OF_LANE, _C_NOPE_OF_LANE)
    cols = np.concatenate([np.where(per_head >= 0, h * (C_NOPE + C_ROPE) + per_head, -1) for h in range(C_HEADS)])
    return jnp.pad(_take_cols(w, cols), ((0, C_Q_PAD - C_Q_RANK), (0, 0))).astype(BF16)


def _pack_w_ukv(w):
    k_cols = np.concatenate([np.where(_C_NOPE_OF_LANE >= 0, h * (C_NOPE + C_V) + _C_NOPE_OF_LANE, -1)
                             for h in range(C_HEADS)])
    v_cols = np.concatenate([h * (C_NOPE + C_V) + C_NOPE + np.arange(C_V) for h in range(C_HEADS)])
    return _take_cols(w, np.concatenate([k_cols, v_cols])).astype(BF16)


def _group_sum_matrix():
    head = np.concatenate([_HEAD_OF_LANE, 2 + _HEAD_OF_LANE])
    return jnp.asarray(head[:, None] == head[None, :], BF16)


def _rope_tables(s_len):
    t = jnp.arange(s_len)

    def cs(pos, dim):
        inv = ROPE_THETA ** (-jnp.arange(0, dim, 2, dtype=F32) / dim)
        ang = pos.astype(F32)[:, None] * inv[None, :]
        ang = jnp.concatenate([ang, ang], axis=-1)
        return jnp.cos(ang), jnp.sin(ang)

    sign = jnp.asarray(np.where(_LANE < HALF, -1.0, 1.0), F32)[None, :]
    rc, rs = cs(t // GRID_W, HEAD_DIM // 2)
    cc, csn = cs(t % GRID_W, HEAD_DIM // 2)
    cos_a, sin_a = jnp.concatenate([rc, cc], -1), jnp.concatenate([rs, csn], -1)
    cos_b, sin_b = cs(t, HEAD_DIM)
    cos_m, sin_m = cs(t, C_ROPE)
    is_rope = jnp.asarray(_C_ROPE_OF_LANE >= 0)[None, :]
    rope_idx = np.maximum(_C_ROPE_OF_LANE, 0)
    return jnp.stack([
        cos_a[:, _DIM_A], sin_a[:, _DIM_A] * sign,
        cos_b[:, _DIM_B], sin_b[:, _DIM_B] * sign,
        jnp.where(is_rope, cos_m[:, rope_idx], 1.0), jnp.where(is_rope, sin_m[:, rope_idx] * sign, 0.0)])


def _proj_body(x_ref, g_ref, w_ref, aqn_ref, akn_ref, cqn_ref, ckvn_ref, wuq_ref, wukv_ref, e_ref, tab_ref,
               qab_ref, ka_ref, va_ref, kb_ref, vb_ref, sg_ref, qc_ref, kc_ref, vc_ref):
    x = x_ref[...]
    h = x * lax.rsqrt(jnp.mean(x * x, axis=-1, keepdims=True) + EPS) * g_ref[...]
    z = jnp.dot(h.astype(BF16), w_ref[...], preferred_element_type=F32)

    def rope(xs, mixer):
        return xs * tab_ref[2 * mixer] + pltpu.roll(xs, HALF, 1) * tab_ref[2 * mixer + 1]

    def slab(off):
        return z[:, off:off + LANES]

    def head_sumsq(s0, s1):
        t = jnp.concatenate([s0 * s0, s1 * s1], axis=1).astype(BF16)
        ss = jnp.dot(t, e_ref[...], preferred_element_type=F32)
        return ss[:, :LANES], ss[:, LANES:]

    def head_norm(xs, ss, gain):
        return xs * lax.rsqrt(ss * (1.0 / HEAD_DIM) + EPS) * gain

    scale_ab = HEAD_DIM ** -0.5
    qa = [slab(OFF_QA + j * LANES) for j in range(N_SLABS)]
    ka = slab(OFF_KA)
    ss0, ss1 = head_sumsq(qa[0], qa[1])
    ss2, ssk = head_sumsq(qa[2], ka)
    for j, ss in enumerate((ss0, ss1, ss2)):
        q = rope(head_norm(qa[j], ss, aqn_ref[...]), 0)
        qab_ref[:, OFF_QA + j * LANES:OFF_QA + (j + 1) * LANES] = (q * scale_ab).astype(BF16)
    ka_ref[...] = rope(head_norm(ka, ssk, akn_ref[...]), 0).astype(BF16)
    for j in range(N_SLABS):
        q = rope(slab(OFF_QB + j * LANES), 1)
        qab_ref[:, OFF_QB + j * LANES:OFF_QB + (j + 1) * LANES] = (q * scale_ab).astype(BF16)
    kb_ref[...] = rope(slab(OFF_KB), 1).astype(BF16)
    va_ref[...] = slab(OFF_VA).astype(BF16)
    vb_ref[...] = slab(OFF_VB).astype(BF16)

    gz = z[:, OFF_G:OFF_G + MIX_W]
    sg_ref[...] = (gz / (1.0 + jnp.exp(-gz))).astype(BF16)

    cq = z[:, OFF_CQ:OFF_CQ + C_Q_PAD]
    cq = cq * lax.rsqrt(jnp.sum(cq * cq, axis=-1, keepdims=True) * (1.0 / C_Q_RANK) + EPS) * cqn_ref[...]
    qc = jnp.dot(cq.astype(BF16), wuq_ref[...], preferred_element_type=F32)
    scale_c = (C_NOPE + C_ROPE) ** -0.5
    for hh in range(C_HEADS):
        q = rope(qc[:, hh * LANES:(hh + 1) * LANES], 2)
        qc_ref[:, hh * LANES:(hh + 1) * LANES] = (q * scale_c).astype(BF16)
    ckv = z[:, OFF_CKV:OFF_CKV + C_KV_RANK]
    ckv = ckv * lax.rsqrt(jnp.mean(ckv * ckv, axis=-1, keepdims=True) + EPS) * ckvn_ref[...]
    kv = jnp.dot(ckv.astype(BF16), wukv_ref[...], preferred_element_type=F32)
    kr = rope(slab(OFF_CKR), 2)
    for hh in range(C_HEADS):
        kc_ref[hh] = (kv[:, hh * LANES:(hh + 1) * LANES] + kr).astype(BF16)
    vc_ref[...] = kv[:, C_HEADS * LANES:].astype(BF16)


def _proj_call(x, g, w_packed, aqn, akn, cqn, ckvn, wuq, wukv, e_mat, tabs):
    bsz, s_len, d = x.shape
    tm = min(PROJ_TM, s_len)
    grid = (bsz, s_len // tm)
    full = lambda a: pl.BlockSpec(a.shape, lambda b, s: (0,) * a.ndim)
    tok = lambda w: pl.BlockSpec((None, tm, w), lambda b, s: (b, s, 0))
    bf = lambda *shape: jax.ShapeDtypeStruct(shape, BF16)
    widths = [2 * N_SLABS * LANES, LANES, LANES, LANES, LANES, MIX_W, C_HEADS * LANES]
    return pl.pallas_call(
        _proj_body,
        grid=grid,
        in_specs=[tok(d), full(g), full(w_packed), full(aqn), full(akn), full(cqn), full(ckvn),
                  full(wuq), full(wukv), full(e_mat),
                  pl.BlockSpec((6, tm, LANES), lambda b, s: (0, s, 0))],
        out_specs=[tok(w) for w in widths]
                  + [pl.BlockSpec((None, C_HEADS, tm, LANES), lambda b, s: (b, 0, s, 0)), tok(C_HEADS * C_V)],
        out_shape=[bf(bsz, s_len, w) for w in widths]
                  + [bf(bsz, C_HEADS, s_len, LANES), bf(bsz, s_len, C_HEADS * C_V)],
        compiler_params=pltpu.CompilerParams(
            dimension_semantics=("arbitrary", "arbitrary"), vmem_limit_bytes=VMEM_LIMIT),
        name="proj",
    )(x, g, w_packed, aqn, akn, cqn, ckvn, wuq, wukv, e_mat, tabs)


def _scores(q, k):
    return lax.dot_general(q, k, (((1,), (1,)), ((), ())), preferred_element_type=F32)


def _pipeline_prologue(t, n_q, v_ref, vaug_ref, s_b, m_b):
    @pl.when(lax.rem(jnp.maximum(t - 1, 0), n_q) == 0)
    def _():
        vaug_ref[:, :LANES] = v_ref[...]
        vaug_ref[:, LANES:] = jnp.ones((vaug_ref.shape[0], LANES), BF16)

    @pl.when(t == 0)
    def _():
        s_b[...] = jnp.zeros(s_b.shape, F32)
        m_b[...] = jnp.zeros(m_b.shape, F32)


def _run_parity(t, step, s_a, m_a, s_b, m_b):
    parity = lax.rem(t, 2)

    @pl.when(parity == 0)
    def _():
        step(s_a, m_a, s_b, m_b)

    @pl.when(parity == 1)
    def _():
        step(s_b, m_b, s_a, m_a)


def _softmax_numerators(s_r, m_r, rows, width):
    m = m_r[rows, :]
    return jnp.concatenate(
        [jnp.exp(s_r[rows, j * LANES:(j + 1) * LANES] - m).astype(BF16) for j in range(width // LANES)], axis=1)


def _stack_pipe_body(*refs, n_q, n_items, windowed):
    if windowed:
        sink_ref, q_ref, k_ref, v_ref, sg_ref, o_ref, s_a, s_b, m_a, m_b, vaug_ref = refs
    else:
        q_ref, k_ref, v_ref, sg_ref, o_ref, s_a, s_b, m_a, m_b, vaug_ref = refs
    t = pl.program_id(0)
    tq = q_ref.shape[0]
    s_len = k_ref.shape[0]
    kw = s_a.shape[1]
    _pipeline_prologue(t, n_q, v_ref, vaug_ref, s_b, m_b)

    lane = lax.broadcasted_iota(jnp.int32, (tq, LANES), 1)
    head_a = (lane & QUARTER) == 0
    out_low = lane < HALF
    q = q_ref[...].astype(F32)
    slabs = [q[:, j * LANES:(j + 1) * LANES] for j in range(N_SLABS)]
    q_stack = jnp.concatenate([jnp.where(head_a, s, 0.0) for s in slabs]
                              + [jnp.where(head_a, 0.0, s) for s in slabs], axis=0).astype(BF16)
    qi_cur = lax.rem(jnp.minimum(t, n_items - 1), n_q)
    qi_prev = lax.rem(jnp.maximum(t - 1, 0), n_q)
    blocks = [pl.ds(i * tq, tq) for i in range(AB_HEADS)]

    def win_start(qi):
        return pl.multiple_of(jnp.clip(qi * tq - WINDOW, 0, s_len - kw), WINDOW)

    def step(s_w, m_w, s_r, m_r):
        if windowed:
            ws = win_start(qi_cur)
            k = k_ref[pl.ds(ws, kw), :]
            dist = (qi_cur * tq - ws) + (lax.broadcasted_iota(jnp.int32, (tq, kw), 0)
                                         - lax.broadcasted_iota(jnp.int32, (tq, kw), 1))
            valid = jnp.abs(dist) <= WINDOW
        else:
            k = k_ref[...]
        s = _scores(q_stack, k)
        for i in range(AB_HEADS):
            blk = s[i * tq:(i + 1) * tq]
            mx = jnp.max(blk, axis=1, keepdims=True)
            if windowed:
                blk = jnp.where(valid, blk, NEG)
                mx = jnp.maximum(jnp.max(blk, axis=1, keepdims=True), sink_ref[i])
            s_w[blocks[i], :] = blk
            m_w[blocks[i], :] = jnp.broadcast_to(mx, (tq, LANES))
        v_win = vaug_ref[pl.ds(win_start(qi_prev), kw), :] if windowed else vaug_ref[...]
        p = jnp.concatenate([_softmax_numerators(s_r, m_r, blocks[i], kw) for i in range(AB_HEADS)], axis=0)
        a = jnp.dot(p, v_win, preferred_element_type=F32)
        for j in range(N_SLABS):
            parts = []
            for i in (j, j + N_SLABS):
                blk = a[i * tq:(i + 1) * tq]
                den = blk[:, LANES:]
                if windowed:
                    den = den + jnp.exp(sink_ref[i] - m_r[blocks[i], :])
                parts.append(blk[:, :LANES] / den)
            cols = slice(j * LANES, (j + 1) * LANES)
            o_ref[:, cols] = (jnp.where(out_low, parts[0], parts[1]) * sg_ref[:, cols].astype(F32)).astype(BF16)

    _run_parity(t, step, s_a, m_a, s_b, m_b)


def _stack_call(name, qab, k, v, sg, blk, tq, sink=None):
    bsz, s_len, _ = qab.shape
    tq = min(tq, s_len)
    n_q = s_len // tq
    n_items = bsz * n_q
    windowed = sink is not None
    kw = min(tq + 2 * WINDOW, s_len) if windowed else s_len
    cur = lambda t: (jnp.minimum(t, n_items - 1) // n_q, jnp.minimum(t, n_items - 1) % n_q)
    prev = lambda t: (jnp.maximum(t - 1, 0) // n_q, jnp.maximum(t - 1, 0) % n_q)
    width = N_SLABS * LANES
    in_specs = [
        pl.BlockSpec((None, tq, width), lambda t: (*cur(t), blk)),
        pl.BlockSpec((None, s_len, LANES), lambda t: (cur(t)[0], 0, 0)),
        pl.BlockSpec((None, s_len, LANES), lambda t: (prev(t)[0], 0, 0)),
        pl.BlockSpec((None, tq, width), lambda t: (*prev(t), blk)),
    ]
    args = [qab, k, v, sg]
    if windowed:
        in_specs = [pl.BlockSpec(memory_space=pltpu.SMEM)] + in_specs
        args = [sink] + args
    return pl.pallas_call(
        functools.partial(_stack_pipe_body, n_q=n_q, n_items=n_items, windowed=windowed),
        grid=(n_items + 1,),
        in_specs=in_specs,
        out_specs=pl.BlockSpec((None, tq, width), lambda t: (*prev(t), 0)),
        out_shape=jax.ShapeDtypeStruct((bsz, s_len, width), BF16),
        scratch_shapes=[pltpu.VMEM((AB_HEADS * tq, kw), F32), pltpu.VMEM((AB_HEADS * tq, kw), F32),
                        pltpu.VMEM((AB_HEADS * tq, LANES), F32), pltpu.VMEM((AB_HEADS * tq, LANES), F32),
                        pltpu.VMEM((s_len, 2 * LANES), BF16)],
        compiler_params=pltpu.CompilerParams(dimension_semantics=("arbitrary",), vmem_limit_bytes=VMEM_LIMIT),
        name=name,
    )(*args)


def _pair_pipe_body(q0_ref, q1_ref, k0_ref, k1_ref, v_ref, sg_ref, o_ref, s_a, s_b, m_a, m_b, vaug_ref, *, n_q):
    t = pl.program_id(0)
    tq, s_len = s_a.shape[1], s_a.shape[2]
    _pipeline_prologue(t, n_q, v_ref, vaug_ref, s_b, m_b)
    out_low = lax.broadcasted_iota(jnp.int32, (tq, LANES), 1) < HALF
    qs, ks = (q0_ref, q1_ref), (k0_ref, k1_ref)

    def step(s_w, m_w, s_r, m_r):
        for hd in range(2):
            s = _scores(qs[hd][...], ks[hd][...])
            s_w[hd] = s
            m_w[hd] = jnp.broadcast_to(jnp.max(s, axis=1, keepdims=True), (tq, LANES))
        outs = []
        for hd in range(2):
            p = _softmax_numerators(s_r.at[hd], m_r.at[hd], slice(None), s_len)
            a = jnp.dot(p, vaug_ref[...], preferred_element_type=F32)
            outs.append(a[:, :LANES] / a[:, LANES:])
        o_ref[...] = (jnp.where(out_low, outs[0], outs[1]) * sg_ref[...].astype(F32)).astype(BF16)

    _run_parity(t, step, s_a, m_a, s_b, m_b)


def _attn_c_call(qc, kc, vc, sg, g_off):
    bsz, s_len, _ = qc.shape
    tq = min(ATTN_C_TQ, s_len)
    n_q = s_len // tq
    n_pairs = C_HEADS // 2
    n_items = bsz * n_pairs * n_q

    def item(t):
        return t // (n_pairs * n_q), (t // n_q) % n_pairs, t % n_q

    cur = lambda t: item(jnp.minimum(t, n_items - 1))
    prev = lambda t: item(jnp.maximum(t - 1, 0))

    def q_spec(o):
        def imap(t):
            b, p, i = cur(t)
            return b, i, 2 * p + o
        return pl.BlockSpec((None, tq, LANES), imap)

    def k_spec(o):
        def imap(t):
            b, p, i = cur(t)
            return b, 2 * p + o, 0, 0
        return pl.BlockSpec((None, None, s_len, LANES), imap)

    def v_map(t):
        b, p, i = prev(t)
        return b, 0, p

    def sg_map(t):
        b, p, i = prev(t)
        return b, i, g_off + p

    def o_map(t):
        b, p, i = prev(t)
        return b, i, p

    return pl.pallas_call(
        functools.partial(_pair_pipe_body, n_q=n_q),
        grid=(n_items + 1,),
        in_specs=[q_spec(0), q_spec(1), k_spec(0), k_spec(1),
                  pl.BlockSpec((None, s_len, LANES), v_map), pl.BlockSpec((None, tq, LANES), sg_map)],
        out_specs=pl.BlockSpec((None, tq, LANES), o_map),
        out_shape=jax.ShapeDtypeStruct((bsz, s_len, n_pairs * LANES), BF16),
        scratch_shapes=[pltpu.VMEM((2, tq, s_len), F32), pltpu.VMEM((2, tq, s_len), F32),
                        pltpu.VMEM((2, tq, LANES), F32), pltpu.VMEM((2, tq, LANES), F32),
                        pltpu.VMEM((s_len, 2 * LANES), BF16)],
        compiler_params=pltpu.CompilerParams(dimension_semantics=("arbitrary",), vmem_limit_bytes=VMEM_LIMIT),
        name="attn_c",
    )(qc, qc, kc, kc, vc, sg)


def _out_body(x_ref, oa_ref, ob_ref, oc_ref, w_ref, fg_ref, y_ref, *, final):
    o = jnp.concatenate([oa_ref[...], ob_ref[...], oc_ref[...]], axis=1)
    r = x_ref[...] + jnp.dot(o, w_ref[...], preferred_element_type=F32)
    if final:
        r = r * lax.rsqrt(jnp.mean(r * r, axis=-1, keepdims=True) + EPS) * fg_ref[...]
    y_ref[...] = r


def _out_call(x, oa, ob, oc, w_out, fg, final):
    bsz, s_len, d = x.shape
    tm = min(OUT_TM, s_len)
    tok = lambda a: pl.BlockSpec((None, tm, a.shape[-1]), lambda b, s: (b, s, 0))
    full = lambda a: pl.BlockSpec(a.shape, lambda b, s: (0,) * a.ndim)
    return pl.pallas_call(
        functools.partial(_out_body, final=final),
        grid=(bsz, s_len // tm),
        in_specs=[tok(x), tok(oa), tok(ob), tok(oc), full(w_out), full(fg)],
        out_specs=tok(x),
        out_shape=jax.ShapeDtypeStruct(x.shape, F32),
        compiler_params=pltpu.CompilerParams(
            dimension_semantics=("arbitrary", "arbitrary"), vmem_limit_bytes=VMEM_LIMIT),
        name="out_final" if final else "out",
    )(x, oa, ob, oc, w_out, fg)


def _layer(x, g, w_in, aqn, akn, sink, cqn, ckvn, wuq, wukv, w_out, fg, e_mat, tabs, final):
    row = lambda v: v.astype(F32)[None, :]
    qab, ka, va, kb, vb, sg, qc, kc, vc = _proj_call(
        x, row(g), _take_cols(w_in, _w_in_columns()).astype(BF16), row(aqn[_DIM_A]), row(akn[_DIM_A]),
        row(jnp.pad(cqn, (0, C_Q_PAD - C_Q_RANK))), row(ckvn),
        _pack_w_uq(wuq), _pack_w_ukv(wukv), e_mat, tabs)
    oa = _stack_call("attn_a", qab, ka, va, sg, 0, ATTN_A_TQ)
    ob = _stack_call("attn_b", qab, kb, vb, sg, 1, ATTN_B_TQ, sink=sink.astype(F32))
    oc = _attn_c_call(qc, kc, vc, sg, 2 * N_SLABS)
    w_out_packed = jnp.take(w_out, jnp.asarray(_w_out_rows()), axis=0).astype(BF16)
    return _out_call(x, oa, ob, oc, w_out_packed, row(fg), final)


def kernel(x, norm_g, w_in, a_q_norm, a_k_norm, b_sink, c_q_norm, c_kv_norm, c_w_uq, c_w_ukv, w_out, final_g):
    depth = w_in.shape[0]
    tabs = _rope_tables(x.shape[1])
    e_mat = _group_sum_matrix()
    h = x
    for l in range(depth):
        h = _layer(h, norm_g[l], w_in[l], a_q_norm[l], a_k_norm[l], b_sink[l], c_q_norm[l], c_kv_norm[l],
                   c_w_uq[l], c_w_ukv[l], w_out[l], final_g, e_mat, tabs, final=(l == depth - 1))
    return h
```

```python
import functools

import numpy as np
import jax
import jax.numpy as jnp
from jax import lax
from jax.experimental import pallas as pl
from jax.experimental.pallas import tpu as pltpu

F32 = jnp.float32
BF16 = jnp.bfloat16

GRID_W = 64
ROPE_THETA = 10000.0
EPS = 1e-6
HEAD_DIM = 64
WINDOW = 128
AB_HEADS = 6
C_HEADS = 4
C_NOPE = 64
C_ROPE = 32
C_V = 64
C_Q_RANK = 192
C_KV_RANK = 128

LANES = 128
HALF = LANES // 2
QUARTER = LANES // 4
N_SLABS = AB_HEADS // 2
C_Q_PAD = 256
NEG = -1e30
LOG2E = 1.4426950408889634

IN_AQ, IN_AK, IN_AV, IN_AG = 0, 384, 512, 640
IN_BQ, IN_BK, IN_BV, IN_BG = 1024, 1408, 1536, 1664
IN_CQ, IN_CKV, IN_CKR, IN_CG = 2048, 2240, 2368, 2400

OFF_CQ = 0
OFF_CKV = 256
OFF_CKR = 384
OFF_QA, OFF_KA = 512, 896
OFF_QB, OFF_KB = 1024, 1408
OFF_VA, OFF_VB = 1536, 1664
OFF_G = 1792
PACKED_W = 2816
MIX_W = 1024

PROJ_TM = 512
ATTN_A_TQ = 256
ATTN_B_TQ = 256
ATTN_C_TQ = 512
OUT_TM = 1024
VMEM_LIMIT = 56 * 1024 * 1024


_LANE = np.arange(LANES)
_HEAD_OF_LANE = (_LANE // QUARTER) % 2


def _dim_of_lane(first_half_dims, partner):
    f = np.asarray(first_half_dims)
    return np.where(_LANE < HALF, f[_LANE % QUARTER], f[_LANE % QUARTER] + partner)


_DIM_A = _dim_of_lane(list(range(16)) + list(range(32, 48)), 16)
_DIM_B = _dim_of_lane(list(range(32)), 32)

_C_ROPE_OF_LANE = np.full(LANES, -1)
_C_ROPE_OF_LANE[0:16] = np.arange(16)
_C_ROPE_OF_LANE[64:80] = 16 + np.arange(16)
_C_NOPE_OF_LANE = np.full(LANES, -1)
_C_NOPE_OF_LANE[16:64] = np.arange(48)
_C_NOPE_OF_LANE[80:96] = 48 + np.arange(16)


def _slab_heads(p):
    return np.where(_HEAD_OF_LANE == 0, p, p + N_SLABS)


def _w_in_columns():
    cols = np.full(PACKED_W, -1)

    def put(off, c):
        cols[off:off + len(c)] = c

    for off, base, dim in ((OFF_QA, IN_AQ, _DIM_A), (OFF_QB, IN_BQ, _DIM_B)):
        for p in range(N_SLABS):
            put(off + p * LANES, base + _slab_heads(p) * HEAD_DIM + dim)
    put(OFF_KA, IN_AK + _HEAD_OF_LANE * HEAD_DIM + _DIM_A)
    put(OFF_KB, IN_BK + _HEAD_OF_LANE * HEAD_DIM + _DIM_B)
    put(OFF_VA, IN_AV + _LANE)
    put(OFF_VB, IN_BV + _LANE)
    for i, base in enumerate((IN_AG, IN_BG)):
        for p in range(N_SLABS):
            put(OFF_G + (i * N_SLABS + p) * LANES,
                base + np.concatenate([p * HEAD_DIM + np.arange(HEAD_DIM),
                                       (p + N_SLABS) * HEAD_DIM + np.arange(HEAD_DIM)]))
    put(OFF_G + 2 * N_SLABS * LANES, IN_CG + np.arange(C_HEADS * C_V))
    put(OFF_CQ, IN_CQ + np.arange(C_Q_RANK))
    put(OFF_CKV, IN_CKV + np.arange(C_KV_RANK))
    put(OFF_CKR, np.where(_C_ROPE_OF_LANE >= 0, IN_CKR + _C_ROPE_OF_LANE, -1))
    return cols


def _w_out_rows():
    rows = []
    for base in (0, AB_HEADS * HEAD_DIM):
        for p in range(N_SLABS):
            rows.append(base + np.concatenate([p * HEAD_DIM + np.arange(HEAD_DIM),
                                               (p + N_SLABS) * HEAD_DIM + np.arange(HEAD_DIM)]))
    rows.append(2 * AB_HEADS * HEAD_DIM + np.arange(C_HEADS * C_V))
    return np.concatenate(rows)


def _take(w, idx, axis):
    pieces, start = [], 0
    for i in range(1, len(idx) + 1):
        same_run = i < len(idx) and ((idx[i - 1] >= 0 and idx[i] == idx[i - 1] + 1)
                                     or (idx[i - 1] < 0 and idx[i] < 0))
        if not same_run:
            n = i - start
            if idx[start] < 0:
                shape = list(w.shape)
                shape[axis] = n
                pieces.append(jnp.zeros(shape, w.dtype))
            else:
                pieces.append(lax.slice_in_dim(w, int(idx[start]), int(idx[start]) + n, axis=axis))
            start = i
    return jnp.concatenate(pieces, axis=axis)


def _take_cols(w, cols):
    return _take(w, cols, 1)


def _pack_w_uq(w):
    per_head = np.where(_C_ROPE_OF_LANE >= 0, C_NOPE + _C_ROPE_OF_LANE, _C_NOPE_OF_LANE)
    cols = np.concatenate([np.where(per_head >= 0, h * (C_NOPE + C_ROPE) + per_head, -1) for h in range(C_HEADS)])
    return jnp.pad(_take_cols(w, cols), ((0, C_Q_PAD - C_Q_RANK), (0, 0))).astype(BF16)


def _pack_w_ukv(w):
    k_cols = np.concatenate([np.where(_C_NOPE_OF_LANE >= 0, h * (C_NOPE + C_V) + _C_NOPE_OF_LANE, -1)
                             for h in range(C_HEADS)])
    v_cols = np.concatenate([h * (C_NOPE + C_V) + C_NOPE + np.arange(C_V) for h in range(C_HEADS)])
    return _take_cols(w, np.concatenate([k_cols, v_cols])).astype(BF16)


def _group_sum_matrix():
    head = np.concatenate([_HEAD_OF_LANE, 2 + _HEAD_OF_LANE])
    return jnp.asarray(head[:, None] == head[None, :], BF16)


def _rope_tables(s_len):
    t = jnp.arange(s_len)

    def cs(pos, dim):
        inv = ROPE_THETA ** (-jnp.arange(0, dim, 2, dtype=F32) / dim)
        ang = pos.astype(F32)[:, None] * inv[None, :]
        ang = jnp.concatenate([ang, ang], axis=-1)
        return jnp.cos(ang), jnp.sin(ang)

    sign = jnp.asarray(np.where(_LANE < HALF, -1.0, 1.0), F32)[None, :]
    rc, rs = cs(t // GRID_W, HEAD_DIM // 2)
    cc, csn = cs(t % GRID_W, HEAD_DIM // 2)
    cos_a, sin_a = jnp.concatenate([rc, cc], -1), jnp.concatenate([rs, csn], -1)
    cos_b, sin_b = cs(t, HEAD_DIM)
    cos_m, sin_m = cs(t, C_ROPE)
    is_rope = jnp.asarray(_C_ROPE_OF_LANE >= 0)[None, :]
    rope_idx = np.maximum(_C_ROPE_OF_LANE, 0)
    return jnp.stack([
        cos_a[:, _DIM_A], sin_a[:, _DIM_A] * sign,
        cos_b[:, _DIM_B], sin_b[:, _DIM_B] * sign,
        jnp.where(is_rope, cos_m[:, rope_idx], 1.0), jnp.where(is_rope, sin_m[:, rope_idx] * sign, 0.0)])


def _proj_body(x_ref, g_ref, w_ref, aqn_ref, akn_ref, cqn_ref, ckvn_ref, wuq_ref, wukv_ref, e_ref, tab_ref,
               qab_ref, ka_ref, va_ref, kb_ref, vb_ref, sg_ref, qc_ref, kc_ref, vc_ref):
    x = x_ref[...]
    h = x * lax.rsqrt(jnp.mean(x * x, axis=-1, keepdims=True) + EPS) * g_ref[...]
    z = jnp.dot(h.astype(BF16), w_ref[...], preferred_element_type=F32)

    def rope(xs, mixer):
        return xs * tab_ref[2 * mixer] + pltpu.roll(xs, HALF, 1) * tab_ref[2 * mixer + 1]

    def slab(off):
        return z[:, off:off + LANES]

    def head_sumsq(s0, s1):
        t = jnp.concatenate([s0 * s0, s1 * s1], axis=1).astype(BF16)
        ss = jnp.dot(t, e_ref[...], preferred_element_type=F32)
        return ss[:, :LANES], ss[:, LANES:]

    def head_norm(xs, ss, gain):
        return xs * lax.rsqrt(ss * (1.0 / HEAD_DIM) + EPS) * gain

    cq = z[:, OFF_CQ:OFF_CQ + C_Q_PAD]
    cq = cq * lax.rsqrt(jnp.sum(cq * cq, axis=-1, keepdims=True) * (1.0 / C_Q_RANK) + EPS) * cqn_ref[...]
    qc = jnp.dot(cq.astype(BF16), wuq_ref[...], preferred_element_type=F32)
    scale_c = (C_NOPE + C_ROPE) ** -0.5 * LOG2E
    for hh in range(C_HEADS):
        q = rope(qc[:, hh * LANES:(hh + 1) * LANES], 2)
        qc_ref[:, hh * LANES:(hh + 1) * LANES] = (q * scale_c).astype(BF16)
    ckv = z[:, OFF_CKV:OFF_CKV + C_KV_RANK]
    ckv = ckv * lax.rsqrt(jnp.mean(ckv * ckv, axis=-1, keepdims=True) + EPS) * ckvn_ref[...]
    kv = jnp.dot(ckv.astype(BF16), wukv_ref[...], preferred_element_type=F32)
    kr = rope(slab(OFF_CKR), 2)
    for hh in range(C_HEADS):
        kc_ref[hh] = (kv[:, hh * LANES:(hh + 1) * LANES] + kr).astype(BF16)
    vc_ref[...] = kv[:, C_HEADS * LANES:].astype(BF16)

    scale_ab = HEAD_DIM ** -0.5 * LOG2E
    qa = [slab(OFF_QA + j * LANES) for j in range(N_SLABS)]
    ka = slab(OFF_KA)
    ss0, ss1 = head_sumsq(qa[0], qa[1])
    ss2, ssk = head_sumsq(qa[2], ka)
    for j, ss in enumerate((ss0, ss1, ss2)):
        q = rope(head_norm(qa[j], ss, aqn_ref[...]), 0)
        qab_ref[:, j * LANES:(j + 1) * LANES] = (q * scale_ab).astype(BF16)
    ka_ref[...] = rope(head_norm(ka, ssk, akn_ref[...]), 0).astype(BF16)
    for j in range(N_SLABS):
        q = rope(slab(OFF_QB + j * LANES), 1)
        qab_ref[:, (N_SLABS + j) * LANES:(N_SLABS + j + 1) * LANES] = (q * scale_ab).astype(BF16)
    kb_ref[...] = rope(slab(OFF_KB), 1).astype(BF16)
    va_ref[...] = slab(OFF_VA).astype(BF16)
    vb_ref[...] = slab(OFF_VB).astype(BF16)

    gz = z[:, OFF_G:OFF_G + MIX_W]
    sg_ref[...] = (gz / (1.0 + jnp.exp(-gz))).astype(BF16)


def _proj_call(x, g, w_packed, aqn, akn, cqn, ckvn, wuq, wukv, e_mat, tabs):
    bsz, s_len, d = x.shape
    tm = min(PROJ_TM, s_len)
    grid = (s_len // tm, bsz)
    full = lambda a: pl.BlockSpec(a.shape, lambda s, b: (0,) * a.ndim)
    tok = lambda w: pl.BlockSpec((None, tm, w), lambda s, b: (b, s, 0))
    bf = lambda *shape: jax.ShapeDtypeStruct(shape, BF16)
    widths = [2 * N_SLABS * LANES, LANES, LANES, LANES, LANES, MIX_W, C_HEADS * LANES]
    return pl.pallas_call(
        _proj_body,
        grid=grid,
        in_specs=[tok(d), full(g), full(w_packed), full(aqn), full(akn), full(cqn), full(ckvn),
                  full(wuq), full(wukv), full(e_mat),
                  pl.BlockSpec((6, tm, LANES), lambda s, b: (0, s, 0))],
        out_specs=[tok(w) for w in widths]
                  + [pl.BlockSpec((None, C_HEADS, tm, LANES), lambda s, b: (b, 0, s, 0)), tok(C_HEADS * C_V)],
        out_shape=[bf(bsz, s_len, w) for w in widths]
                  + [bf(bsz, C_HEADS, s_len, LANES), bf(bsz, s_len, C_HEADS * C_V)],
        compiler_params=pltpu.CompilerParams(
            dimension_semantics=("arbitrary", "arbitrary"), vmem_limit_bytes=VMEM_LIMIT),
        name="proj",
    )(x, g, w_packed, aqn, akn, cqn, ckvn, wuq, wukv, e_mat, tabs)


def _scores(q, k):
    return lax.dot_general(q, k, (((1,), (1,)), ((), ())), preferred_element_type=F32)


def _pipeline_prologue(t, n_q, v_ref, vaug_ref, s_b, m_b):
    @pl.when(lax.rem(jnp.maximum(t - 1, 0), n_q) == 0)
    def _():
        vaug_ref[:, :LANES] = v_ref[...]
        vaug_ref[:, LANES:] = jnp.ones((vaug_ref.shape[0], LANES), BF16)

    @pl.when(t == 0)
    def _():
        s_b[...] = jnp.zeros(s_b.shape, F32)
        m_b[...] = jnp.zeros(m_b.shape, F32)


def _run_parity(t, step, s_a, m_a, s_b, m_b):
    parity = lax.rem(t, 2)

    @pl.when(parity == 0)
    def _():
        step(s_a, m_a, s_b, m_b)

    @pl.when(parity == 1)
    def _():
        step(s_b, m_b, s_a, m_a)


def _softmax_numerators(s_r, m_r, rows, width):
    m = m_r[rows, :]
    return jnp.concatenate(
        [jnp.exp2(s_r[rows, j * LANES:(j + 1) * LANES] - m).astype(BF16) for j in range(width // LANES)], axis=1)


def _stack_queries(q, head_a):
    slabs = [q[:, j * LANES:(j + 1) * LANES] for j in range(N_SLABS)]
    return jnp.concatenate([jnp.where(head_a, s, 0.0) for s in slabs]
                           + [jnp.where(head_a, 0.0, s) for s in slabs], axis=0).astype(BF16)


def _stack_pipe_body(q_ref, k_ref, v_ref, sg_ref, o_ref, s_a, s_b, m_a, m_b, vaug_ref, *, n_q):
    t = pl.program_id(0)
    tq = q_ref.shape[0]
    s_len = k_ref.shape[0]
    _pipeline_prologue(t, n_q, v_ref, vaug_ref, s_b, m_b)

    lane = lax.broadcasted_iota(jnp.int32, (tq, LANES), 1)
    out_low = lane < HALF
    q_stack = _stack_queries(q_ref[...].astype(F32), (lane & QUARTER) == 0)
    blocks = [pl.ds(i * tq, tq) for i in range(AB_HEADS)]

    def step(s_w, m_w, s_r, m_r):
        s = _scores(q_stack, k_ref[...])
        for i in range(AB_HEADS):
            blk = s[i * tq:(i + 1) * tq]
            s_w[blocks[i], :] = blk
            m_w[blocks[i], :] = jnp.broadcast_to(jnp.max(blk, axis=1, keepdims=True), (tq, LANES))
        p = jnp.concatenate([_softmax_numerators(s_r, m_r, blocks[i], s_len) for i in range(AB_HEADS)], axis=0)
        a = jnp.dot(p, vaug_ref[...], preferred_element_type=F32)
        for j in range(N_SLABS):
            lo, hi = a[j * tq:(j + 1) * tq], a[(j + N_SLABS) * tq:(j + N_SLABS + 1) * tq]
            o = jnp.where(out_low, lo[:, :LANES] / lo[:, LANES:], hi[:, :LANES] / hi[:, LANES:])
            cols = slice(j * LANES, (j + 1) * LANES)
            o_ref[:, cols] = (o * sg_ref[:, cols].astype(F32)).astype(BF16)

    _run_parity(t, step, s_a, m_a, s_b, m_b)


def _window_pipe_body(sink_ref, q_ref, k_ref, v_ref, sg_ref, o_ref, s_a, s_b, m_a, m_b, vaug_ref, kpad_ref,
                      *, n_q, n_items):
    t = pl.program_id(0)
    tq = q_ref.shape[0]
    s_len = k_ref.shape[0]
    sub = WINDOW
    kw = 3 * WINDOW
    n_sub = tq // sub
    t_cur = jnp.minimum(t, n_items - 1)
    t_prev = jnp.maximum(t - 1, 0)
    halo = jnp.zeros((WINDOW, 2 * LANES), BF16)

    @pl.when(lax.rem(t_prev, n_q) == 0)
    def _():
        vaug_ref[:WINDOW, :] = halo
        vaug_ref[WINDOW:WINDOW + s_len, :LANES] = v_ref[...]
        vaug_ref[WINDOW:WINDOW + s_len, LANES:] = jnp.ones((s_len, LANES), BF16)
        vaug_ref[WINDOW + s_len:, :] = halo

    @pl.when(lax.rem(t_cur, n_q) == 0)
    def _():
        kpad_ref[:WINDOW, :] = halo[:, :LANES]
        kpad_ref[WINDOW:WINDOW + s_len, :] = k_ref[...]
        kpad_ref[WINDOW + s_len:, :] = halo[:, :LANES]

    @pl.when(t == 0)
    def _():
        s_b[...] = jnp.zeros(s_b.shape, F32)
        m_b[...] = jnp.zeros(m_b.shape, F32)

    lane = lax.broadcasted_iota(jnp.int32, (sub, LANES), 1)
    row = lax.broadcasted_iota(jnp.int32, (sub, LANES), 0)
    head_a = (lane & QUARTER) == 0
    out_low = lane < HALF
    q = q_ref[...].astype(F32)
    row0_cur = lax.rem(t_cur, n_q) * tq
    row0_prev = lax.rem(t_prev, n_q) * tq
    sinks = [sink_ref[i] * LOG2E for i in range(AB_HEADS)]

    def step(s_w, m_w, s_r, m_r):
        for u in range(n_sub):
            r0 = row0_cur + u * sub
            k_win = kpad_ref[pl.ds(pl.multiple_of(r0, sub), kw), :]
            bias_lo = jnp.where(jnp.logical_and(lane >= row, r0 > 0), 0.0, NEG)
            bias_hi = jnp.where(jnp.logical_and(lane <= row, r0 < s_len - sub), 0.0, NEG)
            s = _scores(_stack_queries(q[u * sub:(u + 1) * sub], head_a), k_win)
            for i in range(AB_HEADS):
                blk = s[i * sub:(i + 1) * sub]
                cols = [blk[:, :LANES] + bias_lo, blk[:, LANES:2 * LANES], blk[:, 2 * LANES:] + bias_hi]
                mx = jnp.max(jnp.maximum(jnp.maximum(cols[0], cols[1]), cols[2]), axis=1, keepdims=True)
                rows = pl.ds((u * AB_HEADS + i) * sub, sub)
                for c in range(3):
                    s_w[rows, c * LANES:(c + 1) * LANES] = cols[c]
                m_w[rows, :] = jnp.broadcast_to(jnp.maximum(mx, sinks[i]), (sub, LANES))
        for u in range(n_sub):
            r0 = row0_prev + u * sub
            v_win = vaug_ref[pl.ds(pl.multiple_of(r0, sub), kw), :]
            rows = [pl.ds((u * AB_HEADS + i) * sub, sub) for i in range(AB_HEADS)]
            p = jnp.concatenate([_softmax_numerators(s_r, m_r, rows[i], kw) for i in range(AB_HEADS)], axis=0)
            a = jnp.dot(p, v_win, preferred_element_type=F32)
            for j in range(N_SLABS):
                parts = []
                for i in (j, j + N_SLABS):
                    blk = a[i * sub:(i + 1) * sub]
                    den = blk[:, LANES:] + jnp.exp2(sinks[i] - m_r[rows[i], :])
                    parts.append(blk[:, :LANES] / den)
                cols = slice(j * LANES, (j + 1) * LANES)
                gate = sg_ref[u * sub:(u + 1) * sub, cols].astype(F32)
                o_ref[u * sub:(u + 1) * sub, cols] = (jnp.where(out_low, parts[0], parts[1]) * gate).astype(BF16)

    _run_parity(t, step, s_a, m_a, s_b, m_b)


def _stack_call(name, qab, k, v, sg, blk, tq, sink=None):
    bsz, s_len, _ = qab.shape
    tq = min(tq, s_len)
    n_q = s_len // tq
    n_items = bsz * n_q
    windowed = sink is not None
    cur = lambda t: (jnp.minimum(t, n_items - 1) // n_q, jnp.minimum(t, n_items - 1) % n_q)
    prev = lambda t: (jnp.maximum(t - 1, 0) // n_q, jnp.maximum(t - 1, 0) % n_q)
    width = N_SLABS * LANES
    in_specs = [
        pl.BlockSpec((None, tq, width), lambda t: (*cur(t), blk)),
        pl.BlockSpec((None, s_len, LANES), lambda t: (cur(t)[0], 0, 0)),
        pl.BlockSpec((None, s_len, LANES), lambda t: (prev(t)[0], 0, 0)),
        pl.BlockSpec((None, tq, width), lambda t: (*prev(t), blk)),
    ]
    args = [qab, k, v, sg]
    rows = AB_HEADS * tq
    if windowed:
        in_specs = [pl.BlockSpec(memory_space=pltpu.SMEM)] + in_specs
        args = [sink] + args
        body = functools.partial(_window_pipe_body, n_q=n_q, n_items=n_items)
        padded = s_len + 2 * WINDOW
        scratch = [pltpu.VMEM((rows, 3 * WINDOW), F32), pltpu.VMEM((rows, 3 * WINDOW), F32),
                   pltpu.VMEM((rows, LANES), F32), pltpu.VMEM((rows, LANES), F32),
                   pltpu.VMEM((padded, 2 * LANES), BF16), pltpu.VMEM((padded, LANES), BF16)]
    else:
        body = functools.partial(_stack_pipe_body, n_q=n_q)
        scratch = [pltpu.VMEM((rows, s_len), F32), pltpu.VMEM((rows, s_len), F32),
                   pltpu.VMEM((rows, LANES), F32), pltpu.VMEM((rows, LANES), F32),
                   pltpu.VMEM((s_len, 2 * LANES), BF16)]
    return pl.pallas_call(
        body,
        grid=(n_items + 1,),
        in_specs=in_specs,
        out_specs=pl.BlockSpec((None, tq, width), lambda t: (*prev(t), 0)),
        out_shape=jax.ShapeDtypeStruct((bsz, s_len, width), BF16),
        scratch_shapes=scratch,
        compiler_params=pltpu.CompilerParams(dimension_semantics=("arbitrary",), vmem_limit_bytes=VMEM_LIMIT),
        name=name,
    )(*args)


def _pair_pipe_body(q0_ref, q1_ref, k0_ref, k1_ref, v_ref, sg_ref, o_ref, s_a, s_b, m_a, m_b, vaug_ref, *, n_q):
    t = pl.program_id(0)
    tq, s_len = s_a.shape[1], s_a.shape[2]
    _pipeline_prologue(t, n_q, v_ref, vaug_ref, s_b, m_b)
    out_low = lax.broadcasted_iota(jnp.int32, (tq, LANES), 1) < HALF
    qs, ks = (q0_ref, q1_ref), (k0_ref, k1_ref)

    def step(s_w, m_w, s_r, m_r):
        for hd in range(2):
            s = _scores(qs[hd][...], ks[hd][...])
            s_w[hd] = s
            m_w[hd] = jnp.broadcast_to(jnp.max(s, axis=1, keepdims=True), (tq, LANES))
        outs = []
        for hd in range(2):
            p = _softmax_numerators(s_r.at[hd], m_r.at[hd], slice(None), s_len)
            a = jnp.dot(p, vaug_ref[...], preferred_element_type=F32)
            outs.append(a[:, :LANES] / a[:, LANES:])
        o_ref[...] = (jnp.where(out_low, outs[0], outs[1]) * sg_ref[...].astype(F32)).astype(BF16)

    _run_parity(t, step, s_a, m_a, s_b, m_b)


def _attn_c_call(qc, kc, vc, sg, g_off):
    bsz, s_len, _ = qc.shape
    tq = min(ATTN_C_TQ, s_len)
    n_q = s_len // tq
    n_pairs = C_HEADS // 2
    n_items = bsz * n_pairs * n_q

    def item(t):
        return t // (n_pairs * n_q), (t // n_q) % n_pairs, t % n_q

    cur = lambda t: item(jnp.minimum(t, n_items - 1))
    prev = lambda t: item(jnp.maximum(t - 1, 0))

    def q_spec(o):
        def imap(t):
            b, p, i = cur(t)
            return b, i, 2 * p + o
        return pl.BlockSpec((None, tq, LANES), imap)

    def k_spec(o):
        def imap(t):
            b, p, i = cur(t)
            return b, 2 * p + o, 0, 0
        return pl.BlockSpec((None, None, s_len, LANES), imap)

    def v_map(t):
        b, p, i = prev(t)
        return b, 0, p

    def sg_map(t):
        b, p, i = prev(t)
        return b, i, g_off + p

    def o_map(t):
        b, p, i = prev(t)
        return b, i, p

    return pl.pallas_call(
        functools.partial(_pair_pipe_body, n_q=n_q),
        grid=(n_items + 1,),
        in_specs=[q_spec(0), q_spec(1), k_spec(0), k_spec(1),
                  pl.BlockSpec((None, s_len, LANES), v_map), pl.BlockSpec((None, tq, LANES), sg_map)],
        out_specs=pl.BlockSpec((None, tq, LANES), o_map),
        out_shape=jax.ShapeDtypeStruct((bsz, s_len, n_pairs * LANES), BF16),
        scratch_shapes=[pltpu.VMEM((2, tq, s_len), F32), pltpu.VMEM((2, tq, s_len), F32),
                        pltpu.VMEM((2, tq, LANES), F32), pltpu.VMEM((2, tq, LANES), F32),
                        pltpu.VMEM((s_len, 2 * LANES), BF16)],
        compiler_params=pltpu.CompilerParams(dimension_semantics=("arbitrary",), vmem_limit_bytes=VMEM_LIMIT),
        name="attn_c",
    )(qc, qc, kc, kc, vc, sg)


def _out_body(x_ref, oa_ref, ob_ref, oc_ref, w_ref, fg_ref, y_ref, *, final):
    o = jnp.concatenate([oa_ref[...], ob_ref[...], oc_ref[...]], axis=1)
    r = x_ref[...] + jnp.dot(o, w_ref[...], preferred_element_type=F32)
    if final:
        r = r * lax.rsqrt(jnp.mean(r * r, axis=-1, keepdims=True) + EPS) * fg_ref[...]
    y_ref[...] = r


def _out_call(x, oa, ob, oc, w_out, fg, final):
    bsz, s_len, d = x.shape
    tm = min(OUT_TM, s_len)
    tok = lambda a: pl.BlockSpec((None, tm, a.shape[-1]), lambda b, s: (b, s, 0))
    full = lambda a: pl.BlockSpec(a.shape, lambda b, s: (0,) * a.ndim)
    return pl.pallas_call(
        functools.partial(_out_body, final=final),
        grid=(bsz, s_len // tm),
        in_specs=[tok(x), tok(oa), tok(ob), tok(oc), full(w_out), full(fg)],
        out_specs=tok(x),
        out_shape=jax.ShapeDtypeStruct(x.shape, F32),
        compiler_params=pltpu.CompilerParams(
            dimension_semantics=("arbitrary", "arbitrary"), vmem_limit_bytes=VMEM_LIMIT),
        name="out_final" if final else "out",
    )(x, oa, ob, oc, w_out, fg)


def _layer(x, g, w_in, aqn, akn, sink, cqn, ckvn, wuq, wukv, w_out, fg, e_mat, tabs, final):
    row = lambda v: v.astype(F32)[None, :]
    qab, ka, va, kb, vb, sg, qc, kc, vc = _proj_call(
        x, row(g), _take_cols(w_in, _w_in_columns()).astype(BF16), row(aqn[_DIM_A]), row(akn[_DIM_A]),
        row(jnp.pad(cqn, (0, C_Q_PAD - C_Q_RANK))), row(ckvn),
        _pack_w_uq(wuq), _pack_w_ukv(wukv), e_mat, tabs)
    oa = _stack_call("attn_a", qab, ka, va, sg, 0, ATTN_A_TQ)
    ob = _stack_call("attn_b", qab, kb, vb, sg, 1, ATTN_B_TQ, sink=sink.astype(F32))
    oc = _attn_c_call(qc, kc, vc, sg, 2 * N_SLABS)
    w_out_packed = _take(w_out, _w_out_rows(), 0).astype(BF16)
    return _out_call(x, oa, ob, oc, w_out_packed, row(fg), final)


def kernel(x, norm_g, w_in, a_q_norm, a_k_norm, b_sink, c_q_norm, c_kv_norm, c_w_uq, c_w_ukv, w_out, final_g):
    depth = w_in.shape[0]
    tabs = _rope_tables(x.shape[1])
    e_mat = _group_sum_matrix()
    h = x
    for l in range(depth):
        h = _layer(h, norm_g[l], w_in[l], a_q_norm[l], a_k_norm[l], b_sink[l], c_q_norm[l], c_kv_norm[l],
                   c_w_uq[l], c_w_ukv[l], w_out[l], final_g, e_mat, tabs, final=(l == depth - 1))
    return h
```

```python
import functools

import numpy as np
import jax
import jax.numpy as jnp
from jax import lax
from jax.experimental import pallas as pl
from jax.experimental.pallas import tpu as pltpu

F32 = jnp.float32
BF16 = jnp.bfloat16

GRID_W = 64
ROPE_THETA = 10000.0
EPS = 1e-6
HEAD_DIM = 64
WINDOW = 128
AB_HEADS = 6
C_HEADS = 4
C_NOPE = 64
C_ROPE = 32
C_V = 64
C_Q_RANK = 192
C_KV_RANK = 128

LANES = 128
HALF = LANES // 2
QUARTER = LANES // 4
N_SLABS = AB_HEADS // 2
C_Q_PAD = 256
NEG = -1e30
LOG2E = 1.4426950408889634

IN_AQ, IN_AK, IN_AV, IN_AG = 0, 384, 512, 640
IN_BQ, IN_BK, IN_BV, IN_BG = 1024, 1408, 1536, 1664
IN_CQ, IN_CKV, IN_CKR, IN_CG = 2048, 2240, 2368, 2400

OFF_CQ = 0
OFF_CKV = 256
OFF_CKR = 384
OFF_QA, OFF_KA = 512, 896
OFF_QB, OFF_KB = 1024, 1408
OFF_VA, OFF_VB = 1536, 1664
OFF_G = 1792
PACKED_W = 2816
MIX_W = 1024

PROJ_TM = 512
ATTN_A_TQ = 256
ATTN_B_TQ = 256
ATTN_C_TQ = 1024
OUT_TM = 1024
VMEM_LIMIT = 56 * 1024 * 1024


_LANE = np.arange(LANES)
_HEAD_OF_LANE = (_LANE // QUARTER) % 2


def _dim_of_lane(first_half_dims, partner):
    f = np.asarray(first_half_dims)
    return np.where(_LANE < HALF, f[_LANE % QUARTER], f[_LANE % QUARTER] + partner)


_DIM_A = _dim_of_lane(list(range(16)) + list(range(32, 48)), 16)
_DIM_B = _dim_of_lane(list(range(32)), 32)

_C_ROPE_OF_LANE = np.full(LANES, -1)
_C_ROPE_OF_LANE[0:16] = np.arange(16)
_C_ROPE_OF_LANE[64:80] = 16 + np.arange(16)
_C_NOPE_OF_LANE = np.full(LANES, -1)
_C_NOPE_OF_LANE[16:64] = np.arange(48)
_C_NOPE_OF_LANE[80:96] = 48 + np.arange(16)


def _pack_w_in(w):
    d = w.shape[0]

    def pair_lanes(cols, rotary_blocks):
        i = HEAD_DIM // (2 * rotary_blocks)
        p = cols.shape[1] // (2 * HEAD_DIM)
        x = cols.reshape(d, 2, p, rotary_blocks, 2, i)
        return x.transpose(0, 2, 4, 1, 3, 5).reshape(d, p * LANES)

    def pair_heads(cols):
        p = cols.shape[1] // (2 * HEAD_DIM)
        return cols.reshape(d, 2, p, HEAD_DIM).transpose(0, 2, 1, 3).reshape(d, p * LANES)

    zeros = lambda n: jnp.zeros((d, n), w.dtype)
    sl = lambda a, n: w[:, a:a + n]
    ckr = sl(IN_CKR, C_ROPE)
    pieces = {
        OFF_CQ: jnp.concatenate([sl(IN_CQ, C_Q_RANK), zeros(C_Q_PAD - C_Q_RANK)], 1),
        OFF_CKV: sl(IN_CKV, C_KV_RANK),
        OFF_CKR: jnp.concatenate([ckr[:, :16], zeros(48), ckr[:, 16:], zeros(48)], 1),
        OFF_QA: pair_lanes(sl(IN_AQ, 384), 2), OFF_KA: pair_lanes(sl(IN_AK, 128), 2),
        OFF_QB: pair_lanes(sl(IN_BQ, 384), 1), OFF_KB: pair_lanes(sl(IN_BK, 128), 1),
        OFF_VA: sl(IN_AV, 128), OFF_VB: sl(IN_BV, 128),
        OFF_G: jnp.concatenate([pair_heads(sl(IN_AG, 384)), pair_heads(sl(IN_BG, 384)), sl(IN_CG, 256)], 1),
    }
    ordered, off = [], 0
    for o in sorted(pieces):
        assert o == off
        ordered.append(pieces[o])
        off += pieces[o].shape[1]
    assert off == PACKED_W
    return jnp.concatenate(ordered, 1).astype(BF16)


def _pack_w_out(w):
    dd = w.shape[1]
    n = AB_HEADS * HEAD_DIM
    pair = lambda r: r.reshape(2, N_SLABS, HEAD_DIM, dd).transpose(1, 0, 2, 3).reshape(n, dd)
    return jnp.concatenate([pair(w[:n]), pair(w[n:2 * n]), w[2 * n:]], 0).astype(BF16)


def _take_cols(w, cols):
    pieces, start = [], 0
    for i in range(1, len(cols) + 1):
        same_run = i < len(cols) and ((cols[i - 1] >= 0 and cols[i] == cols[i - 1] + 1)
                                      or (cols[i - 1] < 0 and cols[i] < 0))
        if not same_run:
            n = i - start
            if cols[start] < 0:
                pieces.append(jnp.zeros((w.shape[0], n), w.dtype))
            else:
                pieces.append(w[:, int(cols[start]):int(cols[start]) + n])
            start = i
    return jnp.concatenate(pieces, axis=1)


def _pack_w_uq(w):
    per_head = np.where(_C_ROPE_OF_LANE >= 0, C_NOPE + _C_ROPE_OF_LANE, _C_NOPE_OF_LANE)
    cols = np.concatenate([np.where(per_head >= 0, h * (C_NOPE + C_ROPE) + per_head, -1) for h in range(C_HEADS)])
    return jnp.pad(_take_cols(w, cols), ((0, C_Q_PAD - C_Q_RANK), (0, 0))).astype(BF16)


def _pack_w_ukv(w):
    k_cols = np.concatenate([np.where(_C_NOPE_OF_LANE >= 0, h * (C_NOPE + C_V) + _C_NOPE_OF_LANE, -1)
                             for h in range(C_HEADS)])
    v_cols = np.concatenate([h * (C_NOPE + C_V) + C_NOPE + np.arange(C_V) for h in range(C_HEADS)])
    return _take_cols(w, np.concatenate([k_cols, v_cols])).astype(BF16)


def _group_sum_matrix():
    head = np.concatenate([_HEAD_OF_LANE, 2 + _HEAD_OF_LANE])
    return jnp.asarray(head[:, None] == head[None, :], BF16)


def _rope_tables(s_len):
    t = jnp.arange(s_len)

    def cs(pos, dim):
        inv = ROPE_THETA ** (-jnp.arange(0, dim, 2, dtype=F32) / dim)
        ang = pos.astype(F32)[:, None] * inv[None, :]
        ang = jnp.concatenate([ang, ang], axis=-1)
        return jnp.cos(ang), jnp.sin(ang)

    sign = jnp.asarray(np.where(_LANE < HALF, -1.0, 1.0), F32)[None, :]
    rc, rs = cs(t // GRID_W, HEAD_DIM // 2)
    cc, csn = cs(t % GRID_W, HEAD_DIM // 2)
    cos_a, sin_a = jnp.concatenate([rc, cc], -1), jnp.concatenate([rs, csn], -1)
    cos_b, sin_b = cs(t, HEAD_DIM)
    cos_m, sin_m = cs(t, C_ROPE)
    is_rope = jnp.asarray(_C_ROPE_OF_LANE >= 0)[None, :]
    rope_idx = np.maximum(_C_ROPE_OF_LANE, 0)
    return jnp.stack([
        cos_a[:, _DIM_A], sin_a[:, _DIM_A] * sign,
        cos_b[:, _DIM_B], sin_b[:, _DIM_B] * sign,
        jnp.where(is_rope, cos_m[:, rope_idx], 1.0), jnp.where(is_rope, sin_m[:, rope_idx] * sign, 0.0)])


def _proj_body(x_ref, *refs):
    _proj_compute(x_ref[...], *refs)


def _out_proj_body(x_ref, oa_ref, ob_ref, oc_ref, wo_ref, *refs):
    proj_refs, x_out_ref = refs[:-1], refs[-1]
    o = jnp.concatenate([oa_ref[...], ob_ref[...], oc_ref[...]], axis=1)
    x = x_ref[...] + jnp.dot(o, wo_ref[...], preferred_element_type=F32)
    x_out_ref[...] = x
    _proj_compute(x, *proj_refs)


def _proj_compute(x, g_ref, w_ref, aqn_ref, akn_ref, cqn_ref, ckvn_ref, wuq_ref, wukv_ref, e_ref, tab_ref,
                  qab_ref, ka_ref, va_ref, kb_ref, vb_ref, sg_ref, qc_ref, kc_ref, vc_ref):
    h = x * lax.rsqrt(jnp.mean(x * x, axis=-1, keepdims=True) + EPS) * g_ref[...]
    z = jnp.dot(h.astype(BF16), w_ref[...], preferred_element_type=F32)

    def rope(xs, mixer):
        return xs * tab_ref[2 * mixer] + pltpu.roll(xs, HALF, 1) * tab_ref[2 * mixer + 1]

    def slab(off):
        return z[:, off:off + LANES]

    def head_sumsq(s0, s1):
        t = jnp.concatenate([s0 * s0, s1 * s1], axis=1).astype(BF16)
        ss = jnp.dot(t, e_ref[...], preferred_element_type=F32)
        return ss[:, :LANES], ss[:, LANES:]

    def head_norm(xs, ss, gain):
        return xs * lax.rsqrt(ss * (1.0 / HEAD_DIM) + EPS) * gain

    cq = z[:, OFF_CQ:OFF_CQ + C_Q_PAD]
    cq = cq * lax.rsqrt(jnp.sum(cq * cq, axis=-1, keepdims=True) * (1.0 / C_Q_RANK) + EPS) * cqn_ref[...]
    qc = jnp.dot(cq.astype(BF16), wuq_ref[...], preferred_element_type=F32)
    scale_c = (C_NOPE + C_ROPE) ** -0.5 * LOG2E
    for hh in range(C_HEADS):
        q = rope(qc[:, hh * LANES:(hh + 1) * LANES], 2)
        qc_ref[:, hh * LANES:(hh + 1) * LANES] = (q * scale_c).astype(BF16)
    ckv = z[:, OFF_CKV:OFF_CKV + C_KV_RANK]
    ckv = ckv * lax.rsqrt(jnp.mean(ckv * ckv, axis=-1, keepdims=True) + EPS) * ckvn_ref[...]
    kv = jnp.dot(ckv.astype(BF16), wukv_ref[...], preferred_element_type=F32)
    kr = rope(slab(OFF_CKR), 2)
    for hh in range(C_HEADS):
        kc_ref[hh] = (kv[:, hh * LANES:(hh + 1) * LANES] + kr).astype(BF16)
    vc_ref[...] = kv[:, C_HEADS * LANES:].astype(BF16)

    scale_ab = HEAD_DIM ** -0.5 * LOG2E
    qa = [slab(OFF_QA + j * LANES) for j in range(N_SLABS)]
    ka = slab(OFF_KA)
    ss0, ss1 = head_sumsq(qa[0], qa[1])
    ss2, ssk = head_sumsq(qa[2], ka)
    for j, ss in enumerate((ss0, ss1, ss2)):
        q = rope(head_norm(qa[j], ss, aqn_ref[...]), 0)
        qab_ref[:, j * LANES:(j + 1) * LANES] = (q * scale_ab).astype(BF16)
    ka_ref[...] = rope(head_norm(ka, ssk, akn_ref[...]), 0).astype(BF16)
    for j in range(N_SLABS):
        q = rope(slab(OFF_QB + j * LANES), 1)
        qab_ref[:, (N_SLABS + j) * LANES:(N_SLABS + j + 1) * LANES] = (q * scale_ab).astype(BF16)
    kb_ref[...] = rope(slab(OFF_KB), 1).astype(BF16)
    va_ref[...] = slab(OFF_VA).astype(BF16)
    vb_ref[...] = slab(OFF_VB).astype(BF16)

    gz = z[:, OFF_G:OFF_G + MIX_W]
    sg_ref[...] = (gz / (1.0 + jnp.exp(-gz))).astype(BF16)


def _proj_call(x, g, w_packed, aqn, akn, cqn, ckvn, wuq, wukv, e_mat, tabs, prev_out=None):
    bsz, s_len, d = x.shape
    tm = min(PROJ_TM, s_len)
    grid = (s_len // tm, bsz)
    full = lambda a: pl.BlockSpec(a.shape, lambda s, b: (0,) * a.ndim)
    tok = lambda w: pl.BlockSpec((None, tm, w), lambda s, b: (b, s, 0))
    bf = lambda *shape: jax.ShapeDtypeStruct(shape, BF16)
    widths = [2 * N_SLABS * LANES, LANES, LANES, LANES, LANES, MIX_W, C_HEADS * LANES]
    weights = [g, w_packed, aqn, akn, cqn, ckvn, wuq, wukv, e_mat]
    in_specs = [tok(d)]
    args = [x]
    out_specs = ([tok(w) for w in widths]
                 + [pl.BlockSpec((None, C_HEADS, tm, LANES), lambda s, b: (b, 0, s, 0)), tok(C_HEADS * C_V)])
    out_shape = ([bf(bsz, s_len, w) for w in widths]
                 + [bf(bsz, C_HEADS, s_len, LANES), bf(bsz, s_len, C_HEADS * C_V)])
    body = _proj_body
    if prev_out is not None:
        oa, ob, oc, w_out = prev_out
        in_specs += [tok(oa.shape[-1]), tok(ob.shape[-1]), tok(oc.shape[-1]), full(w_out)]
        args += [oa, ob, oc, w_out]
        out_specs.append(tok(d))
        out_shape.append(jax.ShapeDtypeStruct(x.shape, F32))
        body = _out_proj_body
    return pl.pallas_call(
        body,
        grid=grid,
        in_specs=in_specs + [full(a) for a in weights] + [pl.BlockSpec((6, tm, LANES), lambda s, b: (0, s, 0))],
        out_specs=out_specs,
        out_shape=out_shape,
        compiler_params=pltpu.CompilerParams(
            dimension_semantics=("arbitrary", "arbitrary"), vmem_limit_bytes=VMEM_LIMIT),
        name="proj" if prev_out is None else "out_proj",
    )(*args, *weights, tabs)


def _scores(q, k):
    return lax.dot_general(q, k, (((1,), (1,)), ((), ())), preferred_element_type=F32)


def _pipeline_prologue(t, n_q, v_ref, vaug_ref, s_b, m_b):
    @pl.when(lax.rem(jnp.maximum(t - 1, 0), n_q) == 0)
    def _():
        vaug_ref[:, :LANES] = v_ref[...]
        vaug_ref[:, LANES:] = jnp.ones((vaug_ref.shape[0], LANES), BF16)

    @pl.when(t == 0)
    def _():
        s_b[...] = jnp.zeros(s_b.shape, F32)
        m_b[...] = jnp.zeros(m_b.shape, F32)


def _run_parity(t, step, s_a, m_a, s_b, m_b):
    parity = lax.rem(t, 2)

    @pl.when(parity == 0)
    def _():
        step(s_a, m_a, s_b, m_b)

    @pl.when(parity == 1)
    def _():
        step(s_b, m_b, s_a, m_a)


def _softmax_numerators(s_r, m_r, rows, width):
    m = m_r[rows, :]
    return jnp.concatenate(
        [jnp.exp2(s_r[rows, j * LANES:(j + 1) * LANES] - m).astype(BF16) for j in range(width // LANES)], axis=1)


def _stack_queries(q, head_a):
    slabs = [q[:, j * LANES:(j + 1) * LANES] for j in range(N_SLABS)]
    return jnp.concatenate([jnp.where(head_a, s, 0.0) for s in slabs]
                           + [jnp.where(head_a, 0.0, s) for s in slabs], axis=0).astype(BF16)


def _stack_pipe_body(q_ref, k_ref, v_ref, sg_ref, o_ref, s_a, s_b, m_a, m_b, vaug_ref, *, n_q):
    t = pl.program_id(0)
    tq = q_ref.shape[0]
    s_len = k_ref.shape[0]
    _pipeline_prologue(t, n_q, v_ref, vaug_ref, s_b, m_b)

    lane = lax.broadcasted_iota(jnp.int32, (tq, LANES), 1)
    out_low = lane < HALF
    q_stack = _stack_queries(q_ref[...].astype(F32), (lane & QUARTER) == 0)
    blocks = [pl.ds(i * tq, tq) for i in range(AB_HEADS)]

    def step(s_w, m_w, s_r, m_r):
        s = _scores(q_stack, k_ref[...])
        for i in range(AB_HEADS):
            blk = s[i * tq:(i + 1) * tq]
            s_w[blocks[i], :] = blk
            m_w[blocks[i], :] = jnp.broadcast_to(jnp.max(blk, axis=1, keepdims=True), (tq, LANES))
        p = jnp.concatenate([_softmax_numerators(s_r, m_r, blocks[i], s_len) for i in range(AB_HEADS)], axis=0)
        a = jnp.dot(p, vaug_ref[...], preferred_element_type=F32)
        for j in range(N_SLABS):
            lo, hi = a[j * tq:(j + 1) * tq], a[(j + N_SLABS) * tq:(j + N_SLABS + 1) * tq]
            o = jnp.where(out_low, lo[:, :LANES] / lo[:, LANES:], hi[:, :LANES] / hi[:, LANES:])
            cols = slice(j * LANES, (j + 1) * LANES)
            o_ref[:, cols] = (o * sg_ref[:, cols].astype(F32)).astype(BF16)

    _run_parity(t, step, s_a, m_a, s_b, m_b)


def _window_pipe_body(sink_ref, q_ref, k_ref, v_ref, sg_ref, o_ref, s_a, s_b, m_a, m_b, vaug_ref, kpad_ref,
                      *, n_q, n_items):
    t = pl.program_id(0)
    tq = q_ref.shape[0]
    s_len = k_ref.shape[0]
    sub = WINDOW
    kw = 3 * WINDOW
    n_sub = tq // sub
    t_cur = jnp.minimum(t, n_items - 1)
    t_prev = jnp.maximum(t - 1, 0)
    halo = jnp.zeros((WINDOW, 2 * LANES), BF16)

    @pl.when(lax.rem(t_prev, n_q) == 0)
    def _():
        vaug_ref[:WINDOW, :] = halo
        vaug_ref[WINDOW:WINDOW + s_len, :LANES] = v_ref[...]
        vaug_ref[WINDOW:WINDOW + s_len, LANES:] = jnp.ones((s_len, LANES), BF16)
        vaug_ref[WINDOW + s_len:, :] = halo

    @pl.when(lax.rem(t_cur, n_q) == 0)
    def _():
        kpad_ref[:WINDOW, :] = halo[:, :LANES]
        kpad_ref[WINDOW:WINDOW + s_len, :] = k_ref[...]
        kpad_ref[WINDOW + s_len:, :] = halo[:, :LANES]

    @pl.when(t == 0)
    def _():
        s_b[...] = jnp.zeros(s_b.shape, F32)
        m_b[...] = jnp.zeros(m_b.shape, F32)

    lane = lax.broadcasted_iota(jnp.int32, (sub, LANES), 1)
    row = lax.broadcasted_iota(jnp.int32, (sub, LANES), 0)
    head_a = (lane & QUARTER) == 0
    out_low = lane < HALF
    q = q_ref[...].astype(F32)
    row0_cur = lax.rem(t_cur, n_q) * tq
    row0_prev = lax.rem(t_prev, n_q) * tq
    sinks = [sink_ref[i] * LOG2E for i in range(AB_HEADS)]

    def step(s_w, m_w, s_r, m_r):
        for u in range(n_sub):
            r0 = row0_cur + u * sub
            k_win = kpad_ref[pl.ds(pl.multiple_of(r0, sub), kw), :]
            bias_lo = jnp.where(jnp.logical_and(lane >= row, r0 > 0), 0.0, NEG)
            bias_hi = jnp.where(jnp.logical_and(lane <= row, r0 < s_len - sub), 0.0, NEG)
            s = _scores(_stack_queries(q[u * sub:(u + 1) * sub], head_a), k_win)
            for i in range(AB_HEADS):
                blk = s[i * sub:(i + 1) * sub]
                cols = [blk[:, :LANES] + bias_lo, blk[:, LANES:2 * LANES], blk[:, 2 * LANES:] + bias_hi]
                mx = jnp.max(jnp.maximum(jnp.maximum(cols[0], cols[1]), cols[2]), axis=1, keepdims=True)
                rows = pl.ds((u * AB_HEADS + i) * sub, sub)
                for c in range(3):
                    s_w[rows, c * LANES:(c + 1) * LANES] = cols[c]
                m_w[rows, :] = jnp.broadcast_to(jnp.maximum(mx, sinks[i]), (sub, LANES))
        for u in range(n_sub):
            r0 = row0_prev + u * sub
            v_win = vaug_ref[pl.ds(pl.multiple_of(r0, sub), kw), :]
            rows = [pl.ds((u * AB_HEADS + i) * sub, sub) for i in range(AB_HEADS)]
            p = jnp.concatenate([_softmax_numerators(s_r, m_r, rows[i], kw) for i in range(AB_HEADS)], axis=0)
            a = jnp.dot(p, v_win, preferred_element_type=F32)
            for j in range(N_SLABS):
                parts = []
                for i in (j, j + N_SLABS):
                    blk = a[i * sub:(i + 1) * sub]
                    den = blk[:, LANES:] + jnp.exp2(sinks[i] - m_r[rows[i], :])
                    parts.append(blk[:, :LANES] / den)
                cols = slice(j * LANES, (j + 1) * LANES)
                gate = sg_ref[u * sub:(u + 1) * sub, cols].astype(F32)
                o_ref[u * sub:(u + 1) * sub, cols] = (jnp.where(out_low, parts[0], parts[1]) * gate).astype(BF16)

    _run_parity(t, step, s_a, m_a, s_b, m_b)


def _stack_call(name, qab, k, v, sg, blk, tq, sink=None):
    bsz, s_len, _ = qab.shape
    tq = min(tq, s_len)
    n_q = s_len // tq
    n_items = bsz * n_q
    windowed = sink is not None
    cur = lambda t: (jnp.minimum(t, n_items - 1) // n_q, jnp.minimum(t, n_items - 1) % n_q)
    prev = lambda t: (jnp.maximum(t - 1, 0) // n_q, jnp.maximum(t - 1, 0) % n_q)
    width = N_SLABS * LANES
    in_specs = [
        pl.BlockSpec((None, tq, width), lambda t: (*cur(t), blk)),
        pl.BlockSpec((None, s_len, LANES), lambda t: (cur(t)[0], 0, 0)),
        pl.BlockSpec((None, s_len, LANES), lambda t: (prev(t)[0], 0, 0)),
        pl.BlockSpec((None, tq, width), lambda t: (*prev(t), blk)),
    ]
    args = [qab, k, v, sg]
    rows = AB_HEADS * tq
    if windowed:
        in_specs = [pl.BlockSpec(memory_space=pltpu.SMEM)] + in_specs
        args = [sink] + args
        body = functools.partial(_window_pipe_body, n_q=n_q, n_items=n_items)
        padded = s_len + 2 * WINDOW
        scratch = [pltpu.VMEM((rows, 3 * WINDOW), F32), pltpu.VMEM((rows, 3 * WINDOW), F32),
                   pltpu.VMEM((rows, LANES), F32), pltpu.VMEM((rows, LANES), F32),
                   pltpu.VMEM((padded, 2 * LANES), BF16), pltpu.VMEM((padded, LANES), BF16)]
    else:
        body = functools.partial(_stack_pipe_body, n_q=n_q)
        scratch = [pltpu.VMEM((rows, s_len), F32), pltpu.VMEM((rows, s_len), F32),
                   pltpu.VMEM((rows, LANES), F32), pltpu.VMEM((rows, LANES), F32),
                   pltpu.VMEM((s_len, 2 * LANES), BF16)]
    return pl.pallas_call(
        body,
        grid=(n_items + 1,),
        in_specs=in_specs,
        out_specs=pl.BlockSpec((None, tq, width), lambda t: (*prev(t), 0)),
        out_shape=jax.ShapeDtypeStruct((bsz, s_len, width), BF16),
        scratch_shapes=scratch,
        compiler_params=pltpu.CompilerParams(dimension_semantics=("arbitrary",), vmem_limit_bytes=VMEM_LIMIT),
        name=name,
    )(*args)


def _pair_pipe_body(q0_ref, q1_ref, k0_ref, k1_ref, v_ref, sg_ref, o_ref, s_a, s_b, m_a, m_b, vaug_ref, *, n_q):
    t = pl.program_id(0)
    tq, s_len = s_a.shape[1], s_a.shape[2]
    _pipeline_prologue(t, n_q, v_ref, vaug_ref, s_b, m_b)
    out_low = lax.broadcasted_iota(jnp.int32, (tq, LANES), 1) < HALF
    qs, ks = (q0_ref, q1_ref), (k0_ref, k1_ref)

    def step(s_w, m_w, s_r, m_r):
        for hd in range(2):
            s = _scores(qs[hd][...], ks[hd][...])
            s_w[hd] = s
            m_w[hd] = jnp.broadcast_to(jnp.max(s, axis=1, keepdims=True), (tq, LANES))
        outs = []
        for hd in range(2):
            p = _softmax_numerators(s_r.at[hd], m_r.at[hd], slice(None), s_len)
            a = jnp.dot(p, vaug_ref[...], preferred_element_type=F32)
            outs.append(a[:, :LANES] / a[:, LANES:])
        o_ref[...] = (jnp.where(out_low, outs[0], outs[1]) * sg_ref[...].astype(F32)).astype(BF16)

    _run_parity(t, step, s_a, m_a, s_b, m_b)


def _attn_c_call(qc, kc, vc, sg, g_off):
    bsz, s_len, _ = qc.shape
    tq = min(ATTN_C_TQ, s_len)
    n_q = s_len // tq
    n_pairs = C_HEADS // 2
    n_items = bsz * n_pairs * n_q

    def item(t):
        return t // (n_pairs * n_q), (t // n_q) % n_pairs, t % n_q

    cur = lambda t: item(jnp.minimum(t, n_items - 1))
    prev = lambda t: item(jnp.maximum(t - 1, 0))

    def q_spec(o):
        def imap(t):
            b, p, i = cur(t)
            return b, i, 2 * p + o
        return pl.BlockSpec((None, tq, LANES), imap)

    def k_spec(o):
        def imap(t):
            b, p, i = cur(t)
            return b, 2 * p + o, 0, 0
        return pl.BlockSpec((None, None, s_len, LANES), imap)

    def v_map(t):
        b, p, i = prev(t)
        return b, 0, p

    def sg_map(t):
        b, p, i = prev(t)
        return b, i, g_off + p

    def o_map(t):
        b, p, i = prev(t)
        return b, i, p

    return pl.pallas_call(
        functools.partial(_pair_pipe_body, n_q=n_q),
        grid=(n_items + 1,),
        in_specs=[q_spec(0), q_spec(1), k_spec(0), k_spec(1),
                  pl.BlockSpec((None, s_len, LANES), v_map), pl.BlockSpec((None, tq, LANES), sg_map)],
        out_specs=pl.BlockSpec((None, tq, LANES), o_map),
        out_shape=jax.ShapeDtypeStruct((bsz, s_len, n_pairs * LANES), BF16),
        scratch_shapes=[pltpu.VMEM((2, tq, s_len), F32), pltpu.VMEM((2, tq, s_len), F32),
                        pltpu.VMEM((2, tq, LANES), F32), pltpu.VMEM((2, tq, LANES), F32),
                        pltpu.VMEM((s_len, 2 * LANES), BF16)],
        compiler_params=pltpu.CompilerParams(dimension_semantics=("arbitrary",), vmem_limit_bytes=VMEM_LIMIT),
        name="attn_c",
    )(qc, qc, kc, kc, vc, sg)


def _out_body(x_ref, oa_ref, ob_ref, oc_ref, w_ref, fg_ref, y_ref):
    o = jnp.concatenate([oa_ref[...], ob_ref[...], oc_ref[...]], axis=1)
    r = x_ref[...] + jnp.dot(o, w_ref[...], preferred_element_type=F32)
    y_ref[...] = r * lax.rsqrt(jnp.mean(r * r, axis=-1, keepdims=True) + EPS) * fg_ref[...]


def _out_call(x, oa, ob, oc, w_out, fg):
    bsz, s_len, d = x.shape
    tm = min(OUT_TM, s_len)
    tok = lambda a: pl.BlockSpec((None, tm, a.shape[-1]), lambda b, s: (b, s, 0))
    full = lambda a: pl.BlockSpec(a.shape, lambda b, s: (0,) * a.ndim)
    return pl.pallas_call(
        _out_body,
        grid=(bsz, s_len // tm),
        in_specs=[tok(x), tok(oa), tok(ob), tok(oc), full(w_out), full(fg)],
        out_specs=tok(x),
        out_shape=jax.ShapeDtypeStruct(x.shape, F32),
        compiler_params=pltpu.CompilerParams(
            dimension_semantics=("arbitrary", "arbitrary"), vmem_limit_bytes=VMEM_LIMIT),
        name="out_final",
    )(x, oa, ob, oc, w_out, fg)


def kernel(x, norm_g, w_in, a_q_norm, a_k_norm, b_sink, c_q_norm, c_kv_norm, c_w_uq, c_w_ukv, w_out, final_g):
    depth = w_in.shape[0]
    tabs = _rope_tables(x.shape[1])
    e_mat = _group_sum_matrix()
    row = lambda v: v.astype(F32)[None, :]
    prev_out = None
    for l in range(depth):
        outs = _proj_call(
            x, row(norm_g[l]), _pack_w_in(w_in[l]), row(a_q_norm[l][_DIM_A]), row(a_k_norm[l][_DIM_A]),
            row(jnp.pad(c_q_norm[l], (0, C_Q_PAD - C_Q_RANK))), row(c_kv_norm[l]),
            _pack_w_uq(c_w_uq[l]), _pack_w_ukv(c_w_ukv[l]), e_mat, tabs, prev_out=prev_out)
        if prev_out is not None:
            x = outs[-1]
        qab, ka, va, kb, vb, sg, qc, kc, vc = outs[:9]
        oa = _stack_call("attn_a", qab, ka, va, sg, 0, ATTN_A_TQ)
        ob = _stack_call("attn_b", qab, kb, vb, sg, 1, ATTN_B_TQ, sink=b_sink[l].astype(F32))
        oc = _attn_c_call(qc, kc, vc, sg, 2 * N_SLABS)
        prev_out = (oa, ob, oc, _pack_w_out(w_out[l]))
    return _out_call(x, *prev_out, row(final_g))
```

```python
import functools

import numpy as np
import jax
import jax.numpy as jnp
from jax import lax
from jax.experimental import pallas as pl
from jax.experimental.pallas import tpu as pltpu

F32 = jnp.float32
BF16 = jnp.bfloat16

GRID_W = 64
ROPE_THETA = 10000.0
EPS = 1e-6
HEAD_DIM = 64
WINDOW = 128
AB_HEADS = 6
C_HEADS = 4
C_NOPE = 64
C_ROPE = 32
C_V = 64
C_Q_RANK = 192
C_KV_RANK = 128

LANES = 128
HALF = LANES // 2
QUARTER = LANES // 4
N_SLABS = AB_HEADS // 2
C_Q_PAD = 256
NEG = -1e30
LOG2E = 1.4426950408889634

IN_AQ, IN_AK, IN_AV, IN_AG = 0, 384, 512, 640
IN_BQ, IN_BK, IN_BV, IN_BG = 1024, 1408, 1536, 1664
IN_CQ, IN_CKV, IN_CKR, IN_CG = 2048, 2240, 2368, 2400

OFF_CQ = 0
OFF_CKV = 256
OFF_CKR = 384
OFF_QA, OFF_KA = 512, 896
OFF_QB, OFF_KB = 1024, 1408
OFF_VA, OFF_VB = 1536, 1664
OFF_G = 1792
PACKED_W = 2816
MIX_W = 1024

PROJ_TM = 512
ATTN_A_TQ = 256
ATTN_B_TQ = 512
ATTN_C_TQ = 1024
OUT_TM = 1024
VMEM_LIMIT = 56 * 1024 * 1024


_LANE = np.arange(LANES)
_HEAD_OF_LANE = (_LANE // QUARTER) % 2


def _dim_of_lane(first_half_dims, partner):
    f = np.asarray(first_half_dims)
    return np.where(_LANE < HALF, f[_LANE % QUARTER], f[_LANE % QUARTER] + partner)


_DIM_A = _dim_of_lane(list(range(16)) + list(range(32, 48)), 16)
_DIM_B = _dim_of_lane(list(range(32)), 32)

_C_ROPE_OF_LANE = np.full(LANES, -1)
_C_ROPE_OF_LANE[0:16] = np.arange(16)
_C_ROPE_OF_LANE[64:80] = 16 + np.arange(16)


def _pack_w_in(w):
    d = w.shape[0]
    w = w.astype(BF16)

    def pair_lanes(cols, rotary_blocks):
        i = HEAD_DIM // (2 * rotary_blocks)
        p = cols.shape[1] // (2 * HEAD_DIM)
        x = cols.reshape(d, 2, p, rotary_blocks, 2, i)
        return x.transpose(0, 2, 4, 1, 3, 5).reshape(d, p * LANES)

    def pair_heads(cols):
        p = cols.shape[1] // (2 * HEAD_DIM)
        return cols.reshape(d, 2, p, HEAD_DIM).transpose(0, 2, 1, 3).reshape(d, p * LANES)

    zeros = lambda n: jnp.zeros((d, n), w.dtype)
    sl = lambda a, n: w[:, a:a + n]
    ckr = sl(IN_CKR, C_ROPE)
    pieces = {
        OFF_CQ: jnp.concatenate([sl(IN_CQ, C_Q_RANK), zeros(C_Q_PAD - C_Q_RANK)], 1),
        OFF_CKV: sl(IN_CKV, C_KV_RANK),
        OFF_CKR: jnp.concatenate([ckr[:, :16], zeros(48), ckr[:, 16:], zeros(48)], 1),
        OFF_QA: pair_lanes(sl(IN_AQ, 384), 2), OFF_KA: pair_lanes(sl(IN_AK, 128), 2),
        OFF_QB: pair_lanes(sl(IN_BQ, 384), 1), OFF_KB: pair_lanes(sl(IN_BK, 128), 1),
        OFF_VA: sl(IN_AV, 128), OFF_VB: sl(IN_BV, 128),
        OFF_G: jnp.concatenate([pair_heads(sl(IN_AG, 384)), pair_heads(sl(IN_BG, 384)), sl(IN_CG, 256)], 1),
    }
    ordered, off = [], 0
    for o in sorted(pieces):
        assert o == off
        ordered.append(pieces[o])
        off += pieces[o].shape[1]
    assert off == PACKED_W
    return jnp.concatenate(ordered, 1)


def _pack_w_out(w):
    dd = w.shape[1]
    n = AB_HEADS * HEAD_DIM
    pair = lambda r: r.reshape(2, N_SLABS, HEAD_DIM, dd).transpose(1, 0, 2, 3).reshape(n, dd)
    return jnp.concatenate([pair(w[:n]), pair(w[n:2 * n]), w[2 * n:]], 0).astype(BF16)


def _c_slab(x, rope):
    lead = x.shape[:-1]
    zeros = lambda n: jnp.zeros(lead + (n,), x.dtype)
    r0, r1 = (rope[..., :16], rope[..., 16:]) if rope is not None else (zeros(16), zeros(16))
    return jnp.concatenate([r0, x[..., :48], r1, x[..., 48:C_NOPE], zeros(LANES - C_NOPE - C_ROPE)], axis=-1)


def _pack_w_uq(w):
    w = w.astype(BF16).reshape(C_Q_RANK, C_HEADS, C_NOPE + C_ROPE)
    slab = _c_slab(w[..., :C_NOPE], w[..., C_NOPE:]).reshape(C_Q_RANK, C_HEADS * LANES)
    return jnp.pad(slab, ((0, C_Q_PAD - C_Q_RANK), (0, 0)))


def _pack_w_ukv(w):
    w = w.astype(BF16).reshape(C_KV_RANK, C_HEADS, C_NOPE + C_V)
    k = _c_slab(w[..., :C_NOPE], None).reshape(C_KV_RANK, C_HEADS * LANES)
    v = w[..., C_NOPE:].reshape(C_KV_RANK, C_HEADS * C_V)
    return jnp.concatenate([k, v], axis=1)


def _group_sum_matrix():
    head = np.concatenate([_HEAD_OF_LANE, 2 + _HEAD_OF_LANE])
    return jnp.asarray(head[:, None] == head[None, :], BF16)


def _rope_tables(s_len):
    t = jnp.arange(s_len)[:, None]
    sign = np.where(_LANE < HALF, -1.0, 1.0).astype(np.float32)[None, :]

    def tables(pos, freq, dim, active=None):
        expo = (-(2 * freq).astype(np.float32) / np.float32(dim))[None, :]
        ang = pos.astype(F32) * (ROPE_THETA ** jnp.asarray(expo))
        cos, sin = jnp.cos(ang), jnp.sin(ang) * sign
        if active is not None:
            cos, sin = jnp.where(active[None, :], cos, 1.0), jnp.where(active[None, :], sin, 0.0)
        return [cos, sin]

    pos_a = jnp.where((_DIM_A < HEAD_DIM // 2)[None, :], t // GRID_W, t % GRID_W)
    rope_c = _C_ROPE_OF_LANE >= 0
    return jnp.stack(tables(pos_a, _DIM_A % 16, HEAD_DIM // 2)
                     + tables(t, _DIM_B % 32, HEAD_DIM)
                     + tables(t, np.maximum(_C_ROPE_OF_LANE, 0) % 16, C_ROPE, rope_c))


def _proj_body(x_ref, *refs):
    _proj_compute(x_ref[...], *refs)


def _out_proj_body(x_ref, oa_ref, ob_ref, oc_ref, wo_ref, *refs):
    proj_refs, x_out_ref = refs[:-1], refs[-1]
    o = jnp.concatenate([oa_ref[...], ob_ref[...], oc_ref[...]], axis=1)
    x = x_ref[...] + jnp.dot(o, wo_ref[...], preferred_element_type=F32)
    x_out_ref[...] = x
    _proj_compute(x, *proj_refs)


def _proj_compute(x, g_ref, w_ref, aqn_ref, akn_ref, cqn_ref, ckvn_ref, wuq_ref, wukv_ref, e_ref, tab_ref,
                  qab_ref, ka_ref, va_ref, kb_ref, vb_ref, sg_ref, qc_ref, kc_ref, vc_ref):
    h = x * lax.rsqrt(jnp.mean(x * x, axis=-1, keepdims=True) + EPS) * g_ref[...]
    z = jnp.dot(h.astype(BF16), w_ref[...], preferred_element_type=F32)

    def rope(xs, mixer):
        return xs * tab_ref[2 * mixer] + pltpu.roll(xs, HALF, 1) * tab_ref[2 * mixer + 1]

    def slab(off):
        return z[:, off:off + LANES]

    def head_sumsq(s0, s1):
        t = jnp.concatenate([s0 * s0, s1 * s1], axis=1).astype(BF16)
        ss = jnp.dot(t, e_ref[...], preferred_element_type=F32)
        return ss[:, :LANES], ss[:, LANES:]

    def head_norm(xs, ss, gain):
        return xs * lax.rsqrt(ss * (1.0 / HEAD_DIM) + EPS) * gain

    cq = z[:, OFF_CQ:OFF_CQ + C_Q_PAD]
    cq = cq * lax.rsqrt(jnp.sum(cq * cq, axis=-1, keepdims=True) * (1.0 / C_Q_RANK) + EPS) * cqn_ref[...]
    qc = jnp.dot(cq.astype(BF16), wuq_ref[...], preferred_element_type=F32)
    scale_c = (C_NOPE + C_ROPE) ** -0.5 * LOG2E
    for hh in range(C_HEADS):
        q = rope(qc[:, hh * LANES:(hh + 1) * LANES], 2)
        qc_ref[:, hh * LANES:(hh + 1) * LANES] = (q * scale_c).astype(BF16)
    ckv = z[:, OFF_CKV:OFF_CKV + C_KV_RANK]
    ckv = ckv * lax.rsqrt(jnp.mean(ckv * ckv, axis=-1, keepdims=True) + EPS) * ckvn_ref[...]
    kv = jnp.dot(ckv.astype(BF16), wukv_ref[...], preferred_element_type=F32)
    kr = rope(slab(OFF_CKR), 2)
    for hh in range(C_HEADS):
        kc_ref[hh] = (kv[:, hh * LANES:(hh + 1) * LANES] + kr).astype(BF16)
    vc_ref[...] = kv[:, C_HEADS * LANES:].astype(BF16)

    scale_ab = HEAD_DIM ** -0.5 * LOG2E
    qa = [slab(OFF_QA + j * LANES) for j in range(N_SLABS)]
    ka = slab(OFF_KA)
    ss0, ss1 = head_sumsq(qa[0], qa[1])
    ss2, ssk = head_sumsq(qa[2], ka)
    for j, ss in enumerate((ss0, ss1, ss2)):
        q = rope(head_norm(qa[j], ss, aqn_ref[...]), 0)
        qab_ref[:, j * LANES:(j + 1) * LANES] = (q * scale_ab).astype(BF16)
    ka_ref[...] = rope(head_norm(ka, ssk, akn_ref[...]), 0).astype(BF16)
    for j in range(N_SLABS):
        q = rope(slab(OFF_QB + j * LANES), 1)
        qab_ref[:, (N_SLABS + j) * LANES:(N_SLABS + j + 1) * LANES] = (q * scale_ab).astype(BF16)
    kb_ref[...] = rope(slab(OFF_KB), 1).astype(BF16)
    va_ref[...] = slab(OFF_VA).astype(BF16)
    vb_ref[...] = slab(OFF_VB).astype(BF16)

    gz = z[:, OFF_G:OFF_G + MIX_W]
    sg_ref[...] = (gz / (1.0 + jnp.exp(-gz))).astype(BF16)


def _proj_call(x, g, w_packed, aqn, akn, cqn, ckvn, wuq, wukv, e_mat, tabs, prev_out=None):
    bsz, s_len, d = x.shape
    tm = min(PROJ_TM, s_len)
    grid = (s_len // tm, bsz)
    full = lambda a: pl.BlockSpec(a.shape, lambda s, b: (0,) * a.ndim)
    tok = lambda w: pl.BlockSpec((None, tm, w), lambda s, b: (b, s, 0))
    bf = lambda *shape: jax.ShapeDtypeStruct(shape, BF16)
    widths = [2 * N_SLABS * LANES, LANES, LANES, LANES, LANES, MIX_W, C_HEADS * LANES]
    weights = [g, w_packed, aqn, akn, cqn, ckvn, wuq, wukv, e_mat]
    in_specs = [tok(d)]
    args = [x]
    out_specs = ([tok(w) for w in widths]
                 + [pl.BlockSpec((None, C_HEADS, tm, LANES), lambda s, b: (b, 0, s, 0)), tok(C_HEADS * C_V)])
    out_shape = ([bf(bsz, s_len, w) for w in widths]
                 + [bf(bsz, C_HEADS, s_len, LANES), bf(bsz, s_len, C_HEADS * C_V)])
    body = _proj_body
    if prev_out is not None:
        oa, ob, oc, w_out = prev_out
        in_specs += [tok(oa.shape[-1]), tok(ob.shape[-1]), tok(oc.shape[-1]), full(w_out)]
        args += [oa, ob, oc, w_out]
        out_specs.append(tok(d))
        out_shape.append(jax.ShapeDtypeStruct(x.shape, F32))
        body = _out_proj_body
    return pl.pallas_call(
        body,
        grid=grid,
        in_specs=in_specs + [full(a) for a in weights] + [pl.BlockSpec((6, tm, LANES), lambda s, b: (0, s, 0))],
        out_specs=out_specs,
        out_shape=out_shape,
        compiler_params=pltpu.CompilerParams(
            dimension_semantics=("arbitrary", "arbitrary"), vmem_limit_bytes=VMEM_LIMIT),
        name="proj" if prev_out is None else "out_proj",
    )(*args, *weights, tabs)


def _scores(q, k):
    return lax.dot_general(q, k, (((1,), (1,)), ((), ())), preferred_element_type=F32)


def _pipeline_prologue(t, n_q, v_ref, vaug_ref, s_b, m_b):
    @pl.when(lax.rem(jnp.maximum(t - 1, 0), n_q) == 0)
    def _():
        vaug_ref[:, :LANES] = v_ref[...]
        vaug_ref[:, LANES:] = jnp.ones((vaug_ref.shape[0], LANES), BF16)

    @pl.when(t == 0)
    def _():
        s_b[...] = jnp.zeros(s_b.shape, F32)
        m_b[...] = jnp.zeros(m_b.shape, F32)


def _run_parity(t, step, s_a, m_a, s_b, m_b):
    parity = lax.rem(t, 2)

    @pl.when(parity == 0)
    def _():
        step(s_a, m_a, s_b, m_b)

    @pl.when(parity == 1)
    def _():
        step(s_b, m_b, s_a, m_a)


def _softmax_numerators(s_r, m_r, rows, width):
    m = m_r[rows, :]
    return jnp.concatenate(
        [jnp.exp2(s_r[rows, j * LANES:(j + 1) * LANES] - m).astype(BF16) for j in range(width // LANES)], axis=1)


def _stack_queries(q, head_a):
    slabs = [q[:, j * LANES:(j + 1) * LANES] for j in range(N_SLABS)]
    return jnp.concatenate([jnp.where(head_a, s, 0.0) for s in slabs]
                           + [jnp.where(head_a, 0.0, s) for s in slabs], axis=0).astype(BF16)


def _stack_pipe_body(q_ref, k_ref, v_ref, sg_ref, o_ref, s_a, s_b, m_a, m_b, vaug_ref, *, n_q):
    t = pl.program_id(0)
    tq = q_ref.shape[0]
    s_len = k_ref.shape[0]
    _pipeline_prologue(t, n_q, v_ref, vaug_ref, s_b, m_b)

    lane = lax.broadcasted_iota(jnp.int32, (tq, LANES), 1)
    out_low = lane < HALF
    blocks = [pl.ds(i * tq, tq) for i in range(AB_HEADS)]

    def step(s_w, m_w, s_r, m_r):
        q_stack = _stack_queries(q_ref[...].astype(F32), (lane & QUARTER) == 0)
        s = _scores(q_stack, k_ref[...])
        for i in range(AB_HEADS):
            blk = s[i * tq:(i + 1) * tq]
            s_w[blocks[i], :] = blk
            m_w[blocks[i], :] = jnp.broadcast_to(jnp.max(blk, axis=1, keepdims=True), (tq, LANES))
        p = jnp.concatenate([_softmax_numerators(s_r, m_r, blocks[i], s_len) for i in range(AB_HEADS)], axis=0)
        a = jnp.dot(p, vaug_ref[...], preferred_element_type=F32)
        for j in range(N_SLABS):
            lo, hi = a[j * tq:(j + 1) * tq], a[(j + N_SLABS) * tq:(j + N_SLABS + 1) * tq]
            o = jnp.where(out_low, lo[:, :LANES] / lo[:, LANES:], hi[:, :LANES] / hi[:, LANES:])
            cols = slice(j * LANES, (j + 1) * LANES)
            o_ref[:, cols] = (o * sg_ref[:, cols].astype(F32)).astype(BF16)

    _run_parity(t, step, s_a, m_a, s_b, m_b)


def _window_pipe_body(sink_ref, q_ref, k_ref, v_ref, sg_ref, o_ref, s_a, s_b, m_a, m_b, vaug_ref, kpad_ref,
                      *, n_q, n_items):
    t = pl.program_id(0)
    tq = q_ref.shape[0]
    s_len = k_ref.shape[0]
    sub = WINDOW
    kw = 3 * WINDOW
    n_sub = tq // sub
    t_cur = jnp.minimum(t, n_items - 1)
    t_prev = jnp.maximum(t - 1, 0)
    halo = jnp.zeros((WINDOW, 2 * LANES), BF16)

    @pl.when(lax.rem(t_prev, n_q) == 0)
    def _():
        vaug_ref[:WINDOW, :] = halo
        vaug_ref[WINDOW:WINDOW + s_len, :LANES] = v_ref[...]
        vaug_ref[WINDOW:WINDOW + s_len, LANES:] = jnp.ones((s_len, LANES), BF16)
        vaug_ref[WINDOW + s_len:, :] = halo

    @pl.when(lax.rem(t_cur, n_q) == 0)
    def _():
        kpad_ref[:WINDOW, :] = halo[:, :LANES]
        kpad_ref[WINDOW:WINDOW + s_len, :] = k_ref[...]
        kpad_ref[WINDOW + s_len:, :] = halo[:, :LANES]

    @pl.when(t == 0)
    def _():
        s_b[...] = jnp.zeros(s_b.shape, F32)
        m_b[...] = jnp.zeros(m_b.shape, F32)

    lane = lax.broadcasted_iota(jnp.int32, (sub, LANES), 1)
    row = lax.broadcasted_iota(jnp.int32, (sub, LANES), 0)
    head_a = (lane & QUARTER) == 0
    out_low = lane < HALF
    row0_cur = lax.rem(t_cur, n_q) * tq
    row0_prev = lax.rem(t_prev, n_q) * tq
    sinks = [sink_ref[i] * LOG2E for i in range(AB_HEADS)]

    def step(s_w, m_w, s_r, m_r):
        q = q_ref[...].astype(F32)
        for u in range(n_sub):
            r0 = row0_cur + u * sub
            k_win = kpad_ref[pl.ds(pl.multiple_of(r0, sub), kw), :]
            bias_lo = jnp.where(jnp.logical_and(lane >= row, r0 > 0), 0.0, NEG)
            bias_hi = jnp.where(jnp.logical_and(lane <= row, r0 < s_len - sub), 0.0, NEG)
            s = _scores(_stack_queries(q[u * sub:(u + 1) * sub], head_a), k_win)
            for i in range(AB_HEADS):
                blk = s[i * sub:(i + 1) * sub]
                cols = [blk[:, :LANES] + bias_lo, blk[:, LANES:2 * LANES], blk[:, 2 * LANES:] + bias_hi]
                mx = jnp.max(jnp.maximum(jnp.maximum(cols[0], cols[1]), cols[2]), axis=1, keepdims=True)
                rows = pl.ds((u * AB_HEADS + i) * sub, sub)
                for c in range(3):
                    s_w[rows, c * LANES:(c + 1) * LANES] = cols[c]
                m_w[rows, :] = jnp.broadcast_to(jnp.maximum(mx, sinks[i]), (sub, LANES))
        for u in range(n_sub):
            r0 = row0_prev + u * sub
            v_win = vaug_ref[pl.ds(pl.multiple_of(r0, sub), kw), :]
            rows = [pl.ds((u * AB_HEADS + i) * sub, sub) for i in range(AB_HEADS)]
            p = jnp.concatenate([_softmax_numerators(s_r, m_r, rows[i], kw) for i in range(AB_HEADS)], axis=0)
            a = jnp.dot(p, v_win, preferred_element_type=F32)
            for j in range(N_SLABS):
                parts = []
                for i in (j, j + N_SLABS):
                    blk = a[i * sub:(i + 1) * sub]
                    den = blk[:, LANES:] + jnp.exp2(sinks[i] - m_r[rows[i], :])
                    parts.append(blk[:, :LANES] / den)
                cols = slice(j * LANES, (j + 1) * LANES)
                gate = sg_ref[u * sub:(u + 1) * sub, cols].astype(F32)
                o_ref[u * sub:(u + 1) * sub, cols] = (jnp.where(out_low, parts[0], parts[1]) * gate).astype(BF16)

    _run_parity(t, step, s_a, m_a, s_b, m_b)


def _stack_call(name, qab, k, v, sg, blk, tq, sink=None):
    bsz, s_len, _ = qab.shape
    tq = min(tq, s_len)
    n_q = s_len // tq
    n_items = bsz * n_q
    windowed = sink is not None
    cur = lambda t: (jnp.minimum(t, n_items - 1) // n_q, jnp.minimum(t, n_items - 1) % n_q)
    prev = lambda t: (jnp.maximum(t - 1, 0) // n_q, jnp.maximum(t - 1, 0) % n_q)
    width = N_SLABS * LANES
    in_specs = [
        pl.BlockSpec((None, tq, width), lambda t: (*cur(t), blk)),
        pl.BlockSpec((None, s_len, LANES), lambda t: (cur(t)[0], 0, 0)),
        pl.BlockSpec((None, s_len, LANES), lambda t: (prev(t)[0], 0, 0)),
        pl.BlockSpec((None, tq, width), lambda t: (*prev(t), blk)),
    ]
    args = [qab, k, v, sg]
    rows = AB_HEADS * tq
    if windowed:
        in_specs = [pl.BlockSpec(memory_space=pltpu.SMEM)] + in_specs
        args = [sink] + args
        body = functools.partial(_window_pipe_body, n_q=n_q, n_items=n_items)
        padded = s_len + 2 * WINDOW
        scratch = [pltpu.VMEM((rows, 3 * WINDOW), F32), pltpu.VMEM((rows, 3 * WINDOW), F32),
                   pltpu.VMEM((rows, LANES), F32), pltpu.VMEM((rows, LANES), F32),
                   pltpu.VMEM((padded, 2 * LANES), BF16), pltpu.VMEM((padded, LANES), BF16)]
    else:
        body = functools.partial(_stack_pipe_body, n_q=n_q)
        scratch = [pltpu.VMEM((rows, s_len), F32), pltpu.VMEM((rows, s_len), F32),
                   pltpu.VMEM((rows, LANES), F32), pltpu.VMEM((rows, LANES), F32),
                   pltpu.VMEM((s_len, 2 * LANES), BF16)]
    return pl.pallas_call(
        body,
        grid=(n_items + 1,),
        in_specs=in_specs,
        out_specs=pl.BlockSpec((None, tq, width), lambda t: (*prev(t), 0)),
        out_shape=jax.ShapeDtypeStruct((bsz, s_len, width), BF16),
        scratch_shapes=scratch,
        compiler_params=pltpu.CompilerParams(dimension_semantics=("arbitrary",), vmem_limit_bytes=VMEM_LIMIT),
        name=name,
    )(*args)


def _pair_pipe_body(q0_ref, q1_ref, k0_ref, k1_ref, v_ref, sg_ref, o_ref, s_a, s_b, m_a, m_b, vaug_ref, *, n_q):
    t = pl.program_id(0)
    tq, s_len = s_a.shape[1], s_a.shape[2]
    _pipeline_prologue(t, n_q, v_ref, vaug_ref, s_b, m_b)
    out_low = lax.broadcasted_iota(jnp.int32, (tq, LANES), 1) < HALF
    qs, ks = (q0_ref, q1_ref), (k0_ref, k1_ref)

    def step(s_w, m_w, s_r, m_r):
        for hd in range(2):
            s = _scores(qs[hd][...], ks[hd][...])
            s_w[hd] = s
            m_w[hd] = jnp.broadcast_to(jnp.max(s, axis=1, keepdims=True), (tq, LANES))
        outs = []
        for hd in range(2):
            p = _softmax_numerators(s_r.at[hd], m_r.at[hd], slice(None), s_len)
            a = jnp.dot(p, vaug_ref[...], preferred_element_type=F32)
            outs.append(a[:, :LANES] / a[:, LANES:])
        o_ref[...] = (jnp.where(out_low, outs[0], outs[1]) * sg_ref[...].astype(F32)).astype(BF16)

    _run_parity(t, step, s_a, m_a, s_b, m_b)


def _attn_c_call(qc, kc, vc, sg, g_off):
    bsz, s_len, _ = qc.shape
    tq = min(ATTN_C_TQ, s_len)
    n_q = s_len // tq
    n_pairs = C_HEADS // 2
    n_items = bsz * n_pairs * n_q

    def item(t):
        return t // (n_pairs * n_q), (t // n_q) % n_pairs, t % n_q

    cur = lambda t: item(jnp.minimum(t, n_items - 1))
    prev = lambda t: item(jnp.maximum(t - 1, 0))

    def q_spec(o):
        def imap(t):
            b, p, i = cur(t)
            return b, i, 2 * p + o
        return pl.BlockSpec((None, tq, LANES), imap)

    def k_spec(o):
        def imap(t):
            b, p, i = cur(t)
            return b, 2 * p + o, 0, 0
        return pl.BlockSpec((None, None, s_len, LANES), imap)

    def v_map(t):
        b, p, i = prev(t)
        return b, 0, p

    def sg_map(t):
        b, p, i = prev(t)
        return b, i, g_off + p

    def o_map(t):
        b, p, i = prev(t)
        return b, i, p

    return pl.pallas_call(
        functools.partial(_pair_pipe_body, n_q=n_q),
        grid=(n_items + 1,),
        in_specs=[q_spec(0), q_spec(1), k_spec(0), k_spec(1),
                  pl.BlockSpec((None, s_len, LANES), v_map), pl.BlockSpec((None, tq, LANES), sg_map)],
        out_specs=pl.BlockSpec((None, tq, LANES), o_map),
        out_shape=jax.ShapeDtypeStruct((bsz, s_len, n_pairs * LANES), BF16),
        scratch_shapes=[pltpu.VMEM((2, tq, s_len), F32), pltpu.VMEM((2, tq, s_len), F32),
                        pltpu.VMEM((2, tq, LANES), F32), pltpu.VMEM((2, tq, LANES), F32),
                        pltpu.VMEM((s_len, 2 * LANES), BF16)],
        compiler_params=pltpu.CompilerParams(dimension_semantics=("arbitrary",), vmem_limit_bytes=VMEM_LIMIT),
        name="attn_c",
    )(qc, qc, kc, kc, vc, sg)


def _out_body(x_ref, oa_ref, ob_ref, oc_ref, w_ref, fg_ref, y_ref):
    o = jnp.concatenate([oa_ref[...], ob_ref[...], oc_ref[...]], axis=1)
    r = x_ref[...] + jnp.dot(o, w_ref[...], preferred_element_type=F32)
    y_ref[...] = r * lax.rsqrt(jnp.mean(r * r, axis=-1, keepdims=True) + EPS) * fg_ref[...]


def _out_call(x, oa, ob, oc, w_out, fg):
    bsz, s_len, d = x.shape
    tm = min(OUT_TM, s_len)
    tok = lambda a: pl.BlockSpec((None, tm, a.shape[-1]), lambda b, s: (b, s, 0))
    full = lambda a: pl.BlockSpec(a.shape, lambda b, s: (0,) * a.ndim)
    return pl.pallas_call(
        _out_body,
        grid=(bsz, s_len // tm),
        in_specs=[tok(x), tok(oa), tok(ob), tok(oc), full(w_out), full(fg)],
        out_specs=tok(x),
        out_shape=jax.ShapeDtypeStruct(x.shape, F32),
        compiler_params=pltpu.CompilerParams(
            dimension_semantics=("arbitrary", "arbitrary"), vmem_limit_bytes=VMEM_LIMIT),
        name="out_final",
    )(x, oa, ob, oc, w_out, fg)


def kernel(x, norm_g, w_in, a_q_norm, a_k_norm, b_sink, c_q_norm, c_kv_norm, c_w_uq, c_w_ukv, w_out, final_g):
    depth = w_in.shape[0]
    tabs = _rope_tables(x.shape[1])
    e_mat = _group_sum_matrix()
    row = lambda v: v.astype(F32)[None, :]
    prev_out = None
    for l in range(depth):
        outs = _proj_call(
            x, row(norm_g[l]), _pack_w_in(w_in[l]), row(a_q_norm[l][_DIM_A]), row(a_k_norm[l][_DIM_A]),
            row(jnp.pad(c_q_norm[l], (0, C_Q_PAD - C_Q_RANK))), row(c_kv_norm[l]),
            _pack_w_uq(c_w_uq[l]), _pack_w_ukv(c_w_ukv[l]), e_mat, tabs, prev_out=prev_out)
        if prev_out is not None:
            x = outs[-1]
        qab, ka, va, kb, vb, sg, qc, kc, vc = outs[:9]
        oa = _stack_call("attn_a", qab, ka, va, sg, 0, ATTN_A_TQ)
        ob = _stack_call("attn_b", qab, kb, vb, sg, 1, ATTN_B_TQ, sink=b_sink[l].astype(F32))
        oc = _attn_c_call(qc, kc, vc, sg, 2 * N_SLABS)
        prev_out = (oa, ob, oc, _pack_w_out(w_out[l]))
    return _out_call(x, *prev_out, row(final_g))
```

```python
import functools

import numpy as np
import jax
import jax.numpy as jnp
from jax import lax
from jax.experimental import pallas as pl
from jax.experimental.pallas import tpu as pltpu

F32 = jnp.float32
BF16 = jnp.bfloat16

GRID_W = 64
ROPE_THETA = 10000.0
EPS = 1e-6
HEAD_DIM = 64
WINDOW = 128
AB_HEADS = 6
C_HEADS = 4
C_NOPE = 64
C_ROPE = 32
C_V = 64
C_Q_RANK = 192
C_KV_RANK = 128

LANES = 128
HALF = LANES // 2
QUARTER = LANES // 4
N_SLABS = AB_HEADS // 2
AB_W = AB_HEADS * HEAD_DIM
C_Q_PAD = 256
NEG = -1e30
LOG2E = 1.4426950408889634

IN_AQ, IN_AK, IN_AV, IN_AG = 0, 384, 512, 640
IN_BQ, IN_BK, IN_BV, IN_BG = 1024, 1408, 1536, 1664
IN_CQ, IN_CKV, IN_CKR, IN_CG = 2048, 2240, 2368, 2400
IN_AB_W = 2048

AB_QA, AB_KA, AB_QB, AB_KB, AB_VA, AB_VB, AB_GA, AB_GB = 0, 384, 512, 896, 1024, 1152, 1280, 1664
C_CQ, C_CKV, C_CKR, C_G, PACKED_C_W = 0, 256, 384, 512, 768
MIX_W = 1024

PACK_TM = 512
PROJ_TM = 512
ATTN_A_TQ = 256
ATTN_B_TQ = 512
ATTN_C_TQ = 1024
OUT_TM = 1024
VMEM_LIMIT = 56 * 1024 * 1024


_LANE = np.arange(LANES)
_HEAD_OF_LANE = (_LANE // QUARTER) % 2


def _dim_of_lane(first_half_dims, partner):
    f = np.asarray(first_half_dims)
    return np.where(_LANE < HALF, f[_LANE % QUARTER], f[_LANE % QUARTER] + partner)


_DIM_A = _dim_of_lane(list(range(16)) + list(range(32, 48)), 16)
_DIM_B = _dim_of_lane(list(range(32)), 32)

_C_ROPE_OF_LANE = np.full(LANES, -1)
_C_ROPE_OF_LANE[0:16] = np.arange(16)
_C_ROPE_OF_LANE[64:80] = 16 + np.arange(16)


def _permutation(src_of_dst):
    n = len(src_of_dst)
    p = np.zeros((n, n), np.float32)
    p[src_of_dst, np.arange(n)] = 1.0
    return jnp.asarray(p, BF16)


def _slab_sources(dim_of_lane, n_slabs):
    return np.concatenate([np.where(_HEAD_OF_LANE == 0, p, p + n_slabs) * HEAD_DIM + dim_of_lane
                           for p in range(n_slabs)])


def _pair_sources():
    d = np.arange(HEAD_DIM)
    return np.concatenate([np.concatenate([p * HEAD_DIM + d, (p + N_SLABS) * HEAD_DIM + d]) for p in range(N_SLABS)])


def _c_slab(x, rope):
    lead = x.shape[:-1]
    zeros = lambda n: jnp.zeros(lead + (n,), x.dtype)
    r0, r1 = (rope[..., :16], rope[..., 16:]) if rope is not None else (zeros(16), zeros(16))
    return jnp.concatenate([r0, x[..., :48], r1, x[..., 48:C_NOPE], zeros(LANES - C_NOPE - C_ROPE)], axis=-1)


def _pack_w_c(w):
    zeros = lambda n: jnp.zeros(w.shape[:-1] + (n,), w.dtype)
    ckr = w[..., IN_CKR:IN_CKR + C_ROPE]
    return jnp.concatenate([
        w[..., IN_CQ:IN_CQ + C_Q_RANK], zeros(C_Q_PAD - C_Q_RANK),
        w[..., IN_CKV:IN_CKV + C_KV_RANK],
        ckr[..., :16], zeros(48), ckr[..., 16:], zeros(48),
        w[..., IN_CG:IN_CG + C_HEADS * C_V]], axis=-1).astype(BF16)


def _pack_w_uq(w):
    n = w.shape[0]
    w = w.astype(BF16).reshape(n, C_Q_RANK, C_HEADS, C_NOPE + C_ROPE)
    slab = _c_slab(w[..., :C_NOPE], w[..., C_NOPE:]).reshape(n, C_Q_RANK, C_HEADS * LANES)
    return jnp.pad(slab, ((0, 0), (0, C_Q_PAD - C_Q_RANK), (0, 0)))


def _pack_w_ukv(w):
    n = w.shape[0]
    w = w.astype(BF16).reshape(n, C_KV_RANK, C_HEADS, C_NOPE + C_V)
    k = _c_slab(w[..., :C_NOPE], None).reshape(n, C_KV_RANK, C_HEADS * LANES)
    v = w[..., C_NOPE:].reshape(n, C_KV_RANK, C_HEADS * C_V)
    return jnp.concatenate([k, v], axis=-1)


def _pack_w_out(w):
    n, _, dd = w.shape
    pair = lambda r: r.reshape(n, 2, N_SLABS, HEAD_DIM, dd).transpose(0, 2, 1, 3, 4).reshape(n, AB_W, dd)
    return jnp.concatenate([pair(w[:, :AB_W]), pair(w[:, AB_W:2 * AB_W]), w[:, 2 * AB_W:]], axis=1).astype(BF16)


def _group_sum_matrix():
    head = np.concatenate([_HEAD_OF_LANE, 2 + _HEAD_OF_LANE])
    return jnp.asarray(head[:, None] == head[None, :], BF16)


def _rope_tables(s_len):
    t = np.arange(s_len, dtype=np.float64)[:, None]
    sign = np.where(_LANE < HALF, -1.0, 1.0)[None, :]

    def tables(pos, freq, dim, active=None):
        ang = pos * (ROPE_THETA ** (-2.0 * freq / dim))[None, :]
        cos, sin = np.cos(ang), np.sin(ang) * sign
        if active is not None:
            cos, sin = np.where(active[None, :], cos, 1.0), np.where(active[None, :], sin, 0.0)
        return [cos, sin]

    pos_a = np.where((_DIM_A < HEAD_DIM // 2)[None, :], t // GRID_W, t % GRID_W)
    rope_c = _C_ROPE_OF_LANE >= 0
    tabs = (tables(pos_a, _DIM_A % 16, HEAD_DIM // 2) + tables(t, _DIM_B % 32, HEAD_DIM)
            + tables(t, np.maximum(_C_ROPE_OF_LANE, 0) % 16, C_ROPE, rope_c))
    return jnp.asarray(np.stack(tabs).astype(np.float32))


def _pack_body(w_ref, pqa_ref, pka_ref, pqb_ref, pkb_ref, pg_ref, o_ref):
    def move(src, dst, width, perm_ref):
        x = w_ref[:, src:src + width].astype(BF16)
        if perm_ref is not None:
            x = jnp.dot(x, perm_ref[...], preferred_element_type=F32).astype(BF16)
        o_ref[:, dst:dst + width] = x

    move(IN_AQ, AB_QA, AB_W, pqa_ref)
    move(IN_AK, AB_KA, LANES, pka_ref)
    move(IN_BQ, AB_QB, AB_W, pqb_ref)
    move(IN_BK, AB_KB, LANES, pkb_ref)
    move(IN_AV, AB_VA, LANES, None)
    move(IN_BV, AB_VB, LANES, None)
    move(IN_AG, AB_GA, AB_W, pg_ref)
    move(IN_BG, AB_GB, AB_W, pg_ref)


def _pack_w_ab(w_in):
    depth, d, _ = w_in.shape
    tm = min(PACK_TM, d)
    perms = [_permutation(_slab_sources(_DIM_A, N_SLABS)), _permutation(_slab_sources(_DIM_A, 1)),
             _permutation(_slab_sources(_DIM_B, N_SLABS)), _permutation(_slab_sources(_DIM_B, 1)),
             _permutation(_pair_sources())]
    return pl.pallas_call(
        _pack_body,
        grid=(depth, d // tm),
        in_specs=[pl.BlockSpec((None, tm, IN_AB_W), lambda l, r: (l, r, 0))]
                 + [pl.BlockSpec(p.shape, lambda l, r: (0, 0)) for p in perms],
        out_specs=pl.BlockSpec((None, tm, IN_AB_W), lambda l, r: (l, r, 0)),
        out_shape=jax.ShapeDtypeStruct((depth, d, IN_AB_W), BF16),
        compiler_params=pltpu.CompilerParams(
            dimension_semantics=("arbitrary", "arbitrary"), vmem_limit_bytes=VMEM_LIMIT),
        name="pack",
    )(w_in, *perms)


def _proj_body(x_ref, *refs):
    _proj_compute(x_ref[...], *refs)


def _out_proj_body(x_ref, oa_ref, ob_ref, oc_ref, wo_ref, *refs):
    proj_refs, x_out_ref = refs[:-1], refs[-1]
    o = jnp.concatenate([oa_ref[...], ob_ref[...], oc_ref[...]], axis=1)
    x = x_ref[...] + jnp.dot(o, wo_ref[...], preferred_element_type=F32)
    x_out_ref[...] = x
    _proj_compute(x, *proj_refs)


def _proj_compute(x, g_ref, wc_ref, wab_ref, aqn_ref, akn_ref, cqn_ref, ckvn_ref, wuq_ref, wukv_ref, e_ref, tab_ref,
                  qab_ref, ka_ref, va_ref, kb_ref, vb_ref, sg_ref, qc_ref, kc_ref, vc_ref):
    h = (x * lax.rsqrt(jnp.mean(x * x, axis=-1, keepdims=True) + EPS) * g_ref[...]).astype(BF16)
    zc = jnp.dot(h, wc_ref[...], preferred_element_type=F32)
    zab = jnp.dot(h, wab_ref[...], preferred_element_type=F32)

    def rope(xs, mixer):
        return xs * tab_ref[2 * mixer] + pltpu.roll(xs, HALF, 1) * tab_ref[2 * mixer + 1]

    def slab(z, off):
        return z[:, off:off + LANES]

    def head_sumsq(s0, s1):
        t = jnp.concatenate([s0 * s0, s1 * s1], axis=1).astype(BF16)
        ss = jnp.dot(t, e_ref[...], preferred_element_type=F32)
        return ss[:, :LANES], ss[:, LANES:]

    def head_norm(xs, ss, gain):
        return xs * lax.rsqrt(ss * (1.0 / HEAD_DIM) + EPS) * gain

    def silu(gz):
        return (gz / (1.0 + jnp.exp(-gz))).astype(BF16)

    cq = zc[:, C_CQ:C_CQ + C_Q_PAD]
    cq = cq * lax.rsqrt(jnp.sum(cq * cq, axis=-1, keepdims=True) * (1.0 / C_Q_RANK) + EPS) * cqn_ref[...]
    qc = jnp.dot(cq.astype(BF16), wuq_ref[...], preferred_element_type=F32)
    scale_c = (C_NOPE + C_ROPE) ** -0.5 * LOG2E
    for hh in range(C_HEADS):
        q = rope(qc[:, hh * LANES:(hh + 1) * LANES], 2)
        qc_ref[:, hh * LANES:(hh + 1) * LANES] = (q * scale_c).astype(BF16)
    ckv = zc[:, C_CKV:C_CKV + C_KV_RANK]
    ckv = ckv * lax.rsqrt(jnp.mean(ckv * ckv, axis=-1, keepdims=True) + EPS) * ckvn_ref[...]
    kv = jnp.dot(ckv.astype(BF16), wukv_ref[...], preferred_element_type=F32)
    kr = rope(slab(zc, C_CKR), 2)
    for hh in range(C_HEADS):
        kc_ref[hh] = (kv[:, hh * LANES:(hh + 1) * LANES] + kr).astype(BF16)
    vc_ref[...] = kv[:, C_HEADS * LANES:].astype(BF16)
    sg_ref[:, 2 * AB_W:] = silu(zc[:, C_G:])

    scale_ab = HEAD_DIM ** -0.5 * LOG2E
    qa = [slab(zab, AB_QA + j * LANES) for j in range(N_SLABS)]
    ka = slab(zab, AB_KA)
    ss0, ss1 = head_sumsq(qa[0], qa[1])
    ss2, ssk = head_sumsq(qa[2], ka)
    for j, ss in enumerate((ss0, ss1, ss2)):
        q = rope(head_norm(qa[j], ss, aqn_ref[...]), 0)
        qab_ref[:, j * LANES:(j + 1) * LANES] = (q * scale_ab).astype(BF16)
    ka_ref[...] = rope(head_norm(ka, ssk, akn_ref[...]), 0).astype(BF16)
    for j in range(N_SLABS):
        q = rope(slab(zab, AB_QB + j * LANES), 1)
        qab_ref[:, (N_SLABS + j) * LANES:(N_SLABS + j + 1) * LANES] = (q * scale_ab).astype(BF16)
    kb_ref[...] = rope(slab(zab, AB_KB), 1).astype(BF16)
    va_ref[...] = slab(zab, AB_VA).astype(BF16)
    vb_ref[...] = slab(zab, AB_VB).astype(BF16)
    sg_ref[:, :2 * AB_W] = silu(zab[:, AB_GA:AB_GA + 2 * AB_W])


def _layer_spec(a, layer):
    return pl.BlockSpec((None,) + a.shape[1:], lambda *_: (layer,) + (0,) * (a.ndim - 1))


def _proj_call(x, layer, weights, e_mat, tabs, prev_out=None):
    bsz, s_len, d = x.shape
    tm = min(PROJ_TM, s_len)
    grid = (s_len // tm, bsz)
    tok = lambda w: pl.BlockSpec((None, tm, w), lambda s, b: (b, s, 0))
    bf = lambda *shape: jax.ShapeDtypeStruct(shape, BF16)
    widths = [2 * N_SLABS * LANES, LANES, LANES, LANES, LANES, MIX_W, C_HEADS * LANES]
    in_specs = [tok(d)]
    args = [x]
    out_specs = ([tok(w) for w in widths]
                 + [pl.BlockSpec((None, C_HEADS, tm, LANES), lambda s, b: (b, 0, s, 0)), tok(C_HEADS * C_V)])
    out_shape = ([bf(bsz, s_len, w) for w in widths]
                 + [bf(bsz, C_HEADS, s_len, LANES), bf(bsz, s_len, C_HEADS * C_V)])
    body = _proj_body
    if prev_out is not None:
        oa, ob, oc, w_out = prev_out
        in_specs += [tok(oa.shape[-1]), tok(ob.shape[-1]), tok(oc.shape[-1]), _layer_spec(w_out, layer - 1)]
        args += [oa, ob, oc, w_out]
        out_specs.append(tok(d))
        out_shape.append(jax.ShapeDtypeStruct(x.shape, F32))
        body = _out_proj_body
    return pl.pallas_call(
        body,
        grid=grid,
        in_specs=in_specs + [_layer_spec(a, layer) for a in weights]
                 + [pl.BlockSpec(e_mat.shape, lambda s, b: (0, 0)),
                    pl.BlockSpec((6, tm, LANES), lambda s, b: (0, s, 0))],
        out_specs=out_specs,
        out_shape=out_shape,
        compiler_params=pltpu.CompilerParams(
            dimension_semantics=("arbitrary", "arbitrary"), vmem_limit_bytes=VMEM_LIMIT),
        name="proj" if prev_out is None else "out_proj",
    )(*args, *weights, e_mat, tabs)


def _scores(q, k):
    return lax.dot_general(q, k, (((1,), (1,)), ((), ())), preferred_element_type=F32)


def _pipeline_prologue(t, n_q, v_ref, vaug_ref, s_b, m_b):
    @pl.when(lax.rem(jnp.maximum(t - 1, 0), n_q) == 0)
    def _():
        vaug_ref[:, :LANES] = v_ref[...]
        vaug_ref[:, LANES:] = jnp.ones((vaug_ref.shape[0], LANES), BF16)

    @pl.when(t == 0)
    def _():
        s_b[...] = jnp.zeros(s_b.shape, F32)
        m_b[...] = jnp.zeros(m_b.shape, F32)


def _run_parity(t, step, s_a, m_a, s_b, m_b):
    parity = lax.rem(t, 2)

    @pl.when(parity == 0)
    def _():
        step(s_a, m_a, s_b, m_b)

    @pl.when(parity == 1)
    def _():
        step(s_b, m_b, s_a, m_a)


def _softmax_numerators(s_r, m_r, rows, width):
    m = m_r[rows, :]
    return jnp.concatenate(
        [jnp.exp2(s_r[rows, j * LANES:(j + 1) * LANES] - m).astype(BF16) for j in range(width // LANES)], axis=1)


def _stack_queries(q, head_a):
    slabs = [q[:, j * LANES:(j + 1) * LANES] for j in range(N_SLABS)]
    return jnp.concatenate([jnp.where(head_a, s, 0.0) for s in slabs]
                           + [jnp.where(head_a, 0.0, s) for s in slabs], axis=0).astype(BF16)


def _stack_pipe_body(q_ref, k_ref, v_ref, sg_ref, o_ref, s_a, s_b, m_a, m_b, vaug_ref, *, n_q):
    t = pl.program_id(0)
    tq = q_ref.shape[0]
    s_len = k_ref.shape[0]
    _pipeline_prologue(t, n_q, v_ref, vaug_ref, s_b, m_b)

    lane = lax.broadcasted_iota(jnp.int32, (tq, LANES), 1)
    out_low = lane < HALF
    blocks = [pl.ds(i * tq, tq) for i in range(AB_HEADS)]

    def step(s_w, m_w, s_r, m_r):
        q_stack = _stack_queries(q_ref[...].astype(F32), (lane & QUARTER) == 0)
        s = _scores(q_stack, k_ref[...])
        for i in range(AB_HEADS):
            blk = s[i * tq:(i + 1) * tq]
            s_w[blocks[i], :] = blk
            m_w[blocks[i], :] = jnp.broadcast_to(jnp.max(blk, axis=1, keepdims=True), (tq, LANES))
        p = jnp.concatenate([_softmax_numerators(s_r, m_r, blocks[i], s_len) for i in range(AB_HEADS)], axis=0)
        a = jnp.dot(p, vaug_ref[...], preferred_element_type=F32)
        for j in range(N_SLABS):
            lo, hi = a[j * tq:(j + 1) * tq], a[(j + N_SLABS) * tq:(j + N_SLABS + 1) * tq]
            o = jnp.where(out_low, lo[:, :LANES] / lo[:, LANES:], hi[:, :LANES] / hi[:, LANES:])
            cols = slice(j * LANES, (j + 1) * LANES)
            o_ref[:, cols] = (o * sg_ref[:, cols].astype(F32)).astype(BF16)

    _run_parity(t, step, s_a, m_a, s_b, m_b)


def _window_pipe_body(sink_ref, q_ref, k_ref, v_ref, sg_ref, o_ref, s_a, s_b, m_a, m_b, vaug_ref, kpad_ref,
                      *, n_q, n_items, layer):
    t = pl.program_id(0)
    tq = q_ref.shape[0]
    s_len = k_ref.shape[0]
    sub = WINDOW
    kw = 3 * WINDOW
    n_sub = tq // sub
    t_cur = jnp.minimum(t, n_items - 1)
    t_prev = jnp.maximum(t - 1, 0)
    halo = jnp.zeros((WINDOW, 2 * LANES), BF16)

    @pl.when(lax.rem(t_prev, n_q) == 0)
    def _():
        vaug_ref[:WINDOW, :] = halo
        vaug_ref[WINDOW:WINDOW + s_len, :LANES] = v_ref[...]
        vaug_ref[WINDOW:WINDOW + s_len, LANES:] = jnp.ones((s_len, LANES), BF16)
        vaug_ref[WINDOW + s_len:, :] = halo

    @pl.when(lax.rem(t_cur, n_q) == 0)
    def _():
        kpad_ref[:WINDOW, :] = halo[:, :LANES]
        kpad_ref[WINDOW:WINDOW + s_len, :] = k_ref[...]
        kpad_ref[WINDOW + s_len:, :] = halo[:, :LANES]

    @pl.when(t == 0)
    def _():
        s_b[...] = jnp.zeros(s_b.shape, F32)
        m_b[...] = jnp.zeros(m_b.shape, F32)

    lane = lax.broadcasted_iota(jnp.int32, (sub, LANES), 1)
    row = lax.broadcasted_iota(jnp.int32, (sub, LANES), 0)
    head_a = (lane & QUARTER) == 0
    out_low = lane < HALF
    row0_cur = lax.rem(t_cur, n_q) * tq
    row0_prev = lax.rem(t_prev, n_q) * tq
    sinks = [sink_ref[layer, i] * LOG2E for i in range(AB_HEADS)]

    def step(s_w, m_w, s_r, m_r):
        q = q_ref[...].astype(F32)
        for u in range(n_sub):
            r0 = row0_cur + u * sub
            k_win = kpad_ref[pl.ds(pl.multiple_of(r0, sub), kw), :]
            bias_lo = jnp.where(jnp.logical_and(lane >= row, r0 > 0), 0.0, NEG)
            bias_hi = jnp.where(jnp.logical_and(lane <= row, r0 < s_len - sub), 0.0, NEG)
            s = _scores(_stack_queries(q[u * sub:(u + 1) * sub], head_a), k_win)
            for i in range(AB_HEADS):
                blk = s[i * sub:(i + 1) * sub]
                cols = [blk[:, :LANES] + bias_lo, blk[:, LANES:2 * LANES], blk[:, 2 * LANES:] + bias_hi]
                mx = jnp.max(jnp.maximum(jnp.maximum(cols[0], cols[1]), cols[2]), axis=1, keepdims=True)
                rows = pl.ds((u * AB_HEADS + i) * sub, sub)
                for c in range(3):
                    s_w[rows, c * LANES:(c + 1) * LANES] = cols[c]
                m_w[rows, :] = jnp.broadcast_to(jnp.maximum(mx, sinks[i]), (sub, LANES))
        for u in range(n_sub):
            r0 = row0_prev + u * sub
            v_win = vaug_ref[pl.ds(pl.multiple_of(r0, sub), kw), :]
            rows = [pl.ds((u * AB_HEADS + i) * sub, sub) for i in range(AB_HEADS)]
            p = jnp.concatenate([_softmax_numerators(s_r, m_r, rows[i], kw) for i in range(AB_HEADS)], axis=0)
            a = jnp.dot(p, v_win, preferred_element_type=F32)
            for j in range(N_SLABS):
                parts = []
                for i in (j, j + N_SLABS):
                    blk = a[i * sub:(i + 1) * sub]
                    den = blk[:, LANES:] + jnp.exp2(sinks[i] - m_r[rows[i], :])
                    parts.append(blk[:, :LANES] / den)
                cols = slice(j * LANES, (j + 1) * LANES)
                gate = sg_ref[u * sub:(u + 1) * sub, cols].astype(F32)
                o_ref[u * sub:(u + 1) * sub, cols] = (jnp.where(out_low, parts[0], parts[1]) * gate).astype(BF16)

    _run_parity(t, step, s_a, m_a, s_b, m_b)


def _stack_call(name, qab, k, v, sg, blk, tq, sink=None, layer=0):
    bsz, s_len, _ = qab.shape
    tq = min(tq, s_len)
    n_q = s_len // tq
    n_items = bsz * n_q
    windowed = sink is not None
    cur = lambda t: (jnp.minimum(t, n_items - 1) // n_q, jnp.minimum(t, n_items - 1) % n_q)
    prev = lambda t: (jnp.maximum(t - 1, 0) // n_q, jnp.maximum(t - 1, 0) % n_q)
    in_specs = [
        pl.BlockSpec((None, tq, AB_W), lambda t: (*cur(t), blk)),
        pl.BlockSpec((None, s_len, LANES), lambda t: (cur(t)[0], 0, 0)),
        pl.BlockSpec((None, s_len, LANES), lambda t: (prev(t)[0], 0, 0)),
        pl.BlockSpec((None, tq, AB_W), lambda t: (*prev(t), blk)),
    ]
    args = [qab, k, v, sg]
    rows = AB_HEADS * tq
    if windowed:
        in_specs = [pl.BlockSpec(memory_space=pltpu.SMEM)] + in_specs
        args = [sink] + args
        body = functools.partial(_window_pipe_body, n_q=n_q, n_items=n_items, layer=layer)
        padded = s_len + 2 * WINDOW
        scratch = [pltpu.VMEM((rows, 3 * WINDOW), F32), pltpu.VMEM((rows, 3 * WINDOW), F32),
                   pltpu.VMEM((rows, LANES), F32), pltpu.VMEM((rows, LANES), F32),
                   pltpu.VMEM((padded, 2 * LANES), BF16), pltpu.VMEM((padded, LANES), BF16)]
    else:
        body = functools.partial(_stack_pipe_body, n_q=n_q)
        scratch = [pltpu.VMEM((rows, s_len), F32), pltpu.VMEM((rows, s_len), F32),
                   pltpu.VMEM((rows, LANES), F32), pltpu.VMEM((rows, LANES), F32),
                   pltpu.VMEM((s_len, 2 * LANES), BF16)]
    return pl.pallas_call(
        body,
        grid=(n_items + 1,),
        in_specs=in_specs,
        out_specs=pl.BlockSpec((None, tq, AB_W), lambda t: (*prev(t), 0)),
        out_shape=jax.ShapeDtypeStruct((bsz, s_len, AB_W), BF16),
        scratch_shapes=scratch,
        compiler_params=pltpu.CompilerParams(dimension_semantics=("arbitrary",), vmem_limit_bytes=VMEM_LIMIT),
        name=name,
    )(*args)


def _pair_pipe_body(q0_ref, q1_ref, k0_ref, k1_ref, v_ref, sg_ref, o_ref, s_a, s_b, m_a, m_b, vaug_ref, *, n_q):
    t = pl.program_id(0)
    tq, s_len = s_a.shape[1], s_a.shape[2]
    _pipeline_prologue(t, n_q, v_ref, vaug_ref, s_b, m_b)
    out_low = lax.broadcasted_iota(jnp.int32, (tq, LANES), 1) < HALF
    qs, ks = (q0_ref, q1_ref), (k0_ref, k1_ref)

    def step(s_w, m_w, s_r, m_r):
        for hd in range(2):
            s = _scores(qs[hd][...], ks[hd][...])
            s_w[hd] = s
            m_w[hd] = jnp.broadcast_to(jnp.max(s, axis=1, keepdims=True), (tq, LANES))
        outs = []
        for hd in range(2):
            p = _softmax_numerators(s_r.at[hd], m_r.at[hd], slice(None), s_len)
            a = jnp.dot(p, vaug_ref[...], preferred_element_type=F32)
            outs.append(a[:, :LANES] / a[:, LANES:])
        o_ref[...] = (jnp.where(out_low, outs[0], outs[1]) * sg_ref[...].astype(F32)).astype(BF16)

    _run_parity(t, step, s_a, m_a, s_b, m_b)


def _attn_c_call(qc, kc, vc, sg, g_off):
    bsz, s_len, _ = qc.shape
    tq = min(ATTN_C_TQ, s_len)
    n_q = s_len // tq
    n_pairs = C_HEADS // 2
    n_items = bsz * n_pairs * n_q

    def item(t):
        return t // (n_pairs * n_q), (t // n_q) % n_pairs, t % n_q

    cur = lambda t: item(jnp.minimum(t, n_items - 1))
    prev = lambda t: item(jnp.maximum(t - 1, 0))

    def q_spec(o):
        def imap(t):
            b, p, i = cur(t)
            return b, i, 2 * p + o
        return pl.BlockSpec((None, tq, LANES), imap)

    def k_spec(o):
        def imap(t):
            b, p, i = cur(t)
            return b, 2 * p + o, 0, 0
        return pl.BlockSpec((None, None, s_len, LANES), imap)

    def v_map(t):
        b, p, i = prev(t)
        return b, 0, p

    def sg_map(t):
        b, p, i = prev(t)
        return b, i, g_off + p

    def o_map(t):
        b, p, i = prev(t)
        return b, i, p

    return pl.pallas_call(
        functools.partial(_pair_pipe_body, n_q=n_q),
        grid=(n_items + 1,),
        in_specs=[q_spec(0), q_spec(1), k_spec(0), k_spec(1),
                  pl.BlockSpec((None, s_len, LANES), v_map), pl.BlockSpec((None, tq, LANES), sg_map)],
        out_specs=pl.BlockSpec((None, tq, LANES), o_map),
        out_shape=jax.ShapeDtypeStruct((bsz, s_len, n_pairs * LANES), BF16),
        scratch_shapes=[pltpu.VMEM((2, tq, s_len), F32), pltpu.VMEM((2, tq, s_len), F32),
                        pltpu.VMEM((2, tq, LANES), F32), pltpu.VMEM((2, tq, LANES), F32),
                        pltpu.VMEM((s_len, 2 * LANES), BF16)],
        compiler_params=pltpu.CompilerParams(dimension_semantics=("arbitrary",), vmem_limit_bytes=VMEM_LIMIT),
        name="attn_c",
    )(qc, qc, kc, kc, vc, sg)


def _out_body(x_ref, oa_ref, ob_ref, oc_ref, w_ref, fg_ref, y_ref):
    o = jnp.concatenate([oa_ref[...], ob_ref[...], oc_ref[...]], axis=1)
    r = x_ref[...] + jnp.dot(o, w_ref[...], preferred_element_type=F32)
    y_ref[...] = r * lax.rsqrt(jnp.mean(r * r, axis=-1, keepdims=True) + EPS) * fg_ref[...]


def _out_call(x, oa, ob, oc, w_out, layer, fg):
    bsz, s_len, d = x.shape
    tm = min(OUT_TM, s_len)
    tok = lambda a: pl.BlockSpec((None, tm, a.shape[-1]), lambda b, s: (b, s, 0))
    return pl.pallas_call(
        _out_body,
        grid=(bsz, s_len // tm),
        in_specs=[tok(x), tok(oa), tok(ob), tok(oc), _layer_spec(w_out, layer),
                  pl.BlockSpec(fg.shape, lambda b, s: (0, 0))],
        out_specs=tok(x),
        out_shape=jax.ShapeDtypeStruct(x.shape, F32),
        compiler_params=pltpu.CompilerParams(
            dimension_semantics=("arbitrary", "arbitrary"), vmem_limit_bytes=VMEM_LIMIT),
        name="out_final",
    )(x, oa, ob, oc, w_out, fg)


def kernel(x, norm_g, w_in, a_q_norm, a_k_norm, b_sink, c_q_norm, c_kv_norm, c_w_uq, c_w_ukv, w_out, final_g):
    depth = w_in.shape[0]
    rows = lambda v: v.astype(F32)[:, None, :]
    weights = [rows(norm_g), _pack_w_c(w_in), _pack_w_ab(w_in),
               rows(a_q_norm[:, _DIM_A]), rows(a_k_norm[:, _DIM_A]),
               rows(jnp.pad(c_q_norm, ((0, 0), (0, C_Q_PAD - C_Q_RANK)))), rows(c_kv_norm),
               _pack_w_uq(c_w_uq), _pack_w_ukv(c_w_ukv)]
    w_out_packed = _pack_w_out(w_out)
    sink = b_sink.astype(F32)
    tabs = _rope_tables(x.shape[1])
    e_mat = _group_sum_matrix()
    prev_out = None
    for l in range(depth):
        outs = _proj_call(x, l, weights, e_mat, tabs, prev_out=prev_out)
        if prev_out is not None:
            x = outs[-1]
        qab, ka, va, kb, vb, sg, qc, kc, vc = outs[:9]
        oa = _stack_call("attn_a", qab, ka, va, sg, 0, ATTN_A_TQ)
        ob = _stack_call("attn_b", qab, kb, vb, sg, 1, ATTN_B_TQ, sink=sink, layer=l)
        oc = _attn_c_call(qc, kc, vc, sg, 2 * N_SLABS)
        prev_out = (oa, ob, oc, w_out_packed)
    return _out_call(x, *prev_out, depth - 1, final_g.astype(F32)[None, :])
```

```python
import functools

import numpy as np
import jax
import jax.numpy as jnp
from jax import lax
from jax.experimental import pallas as pl
from jax.experimental.pallas import tpu as pltpu

F32 = jnp.float32
BF16 = jnp.bfloat16

GRID_W = 64
ROPE_THETA = 10000.0
EPS = 1e-6
HEAD_DIM = 64
WINDOW = 128
AB_HEADS = 6
C_HEADS = 4
C_NOPE = 64
C_ROPE = 32
C_V = 64
C_Q_RANK = 192
C_KV_RANK = 128

LANES = 128
HALF = LANES // 2
QUARTER = LANES // 4
N_SLABS = AB_HEADS // 2
AB_W = AB_HEADS * HEAD_DIM
C_Q_PAD = 256
NEG = -1e30
LOG2E = 1.4426950408889634

IN_AQ, IN_AK, IN_AV, IN_AG = 0, 384, 512, 640
IN_BQ, IN_BK, IN_BV, IN_BG = 1024, 1408, 1536, 1664
IN_CQ, IN_CKV, IN_CKR, IN_CG = 2048, 2240, 2368, 2400
IN_AB_W = 2048

AB_QA, AB_KA, AB_QB, AB_KB, AB_VA, AB_VB, AB_GA, AB_GB = 0, 384, 512, 896, 1024, 1152, 1280, 1664
C_CQ, C_CKV, C_CKR, C_G, PACKED_C_W = 0, 256, 384, 512, 768
MIX_W = 1024

PACK_TM = 512
PROJ_TM = 512
ATTN_A_TQ = 256
ATTN_A_SUB = 128
ATTN_B_TQ = 512
ATTN_C_TQ = 1024
OUT_TM = 1024
VMEM_LIMIT = 56 * 1024 * 1024


_LANE = np.arange(LANES)
_HEAD_OF_LANE = (_LANE // QUARTER) % 2


def _dim_of_lane(first_half_dims, partner):
    f = np.asarray(first_half_dims)
    return np.where(_LANE < HALF, f[_LANE % QUARTER], f[_LANE % QUARTER] + partner)


_DIM_A = _dim_of_lane(list(range(16)) + list(range(32, 48)), 16)
_DIM_B = _dim_of_lane(list(range(32)), 32)

_C_ROPE_OF_LANE = np.full(LANES, -1)
_C_ROPE_OF_LANE[0:16] = np.arange(16)
_C_ROPE_OF_LANE[64:80] = 16 + np.arange(16)


def _permutation(src_of_dst):
    n = len(src_of_dst)
    p = np.zeros((n, n), np.float32)
    p[src_of_dst, np.arange(n)] = 1.0
    return jnp.asarray(p, BF16)


def _slab_sources(dim_of_lane, n_slabs):
    return np.concatenate([np.where(_HEAD_OF_LANE == 0, p, p + n_slabs) * HEAD_DIM + dim_of_lane
                           for p in range(n_slabs)])


def _pair_sources():
    d = np.arange(HEAD_DIM)
    return np.concatenate([np.concatenate([p * HEAD_DIM + d, (p + N_SLABS) * HEAD_DIM + d]) for p in range(N_SLABS)])


def _c_slab(x, rope):
    lead = x.shape[:-1]
    zeros = lambda n: jnp.zeros(lead + (n,), x.dtype)
    r0, r1 = (rope[..., :16], rope[..., 16:]) if rope is not None else (zeros(16), zeros(16))
    return jnp.concatenate([r0, x[..., :48], r1, x[..., 48:C_NOPE], zeros(LANES - C_NOPE - C_ROPE)], axis=-1)


def _pack_w_c(w):
    zeros = lambda n: jnp.zeros(w.shape[:-1] + (n,), w.dtype)
    ckr = w[..., IN_CKR:IN_CKR + C_ROPE]
    return jnp.concatenate([
        w[..., IN_CQ:IN_CQ + C_Q_RANK], zeros(C_Q_PAD - C_Q_RANK),
        w[..., IN_CKV:IN_CKV + C_KV_RANK],
        ckr[..., :16], zeros(48), ckr[..., 16:], zeros(48),
        w[..., IN_CG:IN_CG + C_HEADS * C_V]], axis=-1)


def _pack_w_uq(w):
    n = w.shape[0]
    w = w.astype(BF16).reshape(n, C_Q_RANK, C_HEADS, C_NOPE + C_ROPE)
    slab = _c_slab(w[..., :C_NOPE], w[..., C_NOPE:]).reshape(n, C_Q_RANK, C_HEADS * LANES)
    return jnp.pad(slab, ((0, 0), (0, C_Q_PAD - C_Q_RANK), (0, 0)))


def _pack_w_ukv(w):
    n = w.shape[0]
    w = w.astype(BF16).reshape(n, C_KV_RANK, C_HEADS, C_NOPE + C_V)
    k = _c_slab(w[..., :C_NOPE], None).reshape(n, C_KV_RANK, C_HEADS * LANES)
    v = w[..., C_NOPE:].reshape(n, C_KV_RANK, C_HEADS * C_V)
    return jnp.concatenate([k, v], axis=-1)


def _pack_w_out(w):
    n, _, dd = w.shape
    pair = lambda r: r.reshape(n, 2, N_SLABS, HEAD_DIM, dd).transpose(0, 2, 1, 3, 4).reshape(n, AB_W, dd)
    return jnp.concatenate([pair(w[:, :AB_W]), pair(w[:, AB_W:2 * AB_W]), w[:, 2 * AB_W:]], axis=1).astype(BF16)


def _group_sum_matrix():
    head = np.concatenate([_HEAD_OF_LANE, 2 + _HEAD_OF_LANE])
    return jnp.asarray(head[:, None] == head[None, :], BF16)


def _rope_tables(s_len):
    t = np.arange(s_len, dtype=np.float64)[:, None]
    sign = np.where(_LANE < HALF, -1.0, 1.0)[None, :]

    def tables(pos, freq, dim, active=None):
        ang = pos * (ROPE_THETA ** (-2.0 * freq / dim))[None, :]
        cos, sin = np.cos(ang), np.sin(ang) * sign
        if active is not None:
            cos, sin = np.where(active[None, :], cos, 1.0), np.where(active[None, :], sin, 0.0)
        return [cos, sin]

    pos_a = np.where((_DIM_A < HEAD_DIM // 2)[None, :], t // GRID_W, t % GRID_W)
    rope_c = _C_ROPE_OF_LANE >= 0
    tabs = (tables(pos_a, _DIM_A % 16, HEAD_DIM // 2) + tables(t, _DIM_B % 32, HEAD_DIM)
            + tables(t, np.maximum(_C_ROPE_OF_LANE, 0) % 16, C_ROPE, rope_c))
    return jnp.asarray(np.stack(tabs).astype(np.float32))


def _pack_body(w_ref, pqa_ref, pka_ref, pqb_ref, pkb_ref, pg_ref, o_ref):
    def move(src, dst, width, perm_ref):
        x = w_ref[:, src:src + width]
        if perm_ref is not None:
            x = jnp.dot(x, perm_ref[...], preferred_element_type=F32).astype(BF16)
        o_ref[:, dst:dst + width] = x

    move(IN_AQ, AB_QA, AB_W, pqa_ref)
    move(IN_AK, AB_KA, LANES, pka_ref)
    move(IN_BQ, AB_QB, AB_W, pqb_ref)
    move(IN_BK, AB_KB, LANES, pkb_ref)
    move(IN_AV, AB_VA, LANES, None)
    move(IN_BV, AB_VB, LANES, None)
    move(IN_AG, AB_GA, AB_W, pg_ref)
    move(IN_BG, AB_GB, AB_W, pg_ref)


def _pack_w_ab(w_in):
    depth, d, _ = w_in.shape
    tm = min(PACK_TM, d)
    perms = [_permutation(_slab_sources(_DIM_A, N_SLABS)), _permutation(_slab_sources(_DIM_A, 1)),
             _permutation(_slab_sources(_DIM_B, N_SLABS)), _permutation(_slab_sources(_DIM_B, 1)),
             _permutation(_pair_sources())]
    return pl.pallas_call(
        _pack_body,
        grid=(depth, d // tm),
        in_specs=[pl.BlockSpec((None, tm, IN_AB_W), lambda l, r: (l, r, 0))]
                 + [pl.BlockSpec(p.shape, lambda l, r: (0, 0)) for p in perms],
        out_specs=pl.BlockSpec((None, tm, IN_AB_W), lambda l, r: (l, r, 0)),
        out_shape=jax.ShapeDtypeStruct((depth, d, IN_AB_W), BF16),
        compiler_params=pltpu.CompilerParams(
            dimension_semantics=("arbitrary", "arbitrary"), vmem_limit_bytes=VMEM_LIMIT),
        name="pack",
    )(w_in, *perms)


def _proj_body(x_ref, *refs):
    _proj_compute(x_ref[...], *refs)


def _out_proj_body(x_ref, oa_ref, ob_ref, oc_ref, wo_ref, *refs):
    proj_refs, x_out_ref = refs[:-1], refs[-1]
    o = jnp.concatenate([oa_ref[...], ob_ref[...], oc_ref[...]], axis=1)
    x = x_ref[...] + jnp.dot(o, wo_ref[...], preferred_element_type=F32)
    x_out_ref[...] = x
    _proj_compute(x, *proj_refs)


def _proj_compute(x, g_ref, wc_ref, wab_ref, aqn_ref, akn_ref, cqn_ref, ckvn_ref, wuq_ref, wukv_ref, e_ref, tab_ref,
                  qab_ref, ka_ref, va_ref, kb_ref, vb_ref, sg_ref, qc_ref, kc_ref, vc_ref):
    h = (x * lax.rsqrt(jnp.mean(x * x, axis=-1, keepdims=True) + EPS) * g_ref[...]).astype(BF16)
    zc = jnp.dot(h, wc_ref[...], preferred_element_type=F32)
    zab = jnp.dot(h, wab_ref[...], preferred_element_type=F32)

    def rope(xs, mixer):
        return xs * tab_ref[2 * mixer] + pltpu.roll(xs, HALF, 1) * tab_ref[2 * mixer + 1]

    def slab(z, off):
        return z[:, off:off + LANES]

    def head_sumsq(s0, s1):
        t = jnp.concatenate([s0 * s0, s1 * s1], axis=1).astype(BF16)
        ss = jnp.dot(t, e_ref[...], preferred_element_type=F32)
        return ss[:, :LANES], ss[:, LANES:]

    def head_norm(xs, ss, gain):
        return xs * lax.rsqrt(ss * (1.0 / HEAD_DIM) + EPS) * gain

    def silu(gz):
        return (gz / (1.0 + jnp.exp(-gz))).astype(BF16)

    cq = zc[:, C_CQ:C_CQ + C_Q_PAD]
    cq = cq * lax.rsqrt(jnp.sum(cq * cq, axis=-1, keepdims=True) * (1.0 / C_Q_RANK) + EPS) * cqn_ref[...]
    qc = jnp.dot(cq.astype(BF16), wuq_ref[...], preferred_element_type=F32)
    scale_c = (C_NOPE + C_ROPE) ** -0.5 * LOG2E
    for hh in range(C_HEADS):
        q = rope(qc[:, hh * LANES:(hh + 1) * LANES], 2)
        qc_ref[:, hh * LANES:(hh + 1) * LANES] = (q * scale_c).astype(BF16)
    ckv = zc[:, C_CKV:C_CKV + C_KV_RANK]
    ckv = ckv * lax.rsqrt(jnp.mean(ckv * ckv, axis=-1, keepdims=True) + EPS) * ckvn_ref[...]
    kv = jnp.dot(ckv.astype(BF16), wukv_ref[...], preferred_element_type=F32)
    kr = rope(slab(zc, C_CKR), 2)
    for hh in range(C_HEADS):
        kc_ref[hh] = (kv[:, hh * LANES:(hh + 1) * LANES] + kr).astype(BF16)
    ones = jnp.ones((x.shape[0], LANES), BF16)
    for p in range(C_HEADS // 2):
        v_pair = kv[:, (C_HEADS + p) * LANES:(C_HEADS + p + 1) * LANES].astype(BF16)
        vc_ref[:, 2 * p * LANES:(2 * p + 2) * LANES] = jnp.concatenate([v_pair, ones], axis=1)
    sg_ref[:, 2 * AB_W:] = silu(zc[:, C_G:])

    scale_ab = HEAD_DIM ** -0.5 * LOG2E
    qa = [slab(zab, AB_QA + j * LANES) for j in range(N_SLABS)]
    ka = slab(zab, AB_KA)
    ss0, ss1 = head_sumsq(qa[0], qa[1])
    ss2, ssk = head_sumsq(qa[2], ka)
    for j, ss in enumerate((ss0, ss1, ss2)):
        q = rope(head_norm(qa[j], ss, aqn_ref[...]), 0)
        qab_ref[:, j * LANES:(j + 1) * LANES] = (q * scale_ab).astype(BF16)
    ka_ref[...] = rope(head_norm(ka, ssk, akn_ref[...]), 0).astype(BF16)
    for j in range(N_SLABS):
        q = rope(slab(zab, AB_QB + j * LANES), 1)
        qab_ref[:, (N_SLABS + j) * LANES:(N_SLABS + j + 1) * LANES] = (q * scale_ab).astype(BF16)
    kb_ref[...] = rope(slab(zab, AB_KB), 1).astype(BF16)
    va_ref[...] = jnp.concatenate([slab(zab, AB_VA).astype(BF16), ones], axis=1)
    vb_ref[...] = jnp.concatenate([slab(zab, AB_VB).astype(BF16), ones], axis=1)
    sg_ref[:, :2 * AB_W] = silu(zab[:, AB_GA:AB_GA + 2 * AB_W])


def _layer_spec(a, layer):
    return pl.BlockSpec((None,) + a.shape[1:], lambda *_: (layer,) + (0,) * (a.ndim - 1))


def _proj_call(x, layer, weights, e_mat, tabs, prev_out=None):
    bsz, s_len, d = x.shape
    tm = min(PROJ_TM, s_len)
    grid = (s_len // tm, bsz)
    tok = lambda w: pl.BlockSpec((None, tm, w), lambda s, b: (b, s, 0))
    bf = lambda *shape: jax.ShapeDtypeStruct(shape, BF16)
    widths = [2 * N_SLABS * LANES, LANES, 2 * LANES, LANES, 2 * LANES, MIX_W, C_HEADS * LANES]
    in_specs = [tok(d)]
    args = [x]
    out_specs = ([tok(w) for w in widths]
                 + [pl.BlockSpec((None, C_HEADS, tm, LANES), lambda s, b: (b, 0, s, 0)), tok(C_HEADS * LANES)])
    out_shape = ([bf(bsz, s_len, w) for w in widths]
                 + [bf(bsz, C_HEADS, s_len, LANES), bf(bsz, s_len, C_HEADS * LANES)])
    body = _proj_body
    if prev_out is not None:
        oa, ob, oc, w_out = prev_out
        in_specs += [tok(oa.shape[-1]), tok(ob.shape[-1]), tok(oc.shape[-1]), _layer_spec(w_out, layer - 1)]
        args += [oa, ob, oc, w_out]
        out_specs.append(tok(d))
        out_shape.append(jax.ShapeDtypeStruct(x.shape, F32))
        body = _out_proj_body
    return pl.pallas_call(
        body,
        grid=grid,
        in_specs=in_specs + [_layer_spec(a, layer) for a in weights]
                 + [pl.BlockSpec(e_mat.shape, lambda s, b: (0, 0)),
                    pl.BlockSpec((6, tm, LANES), lambda s, b: (0, s, 0))],
        out_specs=out_specs,
        out_shape=out_shape,
        compiler_params=pltpu.CompilerParams(
            dimension_semantics=("arbitrary", "arbitrary"), vmem_limit_bytes=VMEM_LIMIT),
        name="proj" if prev_out is None else "out_proj",
    )(*args, *weights, e_mat, tabs)


def _scores(q, k):
    return lax.dot_general(q, k, (((1,), (1,)), ((), ())), preferred_element_type=F32)


def _define_first_slot(t, s_b, m_b):
    @pl.when(t == 0)
    def _():
        s_b[...] = jnp.zeros(s_b.shape, F32)
        m_b[...] = jnp.zeros(m_b.shape, F32)


def _run_parity(t, step, s_a, m_a, s_b, m_b):
    parity = lax.rem(t, 2)

    @pl.when(parity == 0)
    def _():
        step(s_a, m_a, s_b, m_b)

    @pl.when(parity == 1)
    def _():
        step(s_b, m_b, s_a, m_a)


def _softmax_numerators(s_r, m_r, rows, width):
    m = m_r[rows, :]
    return jnp.concatenate(
        [jnp.exp2(s_r[rows, j * LANES:(j + 1) * LANES] - m).astype(BF16) for j in range(width // LANES)], axis=1)


def _stack_queries(q, head_a):
    slabs = [q[:, j * LANES:(j + 1) * LANES] for j in range(N_SLABS)]
    return jnp.concatenate([jnp.where(head_a, s, 0.0) for s in slabs]
                           + [jnp.where(head_a, 0.0, s) for s in slabs], axis=0).astype(BF16)


def _stack_pipe_body(q_ref, k_ref, v_ref, sg_ref, o_ref, s_a, s_b, m_a, m_b):
    t = pl.program_id(0)
    tq = q_ref.shape[0]
    s_len = k_ref.shape[0]
    _define_first_slot(t, s_b, m_b)

    sub = min(ATTN_A_SUB, tq)
    lane = lax.broadcasted_iota(jnp.int32, (sub, LANES), 1)
    out_low = lane < HALF

    def step(s_w, m_w, s_r, m_r):
        q = q_ref[...].astype(F32)
        for u in range(tq // sub):
            s = _scores(_stack_queries(q[u * sub:(u + 1) * sub], (lane & QUARTER) == 0), k_ref[...])
            for i in range(AB_HEADS):
                blk = s[i * sub:(i + 1) * sub]
                rows = pl.ds((u * AB_HEADS + i) * sub, sub)
                s_w[rows, :] = blk
                m_w[rows, :] = jnp.broadcast_to(jnp.max(blk, axis=1, keepdims=True), (sub, LANES))
        for u in range(tq // sub):
            rows = [pl.ds((u * AB_HEADS + i) * sub, sub) for i in range(AB_HEADS)]
            p = jnp.concatenate([_softmax_numerators(s_r, m_r, rows[i], s_len) for i in range(AB_HEADS)], axis=0)
            a = jnp.dot(p, v_ref[...], preferred_element_type=F32)
            for j in range(N_SLABS):
                lo, hi = a[j * sub:(j + 1) * sub], a[(j + N_SLABS) * sub:(j + N_SLABS + 1) * sub]
                o = jnp.where(out_low, lo[:, :LANES] / lo[:, LANES:], hi[:, :LANES] / hi[:, LANES:])
                cols = slice(j * LANES, (j + 1) * LANES)
                gate = sg_ref[u * sub:(u + 1) * sub, cols].astype(F32)
                o_ref[u * sub:(u + 1) * sub, cols] = (o * gate).astype(BF16)

    _run_parity(t, step, s_a, m_a, s_b, m_b)


def _window_pipe_body(sink_ref, q_ref, k_ref, v_ref, sg_ref, o_ref, s_a, s_b, m_a, m_b, *, n_q, n_items, layer):
    t = pl.program_id(0)
    tq = q_ref.shape[0]
    s_len = k_ref.shape[0]
    sub = WINDOW
    kw = 3 * WINDOW
    n_sub = tq // sub
    t_cur = jnp.minimum(t, n_items - 1)
    t_prev = jnp.maximum(t - 1, 0)
    _define_first_slot(t, s_b, m_b)

    lane = lax.broadcasted_iota(jnp.int32, (sub, LANES), 1)
    row = lax.broadcasted_iota(jnp.int32, (sub, LANES), 0)
    head_a = (lane & QUARTER) == 0
    out_low = lane < HALF
    row0_cur = lax.rem(t_cur, n_q) * tq
    row0_prev = lax.rem(t_prev, n_q) * tq
    sinks = [sink_ref[layer, i] * LOG2E for i in range(AB_HEADS)]

    def win_start(r0):
        return pl.multiple_of(jnp.clip(r0 - WINDOW, 0, s_len - kw), WINDOW)

    def step(s_w, m_w, s_r, m_r):
        q = q_ref[...].astype(F32)
        for u in range(n_sub):
            r0 = row0_cur + u * sub
            ws = win_start(r0)
            dist = (r0 - ws) + row - lane
            bias = [jnp.where(jnp.abs(dist - c * LANES) <= WINDOW, 0.0, NEG) for c in range(3)]
            s = _scores(_stack_queries(q[u * sub:(u + 1) * sub], head_a), k_ref[pl.ds(ws, kw), :])
            for i in range(AB_HEADS):
                blk = s[i * sub:(i + 1) * sub]
                cols = [blk[:, c * LANES:(c + 1) * LANES] + bias[c] for c in range(3)]
                mx = jnp.max(jnp.maximum(jnp.maximum(cols[0], cols[1]), cols[2]), axis=1, keepdims=True)
                rows = pl.ds((u * AB_HEADS + i) * sub, sub)
                for c in range(3):
                    s_w[rows, c * LANES:(c + 1) * LANES] = cols[c]
                m_w[rows, :] = jnp.broadcast_to(jnp.maximum(mx, sinks[i]), (sub, LANES))
        for u in range(n_sub):
            v_win = v_ref[pl.ds(win_start(row0_prev + u * sub), kw), :]
            rows = [pl.ds((u * AB_HEADS + i) * sub, sub) for i in range(AB_HEADS)]
            p = jnp.concatenate([_softmax_numerators(s_r, m_r, rows[i], kw) for i in range(AB_HEADS)], axis=0)
            a = jnp.dot(p, v_win, preferred_element_type=F32)
            for j in range(N_SLABS):
                parts = []
                for i in (j, j + N_SLABS):
                    blk = a[i * sub:(i + 1) * sub]
                    den = blk[:, LANES:] + jnp.exp2(sinks[i] - m_r[rows[i], :])
                    parts.append(blk[:, :LANES] / den)
                cols = slice(j * LANES, (j + 1) * LANES)
                gate = sg_ref[u * sub:(u + 1) * sub, cols].astype(F32)
                o_ref[u * sub:(u + 1) * sub, cols] = (jnp.where(out_low, parts[0], parts[1]) * gate).astype(BF16)

    _run_parity(t, step, s_a, m_a, s_b, m_b)


def _stack_call(name, qab, k, v, sg, blk, tq, sink=None, layer=0):
    bsz, s_len, _ = qab.shape
    tq = min(tq, s_len)
    n_q = s_len // tq
    n_items = bsz * n_q
    windowed = sink is not None
    cur = lambda t: (jnp.minimum(t, n_items - 1) // n_q, jnp.minimum(t, n_items - 1) % n_q)
    prev = lambda t: (jnp.maximum(t - 1, 0) // n_q, jnp.maximum(t - 1, 0) % n_q)
    in_specs = [
        pl.BlockSpec((None, tq, AB_W), lambda t: (*cur(t), blk)),
        pl.BlockSpec((None, s_len, LANES), lambda t: (cur(t)[0], 0, 0)),
        pl.BlockSpec((None, s_len, 2 * LANES), lambda t: (prev(t)[0], 0, 0)),
        pl.BlockSpec((None, tq, AB_W), lambda t: (*prev(t), blk)),
    ]
    args = [qab, k, v, sg]
    rows = AB_HEADS * tq
    width = s_len
    body = _stack_pipe_body
    if windowed:
        in_specs = [pl.BlockSpec(memory_space=pltpu.SMEM)] + in_specs
        args = [sink] + args
        width = 3 * WINDOW
        body = functools.partial(_window_pipe_body, n_q=n_q, n_items=n_items, layer=layer)
    scratch = [pltpu.VMEM((rows, width), F32), pltpu.VMEM((rows, width), F32),
               pltpu.VMEM((rows, LANES), F32), pltpu.VMEM((rows, LANES), F32)]
    return pl.pallas_call(
        body,
        grid=(n_items + 1,),
        in_specs=in_specs,
        out_specs=pl.BlockSpec((None, tq, AB_W), lambda t: (*prev(t), 0)),
        out_shape=jax.ShapeDtypeStruct((bsz, s_len, AB_W), BF16),
        scratch_shapes=scratch,
        compiler_params=pltpu.CompilerParams(dimension_semantics=("arbitrary",), vmem_limit_bytes=VMEM_LIMIT),
        name=name,
    )(*args)


def _pair_pipe_body(q0_ref, q1_ref, k0_ref, k1_ref, v_ref, sg_ref, o_ref, s_a, s_b, m_a, m_b):
    t = pl.program_id(0)
    tq, s_len = s_a.shape[1], s_a.shape[2]
    _define_first_slot(t, s_b, m_b)
    out_low = lax.broadcasted_iota(jnp.int32, (tq, LANES), 1) < HALF
    qs, ks = (q0_ref, q1_ref), (k0_ref, k1_ref)

    def step(s_w, m_w, s_r, m_r):
        for hd in range(2):
            s = _scores(qs[hd][...], ks[hd][...])
            s_w[hd] = s
            m_w[hd] = jnp.broadcast_to(jnp.max(s, axis=1, keepdims=True), (tq, LANES))
        outs = []
        for hd in range(2):
            p = _softmax_numerators(s_r.at[hd], m_r.at[hd], slice(None), s_len)
            a = jnp.dot(p, v_ref[...], preferred_element_type=F32)
            outs.append(a[:, :LANES] / a[:, LANES:])
        o_ref[...] = (jnp.where(out_low, outs[0], outs[1]) * sg_ref[...].astype(F32)).astype(BF16)

    _run_parity(t, step, s_a, m_a, s_b, m_b)


def _attn_c_call(qc, kc, vc, sg, g_off):
    bsz, s_len, _ = qc.shape
    tq = min(ATTN_C_TQ, s_len)
    n_q = s_len // tq
    n_pairs = C_HEADS // 2
    n_items = bsz * n_pairs * n_q

    def item(t):
        return t // (n_pairs * n_q), (t // n_q) % n_pairs, t % n_q

    cur = lambda t: item(jnp.minimum(t, n_items - 1))
    prev = lambda t: item(jnp.maximum(t - 1, 0))

    def q_spec(o):
        def imap(t):
            b, p, i = cur(t)
            return b, i, 2 * p + o
        return pl.BlockSpec((None, tq, LANES), imap)

    def k_spec(o):
        def imap(t):
            b, p, i = cur(t)
            return b, 2 * p + o, 0, 0
        return pl.BlockSpec((None, None, s_len, LANES), imap)

    def v_map(t):
        b, p, i = prev(t)
        return b, 0, p

    def sg_map(t):
        b, p, i = prev(t)
        return b, i, g_off + p

    def o_map(t):
        b, p, i = prev(t)
        return b, i, p

    return pl.pallas_call(
        _pair_pipe_body,
        grid=(n_items + 1,),
        in_specs=[q_spec(0), q_spec(1), k_spec(0), k_spec(1),
                  pl.BlockSpec((None, s_len, 2 * LANES), v_map), pl.BlockSpec((None, tq, LANES), sg_map)],
        out_specs=pl.BlockSpec((None, tq, LANES), o_map),
        out_shape=jax.ShapeDtypeStruct((bsz, s_len, n_pairs * LANES), BF16),
        scratch_shapes=[pltpu.VMEM((2, tq, s_len), F32), pltpu.VMEM((2, tq, s_len), F32),
                        pltpu.VMEM((2, tq, LANES), F32), pltpu.VMEM((2, tq, LANES), F32)],
        compiler_params=pltpu.CompilerParams(dimension_semantics=("arbitrary",), vmem_limit_bytes=VMEM_LIMIT),
        name="attn_c",
    )(qc, qc, kc, kc, vc, sg)


def _out_body(x_ref, oa_ref, ob_ref, oc_ref, w_ref, fg_ref, y_ref):
    o = jnp.concatenate([oa_ref[...], ob_ref[...], oc_ref[...]], axis=1)
    r = x_ref[...] + jnp.dot(o, w_ref[...], preferred_element_type=F32)
    y_ref[...] = r * lax.rsqrt(jnp.mean(r * r, axis=-1, keepdims=True) + EPS) * fg_ref[...]


def _out_call(x, oa, ob, oc, w_out, layer, fg):
    bsz, s_len, d = x.shape
    tm = min(OUT_TM, s_len)
    tok = lambda a: pl.BlockSpec((None, tm, a.shape[-1]), lambda b, s: (b, s, 0))
    return pl.pallas_call(
        _out_body,
        grid=(bsz, s_len // tm),
        in_specs=[tok(x), tok(oa), tok(ob), tok(oc), _layer_spec(w_out, layer),
                  pl.BlockSpec(fg.shape, lambda b, s: (0, 0))],
        out_specs=tok(x),
        out_shape=jax.ShapeDtypeStruct(x.shape, F32),
        compiler_params=pltpu.CompilerParams(
            dimension_semantics=("arbitrary", "arbitrary"), vmem_limit_bytes=VMEM_LIMIT),
        name="out_final",
    )(x, oa, ob, oc, w_out, fg)


def kernel(x, norm_g, w_in, a_q_norm, a_k_norm, b_sink, c_q_norm, c_kv_norm, c_w_uq, c_w_ukv, w_out, final_g):
    depth = w_in.shape[0]
    rows = lambda v: v.astype(F32)[:, None, :]
    w_in = w_in.astype(BF16)
    weights = [rows(norm_g), _pack_w_c(w_in), _pack_w_ab(w_in),
               rows(a_q_norm[:, _DIM_A]), rows(a_k_norm[:, _DIM_A]),
               rows(jnp.pad(c_q_norm, ((0, 0), (0, C_Q_PAD - C_Q_RANK)))), rows(c_kv_norm),
               _pack_w_uq(c_w_uq), _pack_w_ukv(c_w_ukv)]
    w_out_packed = _pack_w_out(w_out)
    sink = b_sink.astype(F32)
    tabs = _rope_tables(x.shape[1])
    e_mat = _group_sum_matrix()
    prev_out = None
    for l in range(depth):
        outs = _proj_call(x, l, weights, e_mat, tabs, prev_out=prev_out)
        if prev_out is not None:
            x = outs[-1]
        qab, ka, va, kb, vb, sg, qc, kc, vc = outs[:9]
        oa = _stack_call("attn_a", qab, ka, va, sg, 0, ATTN_A_TQ)
        ob = _stack_call("attn_b", qab, kb, vb, sg, 1, ATTN_B_TQ, sink=sink, layer=l)
        oc = _attn_c_call(qc, kc, vc, sg, 2 * N_SLABS)
        prev_out = (oa, ob, oc, w_out_packed)
    return _out_call(x, *prev_out, depth - 1, final_g.astype(F32)[None, :])
```

```python
import functools

import numpy as np
import jax
import jax.numpy as jnp
from jax import lax
from jax.experimental import pallas as pl
from jax.experimental.pallas import tpu as pltpu

F32 = jnp.float32
BF16 = jnp.bfloat16

GRID_W = 64
ROPE_THETA = 10000.0
EPS = 1e-6
HEAD_DIM = 64
WINDOW = 128
AB_HEADS = 6
C_HEADS = 4
C_NOPE = 64
C_ROPE = 32
C_V = 64
C_Q_RANK = 192
C_KV_RANK = 128

LANES = 128
HALF = LANES // 2
QUARTER = LANES // 4
N_SLABS = AB_HEADS // 2
AB_W = AB_HEADS * HEAD_DIM
C_Q_PAD = 256
NEG = -1e30
LOG2E = 1.4426950408889634

IN_AQ, IN_AK, IN_AV, IN_AG = 0, 384, 512, 640
IN_BQ, IN_BK, IN_BV, IN_BG = 1024, 1408, 1536, 1664
IN_CQ, IN_CKV, IN_CKR, IN_CG = 2048, 2240, 2368, 2400
IN_AB_W = 2048

AB_QA, AB_KA, AB_QB, AB_KB, AB_VA, AB_VB, AB_GA, AB_GB = 0, 384, 512, 896, 1024, 1152, 1280, 1664
C_CQ, C_CKV, C_CKR, C_G, PACKED_C_W = 0, 256, 384, 512, 768
MIX_W = 1024

PACK_TM = 512
PROJ_TM = 512
ATTN_A_TQ = 256
ATTN_A_SUB = 128
ATTN_B_TQ = 1024
ATTN_C_TQ = 1024
OUT_TM = 2048
VMEM_LIMIT = 56 * 1024 * 1024


_LANE = np.arange(LANES)
_HEAD_OF_LANE = (_LANE // QUARTER) % 2


def _dim_of_lane(first_half_dims, partner):
    f = np.asarray(first_half_dims)
    return np.where(_LANE < HALF, f[_LANE % QUARTER], f[_LANE % QUARTER] + partner)


_DIM_A = _dim_of_lane(list(range(16)) + list(range(32, 48)), 16)
_DIM_B = _dim_of_lane(list(range(32)), 32)

_C_ROPE_OF_LANE = np.full(LANES, -1)
_C_ROPE_OF_LANE[0:16] = np.arange(16)
_C_ROPE_OF_LANE[64:80] = 16 + np.arange(16)


def _permutation(src_of_dst):
    n = len(src_of_dst)
    p = np.zeros((n, n), np.float32)
    p[src_of_dst, np.arange(n)] = 1.0
    return jnp.asarray(p, BF16)


def _slab_sources(dim_of_lane, n_slabs):
    return np.concatenate([np.where(_HEAD_OF_LANE == 0, p, p + n_slabs) * HEAD_DIM + dim_of_lane
                           for p in range(n_slabs)])


def _pair_sources():
    d = np.arange(HEAD_DIM)
    return np.concatenate([np.concatenate([p * HEAD_DIM + d, (p + N_SLABS) * HEAD_DIM + d]) for p in range(N_SLABS)])


def _c_slab(x, rope):
    lead = x.shape[:-1]
    zeros = lambda n: jnp.zeros(lead + (n,), x.dtype)
    r0, r1 = (rope[..., :16], rope[..., 16:]) if rope is not None else (zeros(16), zeros(16))
    return jnp.concatenate([r0, x[..., :48], r1, x[..., 48:C_NOPE], zeros(LANES - C_NOPE - C_ROPE)], axis=-1)


def _pack_w_c(w):
    zeros = lambda n: jnp.zeros(w.shape[:-1] + (n,), w.dtype)
    ckr = w[..., IN_CKR:IN_CKR + C_ROPE]
    return jnp.concatenate([
        w[..., IN_CQ:IN_CQ + C_Q_RANK], zeros(C_Q_PAD - C_Q_RANK),
        w[..., IN_CKV:IN_CKV + C_KV_RANK],
        ckr[..., :16], zeros(48), ckr[..., 16:], zeros(48),
        w[..., IN_CG:IN_CG + C_HEADS * C_V]], axis=-1)


def _pack_w_uq(w):
    n = w.shape[0]
    w = w.astype(BF16).reshape(n, C_Q_RANK, C_HEADS, C_NOPE + C_ROPE)
    slab = _c_slab(w[..., :C_NOPE], w[..., C_NOPE:]).reshape(n, C_Q_RANK, C_HEADS * LANES)
    return jnp.pad(slab, ((0, 0), (0, C_Q_PAD - C_Q_RANK), (0, 0)))


def _pack_w_ukv(w):
    n = w.shape[0]
    w = w.astype(BF16).reshape(n, C_KV_RANK, C_HEADS, C_NOPE + C_V)
    k = _c_slab(w[..., :C_NOPE], None).reshape(n, C_KV_RANK, C_HEADS * LANES)
    v = w[..., C_NOPE:].reshape(n, C_KV_RANK, C_HEADS * C_V)
    return jnp.concatenate([k, v], axis=-1)


def _pack_w_out(w):
    n, _, dd = w.shape
    pair = lambda r: r.reshape(n, 2, N_SLABS, HEAD_DIM, dd).transpose(0, 2, 1, 3, 4).reshape(n, AB_W, dd)
    return jnp.concatenate([pair(w[:, :AB_W]), pair(w[:, AB_W:2 * AB_W]), w[:, 2 * AB_W:]], axis=1).astype(BF16)


def _group_sum_matrix():
    head = np.concatenate([_HEAD_OF_LANE, 2 + _HEAD_OF_LANE])
    return jnp.asarray(head[:, None] == head[None, :], BF16)


def _rope_tables(s_len):
    t = np.arange(s_len, dtype=np.float64)[:, None]
    sign = np.where(_LANE < HALF, -1.0, 1.0)[None, :]

    def tables(pos, freq, dim, active=None):
        ang = pos * (ROPE_THETA ** (-2.0 * freq / dim))[None, :]
        cos, sin = np.cos(ang), np.sin(ang) * sign
        if active is not None:
            cos, sin = np.where(active[None, :], cos, 1.0), np.where(active[None, :], sin, 0.0)
        return [cos, sin]

    pos_a = np.where((_DIM_A < HEAD_DIM // 2)[None, :], t // GRID_W, t % GRID_W)
    rope_c = _C_ROPE_OF_LANE >= 0
    tabs = (tables(pos_a, _DIM_A % 16, HEAD_DIM // 2) + tables(t, _DIM_B % 32, HEAD_DIM)
            + tables(t, np.maximum(_C_ROPE_OF_LANE, 0) % 16, C_ROPE, rope_c))
    return jnp.asarray(np.stack(tabs).astype(np.float32))


def _pack_body(w_ref, pqa_ref, pka_ref, pqb_ref, pkb_ref, pg_ref, o_ref):
    def move(src, dst, width, perm_ref):
        x = w_ref[:, src:src + width]
        if perm_ref is not None:
            x = jnp.dot(x, perm_ref[...], preferred_element_type=F32).astype(BF16)
        o_ref[:, dst:dst + width] = x

    move(IN_AQ, AB_QA, AB_W, pqa_ref)
    move(IN_AK, AB_KA, LANES, pka_ref)
    move(IN_BQ, AB_QB, AB_W, pqb_ref)
    move(IN_BK, AB_KB, LANES, pkb_ref)
    move(IN_AV, AB_VA, LANES, None)
    move(IN_BV, AB_VB, LANES, None)
    move(IN_AG, AB_GA, AB_W, pg_ref)
    move(IN_BG, AB_GB, AB_W, pg_ref)


def _pack_w_ab(w_in):
    depth, d, _ = w_in.shape
    tm = min(PACK_TM, d)
    perms = [_permutation(_slab_sources(_DIM_A, N_SLABS)), _permutation(_slab_sources(_DIM_A, 1)),
             _permutation(_slab_sources(_DIM_B, N_SLABS)), _permutation(_slab_sources(_DIM_B, 1)),
             _permutation(_pair_sources())]
    return pl.pallas_call(
        _pack_body,
        grid=(depth, d // tm),
        in_specs=[pl.BlockSpec((None, tm, IN_AB_W), lambda l, r: (l, r, 0))]
                 + [pl.BlockSpec(p.shape, lambda l, r: (0, 0)) for p in perms],
        out_specs=pl.BlockSpec((None, tm, IN_AB_W), lambda l, r: (l, r, 0)),
        out_shape=jax.ShapeDtypeStruct((depth, d, IN_AB_W), BF16),
        compiler_params=pltpu.CompilerParams(
            dimension_semantics=("arbitrary", "arbitrary"), vmem_limit_bytes=VMEM_LIMIT),
        name="pack",
    )(w_in, *perms)


def _proj_body(x_ref, *refs):
    _proj_compute(x_ref[...], *refs)


def _out_proj_body(x_ref, oa_ref, ob_ref, oc_ref, wo_ref, *refs):
    proj_refs, x_out_ref = refs[:-1], refs[-1]
    o = jnp.concatenate([oa_ref[...], ob_ref[...], oc_ref[...]], axis=1)
    x = x_ref[...] + jnp.dot(o, wo_ref[...], preferred_element_type=F32)
    x_out_ref[...] = x
    _proj_compute(x, *proj_refs)


def _proj_compute(x, g_ref, wc_ref, wab_ref, aqn_ref, akn_ref, cqn_ref, ckvn_ref, wuq_ref, wukv_ref, e_ref, tab_ref,
                  qab_ref, ka_ref, va_ref, kb_ref, vb_ref, sg_ref, qc_ref, kc_ref, vc_ref):
    h = (x * lax.rsqrt(jnp.mean(x * x, axis=-1, keepdims=True) + EPS) * g_ref[...]).astype(BF16)
    zc = jnp.dot(h, wc_ref[...], preferred_element_type=F32)
    zab = jnp.dot(h, wab_ref[...], preferred_element_type=F32)

    def rope(xs, mixer):
        return xs * tab_ref[2 * mixer] + pltpu.roll(xs, HALF, 1) * tab_ref[2 * mixer + 1]

    def slab(z, off):
        return z[:, off:off + LANES]

    def head_sumsq(s0, s1):
        t = jnp.concatenate([s0 * s0, s1 * s1], axis=1).astype(BF16)
        ss = jnp.dot(t, e_ref[...], preferred_element_type=F32)
        return ss[:, :LANES], ss[:, LANES:]

    def head_norm(xs, ss, gain):
        return xs * lax.rsqrt(ss * (1.0 / HEAD_DIM) + EPS) * gain

    def silu(gz):
        return (gz * (0.5 * jnp.tanh(0.5 * gz) + 0.5)).astype(BF16)

    cq = zc[:, C_CQ:C_CQ + C_Q_PAD]
    cq = cq * lax.rsqrt(jnp.sum(cq * cq, axis=-1, keepdims=True) * (1.0 / C_Q_RANK) + EPS) * cqn_ref[...]
    qc = jnp.dot(cq.astype(BF16), wuq_ref[...], preferred_element_type=F32)
    scale_c = (C_NOPE + C_ROPE) ** -0.5 * LOG2E
    for hh in range(C_HEADS):
        q = rope(qc[:, hh * LANES:(hh + 1) * LANES], 2)
        qc_ref[:, hh * LANES:(hh + 1) * LANES] = (q * scale_c).astype(BF16)
    ckv = zc[:, C_CKV:C_CKV + C_KV_RANK]
    ckv = ckv * lax.rsqrt(jnp.mean(ckv * ckv, axis=-1, keepdims=True) + EPS) * ckvn_ref[...]
    kv = jnp.dot(ckv.astype(BF16), wukv_ref[...], preferred_element_type=F32)
    kr = rope(slab(zc, C_CKR), 2)
    for hh in range(C_HEADS):
        kc_ref[hh] = (kv[:, hh * LANES:(hh + 1) * LANES] + kr).astype(BF16)
    ones = jnp.ones((x.shape[0], LANES), BF16)
    for p in range(C_HEADS // 2):
        v_pair = kv[:, (C_HEADS + p) * LANES:(C_HEADS + p + 1) * LANES].astype(BF16)
        vc_ref[:, 2 * p * LANES:(2 * p + 2) * LANES] = jnp.concatenate([v_pair, ones], axis=1)
    sg_ref[:, 2 * AB_W:] = silu(zc[:, C_G:])

    scale_ab = HEAD_DIM ** -0.5 * LOG2E
    qa = [slab(zab, AB_QA + j * LANES) for j in range(N_SLABS)]
    ka = slab(zab, AB_KA)
    ss0, ss1 = head_sumsq(qa[0], qa[1])
    ss2, ssk = head_sumsq(qa[2], ka)
    for j, ss in enumerate((ss0, ss1, ss2)):
        q = rope(head_norm(qa[j], ss, aqn_ref[...]), 0)
        qab_ref[:, j * LANES:(j + 1) * LANES] = (q * scale_ab).astype(BF16)
    ka_ref[...] = rope(head_norm(ka, ssk, akn_ref[...]), 0).astype(BF16)
    for j in range(N_SLABS):
        q = rope(slab(zab, AB_QB + j * LANES), 1)
        qab_ref[:, (N_SLABS + j) * LANES:(N_SLABS + j + 1) * LANES] = (q * scale_ab).astype(BF16)
    kb_ref[...] = rope(slab(zab, AB_KB), 1).astype(BF16)
    va_ref[...] = jnp.concatenate([slab(zab, AB_VA).astype(BF16), ones], axis=1)
    vb_ref[...] = jnp.concatenate([slab(zab, AB_VB).astype(BF16), ones], axis=1)
    sg_ref[:, :2 * AB_W] = silu(zab[:, AB_GA:AB_GA + 2 * AB_W])


def _layer_spec(a, layer):
    return pl.BlockSpec((None,) + a.shape[1:], lambda *_: (layer,) + (0,) * (a.ndim - 1))


def _proj_call(x, layer, weights, e_mat, tabs, prev_out=None):
    bsz, s_len, d = x.shape
    tm = min(PROJ_TM, s_len)
    grid = (s_len // tm, bsz)
    tok = lambda w: pl.BlockSpec((None, tm, w), lambda s, b: (b, s, 0))
    bf = lambda *shape: jax.ShapeDtypeStruct(shape, BF16)
    widths = [2 * N_SLABS * LANES, LANES, 2 * LANES, LANES, 2 * LANES, MIX_W, C_HEADS * LANES]
    in_specs = [tok(d)]
    args = [x]
    out_specs = ([tok(w) for w in widths]
                 + [pl.BlockSpec((None, C_HEADS, tm, LANES), lambda s, b: (b, 0, s, 0)), tok(C_HEADS * LANES)])
    out_shape = ([bf(bsz, s_len, w) for w in widths]
                 + [bf(bsz, C_HEADS, s_len, LANES), bf(bsz, s_len, C_HEADS * LANES)])
    body = _proj_body
    if prev_out is not None:
        oa, ob, oc, w_out = prev_out
        in_specs += [tok(oa.shape[-1]), tok(ob.shape[-1]), tok(oc.shape[-1]), _layer_spec(w_out, layer - 1)]
        args += [oa, ob, oc, w_out]
        out_specs.append(tok(d))
        out_shape.append(jax.ShapeDtypeStruct(x.shape, F32))
        body = _out_proj_body
    return pl.pallas_call(
        body,
        grid=grid,
        in_specs=in_specs + [_layer_spec(a, layer) for a in weights]
                 + [pl.BlockSpec(e_mat.shape, lambda s, b: (0, 0)),
                    pl.BlockSpec((6, tm, LANES), lambda s, b: (0, s, 0))],
        out_specs=out_specs,
        out_shape=out_shape,
        compiler_params=pltpu.CompilerParams(
            dimension_semantics=("arbitrary", "arbitrary"), vmem_limit_bytes=VMEM_LIMIT),
        name="proj" if prev_out is None else "out_proj",
    )(*args, *weights, e_mat, tabs)


def _scores(q, k):
    return lax.dot_general(q, k, (((1,), (1,)), ((), ())), preferred_element_type=F32)


def _define_first_slot(t, s_b, m_b):
    @pl.when(t == 0)
    def _():
        s_b[...] = jnp.zeros(s_b.shape, F32)
        m_b[...] = jnp.zeros(m_b.shape, F32)


def _run_parity(t, step, s_a, m_a, s_b, m_b):
    parity = lax.rem(t, 2)

    @pl.when(parity == 0)
    def _():
        step(s_a, m_a, s_b, m_b)

    @pl.when(parity == 1)
    def _():
        step(s_b, m_b, s_a, m_a)


def _softmax_numerators(s_r, m_r, rows, width):
    m = m_r[rows, :]
    return jnp.concatenate(
        [jnp.exp2(s_r[rows, j * LANES:(j + 1) * LANES] - m).astype(BF16) for j in range(width // LANES)], axis=1)


def _stack_queries(q, head_a):
    slabs = [q[:, j * LANES:(j + 1) * LANES] for j in range(N_SLABS)]
    return jnp.concatenate([jnp.where(head_a, s, 0.0) for s in slabs]
                           + [jnp.where(head_a, 0.0, s) for s in slabs], axis=0).astype(BF16)


def _stack_pipe_body(q_ref, k_ref, v_ref, sg_ref, o_ref, s_a, s_b, m_a, m_b):
    t = pl.program_id(0)
    tq = q_ref.shape[0]
    s_len = k_ref.shape[0]
    _define_first_slot(t, s_b, m_b)

    sub = min(ATTN_A_SUB, tq)
    lane = lax.broadcasted_iota(jnp.int32, (sub, LANES), 1)
    out_low = lane < HALF

    def step(s_w, m_w, s_r, m_r):
        q = q_ref[...].astype(F32)
        for u in range(tq // sub):
            s = _scores(_stack_queries(q[u * sub:(u + 1) * sub], (lane & QUARTER) == 0), k_ref[...])
            for i in range(AB_HEADS):
                blk = s[i * sub:(i + 1) * sub]
                rows = pl.ds((u * AB_HEADS + i) * sub, sub)
                s_w[rows, :] = blk
                m_w[rows, :] = jnp.broadcast_to(jnp.max(blk, axis=1, keepdims=True), (sub, LANES))
        for u in range(tq // sub):
            rows = [pl.ds((u * AB_HEADS + i) * sub, sub) for i in range(AB_HEADS)]
            p = jnp.concatenate([_softmax_numerators(s_r, m_r, rows[i], s_len) for i in range(AB_HEADS)], axis=0)
            a = jnp.dot(p, v_ref[...], preferred_element_type=F32)
            for j in range(N_SLABS):
                lo, hi = a[j * sub:(j + 1) * sub], a[(j + N_SLABS) * sub:(j + N_SLABS + 1) * sub]
                o = jnp.where(out_low, lo[:, :LANES] / lo[:, LANES:], hi[:, :LANES] / hi[:, LANES:])
                cols = slice(j * LANES, (j + 1) * LANES)
                gate = sg_ref[u * sub:(u + 1) * sub, cols].astype(F32)
                o_ref[u * sub:(u + 1) * sub, cols] = (o * gate).astype(BF16)

    _run_parity(t, step, s_a, m_a, s_b, m_b)


def _window_pipe_body(sink_ref, q_ref, k_ref, v_ref, sg_ref, o_ref, s_a, s_b, m_a, m_b, *, n_q, n_items, layer):
    t = pl.program_id(0)
    tq = q_ref.shape[0]
    s_len = k_ref.shape[0]
    sub = WINDOW
    kw = 3 * WINDOW
    n_sub = tq // sub
    t_cur = jnp.minimum(t, n_items - 1)
    t_prev = jnp.maximum(t - 1, 0)
    _define_first_slot(t, s_b, m_b)

    lane = lax.broadcasted_iota(jnp.int32, (sub, LANES), 1)
    row = lax.broadcasted_iota(jnp.int32, (sub, LANES), 0)
    head_a = (lane & QUARTER) == 0
    out_low = lane < HALF
    row0_cur = lax.rem(t_cur, n_q) * tq
    row0_prev = lax.rem(t_prev, n_q) * tq
    sinks = [sink_ref[layer, i] * LOG2E for i in range(AB_HEADS)]

    def win_start(r0):
        return pl.multiple_of(jnp.clip(r0 - WINDOW, 0, s_len - kw), WINDOW)

    def step(s_w, m_w, s_r, m_r):
        q = q_ref[...].astype(F32)
        for u in range(n_sub):
            r0 = row0_cur + u * sub
            ws = win_start(r0)
            dist = (r0 - ws) + row - lane
            bias = [jnp.where(jnp.abs(dist - c * LANES) <= WINDOW, 0.0, NEG) for c in range(3)]
            s = _scores(_stack_queries(q[u * sub:(u + 1) * sub], head_a), k_ref[pl.ds(ws, kw), :])
            for i in range(AB_HEADS):
                blk = s[i * sub:(i + 1) * sub]
                cols = [blk[:, c * LANES:(c + 1) * LANES] + bias[c] for c in range(3)]
                mx = jnp.max(jnp.maximum(jnp.maximum(cols[0], cols[1]), cols[2]), axis=1, keepdims=True)
                rows = pl.ds((u * AB_HEADS + i) * sub, sub)
                for c in range(3):
                    s_w[rows, c * LANES:(c + 1) * LANES] = cols[c]
                m_w[rows, :] = jnp.broadcast_to(jnp.maximum(mx, sinks[i]), (sub, LANES))
        for u in range(n_sub):
            v_win = v_ref[pl.ds(win_start(row0_prev + u * sub), kw), :]
            rows = [pl.ds((u * AB_HEADS + i) * sub, sub) for i in range(AB_HEADS)]
            p = jnp.concatenate([_softmax_numerators(s_r, m_r, rows[i], kw) for i in range(AB_HEADS)], axis=0)
            a = jnp.dot(p, v_win, preferred_element_type=F32)
            for j in range(N_SLABS):
                parts = []
                for i in (j, j + N_SLABS):
                    blk = a[i * sub:(i + 1) * sub]
                    den = blk[:, LANES:] + jnp.exp2(sinks[i] - m_r[rows[i], :])
                    parts.append(blk[:, :LANES] / den)
                cols = slice(j * LANES, (j + 1) * LANES)
                gate = sg_ref[u * sub:(u + 1) * sub, cols].astype(F32)
                o_ref[u * sub:(u + 1) * sub, cols] = (jnp.where(out_low, parts[0], parts[1]) * gate).astype(BF16)

    _run_parity(t, step, s_a, m_a, s_b, m_b)


def _stack_call(name, qab, k, v, sg, blk, tq, sink=None, layer=0):
    bsz, s_len, _ = qab.shape
    tq = min(tq, s_len)
    n_q = s_len // tq
    n_items = bsz * n_q
    windowed = sink is not None
    cur = lambda t: (jnp.minimum(t, n_items - 1) // n_q, jnp.minimum(t, n_items - 1) % n_q)
    prev = lambda t: (jnp.maximum(t - 1, 0) // n_q, jnp.maximum(t - 1, 0) % n_q)
    in_specs = [
        pl.BlockSpec((None, tq, AB_W), lambda t: (*cur(t), blk)),
        pl.BlockSpec((None, s_len, LANES), lambda t: (cur(t)[0], 0, 0)),
        pl.BlockSpec((None, s_len, 2 * LANES), lambda t: (prev(t)[0], 0, 0)),
        pl.BlockSpec((None, tq, AB_W), lambda t: (*prev(t), blk)),
    ]
    args = [qab, k, v, sg]
    rows = AB_HEADS * tq
    width = s_len
    body = _stack_pipe_body
    if windowed:
        in_specs = [pl.BlockSpec(memory_space=pltpu.SMEM)] + in_specs
        args = [sink] + args
        width = 3 * WINDOW
        body = functools.partial(_window_pipe_body, n_q=n_q, n_items=n_items, layer=layer)
    scratch = [pltpu.VMEM((rows, width), F32), pltpu.VMEM((rows, width), F32),
               pltpu.VMEM((rows, LANES), F32), pltpu.VMEM((rows, LANES), F32)]
    return pl.pallas_call(
        body,
        grid=(n_items + 1,),
        in_specs=in_specs,
        out_specs=pl.BlockSpec((None, tq, AB_W), lambda t: (*prev(t), 0)),
        out_shape=jax.ShapeDtypeStruct((bsz, s_len, AB_W), BF16),
        scratch_shapes=scratch,
        compiler_params=pltpu.CompilerParams(dimension_semantics=("arbitrary",), vmem_limit_bytes=VMEM_LIMIT),
        name=name,
    )(*args)


def _pair_pipe_body(q0_ref, q1_ref, k0_ref, k1_ref, v_ref, sg_ref, o_ref, s_a, s_b, m_a, m_b):
    t = pl.program_id(0)
    tq, s_len = s_a.shape[1], s_a.shape[2]
    _define_first_slot(t, s_b, m_b)
    out_low = lax.broadcasted_iota(jnp.int32, (tq, LANES), 1) < HALF
    qs, ks = (q0_ref, q1_ref), (k0_ref, k1_ref)

    def step(s_w, m_w, s_r, m_r):
        for hd in range(2):
            s = _scores(qs[hd][...], ks[hd][...])
            s_w[hd] = s
            m_w[hd] = jnp.broadcast_to(jnp.max(s, axis=1, keepdims=True), (tq, LANES))
        outs = []
        for hd in range(2):
            p = _softmax_numerators(s_r.at[hd], m_r.at[hd], slice(None), s_len)
            a = jnp.dot(p, v_ref[...], preferred_element_type=F32)
            outs.append(a[:, :LANES] / a[:, LANES:])
        o_ref[...] = (jnp.where(out_low, outs[0], outs[1]) * sg_ref[...].astype(F32)).astype(BF16)

    _run_parity(t, step, s_a, m_a, s_b, m_b)


def _attn_c_call(qc, kc, vc, sg, g_off):
    bsz, s_len, _ = qc.shape
    tq = min(ATTN_C_TQ, s_len)
    n_q = s_len // tq
    n_pairs = C_HEADS // 2
    n_items = bsz * n_pairs * n_q

    def item(t):
        return t // (n_pairs * n_q), (t // n_q) % n_pairs, t % n_q

    cur = lambda t: item(jnp.minimum(t, n_items - 1))
    prev = lambda t: item(jnp.maximum(t - 1, 0))

    def q_spec(o):
        def imap(t):
            b, p, i = cur(t)
            return b, i, 2 * p + o
        return pl.BlockSpec((None, tq, LANES), imap)

    def k_spec(o):
        def imap(t):
            b, p, i = cur(t)
            return b, 2 * p + o, 0, 0
        return pl.BlockSpec((None, None, s_len, LANES), imap)

    def v_map(t):
        b, p, i = prev(t)
        return b, 0, p

    def sg_map(t):
        b, p, i = prev(t)
        return b, i, g_off + p

    def o_map(t):
        b, p, i = prev(t)
        return b, i, p

    return pl.pallas_call(
        _pair_pipe_body,
        grid=(n_items + 1,),
        in_specs=[q_spec(0), q_spec(1), k_spec(0), k_spec(1),
                  pl.BlockSpec((None, s_len, 2 * LANES), v_map), pl.BlockSpec((None, tq, LANES), sg_map)],
        out_specs=pl.BlockSpec((None, tq, LANES), o_map),
        out_shape=jax.ShapeDtypeStruct((bsz, s_len, n_pairs * LANES), BF16),
        scratch_shapes=[pltpu.VMEM((2, tq, s_len), F32), pltpu.VMEM((2, tq, s_len), F32),
                        pltpu.VMEM((2, tq, LANES), F32), pltpu.VMEM((2, tq, LANES), F32)],
        compiler_params=pltpu.CompilerParams(dimension_semantics=("arbitrary",), vmem_limit_bytes=VMEM_LIMIT),
        name="attn_c",
    )(qc, qc, kc, kc, vc, sg)


def _out_body(x_ref, oa_ref, ob_ref, oc_ref, w_ref, fg_ref, y_ref):
    o = jnp.concatenate([oa_ref[...], ob_ref[...], oc_ref[...]], axis=1)
    r = x_ref[...] + jnp.dot(o, w_ref[...], preferred_element_type=F32)
    y_ref[...] = r * lax.rsqrt(jnp.mean(r * r, axis=-1, keepdims=True) + EPS) * fg_ref[...]


def _out_call(x, oa, ob, oc, w_out, layer, fg):
    bsz, s_len, d = x.shape
    tm = min(OUT_TM, s_len)
    tok = lambda a: pl.BlockSpec((None, tm, a.shape[-1]), lambda b, s: (b, s, 0))
    return pl.pallas_call(
        _out_body,
        grid=(bsz, s_len // tm),
        in_specs=[tok(x), tok(oa), tok(ob), tok(oc), _layer_spec(w_out, layer),
                  pl.BlockSpec(fg.shape, lambda b, s: (0, 0))],
        out_specs=tok(x),
        out_shape=jax.ShapeDtypeStruct(x.shape, F32),
        compiler_params=pltpu.CompilerParams(
            dimension_semantics=("arbitrary", "arbitrary"), vmem_limit_bytes=VMEM_LIMIT),
        name="out_final",
    )(x, oa, ob, oc, w_out, fg)


def kernel(x, norm_g, w_in, a_q_norm, a_k_norm, b_sink, c_q_norm, c_kv_norm, c_w_uq, c_w_ukv, w_out, final_g):
    depth = w_in.shape[0]
    rows = lambda v: v.astype(F32)[:, None, :]
    w_in = w_in.astype(BF16)
    weights = [rows(norm_g), _pack_w_c(w_in), _pack_w_ab(w_in),
               rows(a_q_norm[:, _DIM_A]), rows(a_k_norm[:, _DIM_A]),
               rows(jnp.pad(c_q_norm, ((0, 0), (0, C_Q_PAD - C_Q_RANK)))), rows(c_kv_norm),
               _pack_w_uq(c_w_uq), _pack_w_ukv(c_w_ukv)]
    w_out_packed = _pack_w_out(w_out)
    sink = b_sink.astype(F32)
    tabs = _rope_tables(x.shape[1])
    e_mat = _group_sum_matrix()
    prev_out = None
    for l in range(depth):
        outs = _proj_call(x, l, weights, e_mat, tabs, prev_out=prev_out)
        if prev_out is not None:
            x = outs[-1]
        qab, ka, va, kb, vb, sg, qc, kc, vc = outs[:9]
        oa = _stack_call("attn_a", qab, ka, va, sg, 0, ATTN_A_TQ)
        ob = _stack_call("attn_b", qab, kb, vb, sg, 1, ATTN_B_TQ, sink=sink, layer=l)
        oc = _attn_c_call(qc, kc, vc, sg, 2 * N_SLABS)
        prev_out = (oa, ob, oc, w_out_packed)
    return _out_call(x, *prev_out, depth - 1, final_g.astype(F32)[None, :])
```

```python
import functools

import numpy as np
import jax
import jax.numpy as jnp
from jax import lax
from jax.experimental import pallas as pl
from jax.experimental.pallas import tpu as pltpu

F32 = jnp.float32
BF16 = jnp.bfloat16

GRID_W = 64
ROPE_THETA = 10000.0
EPS = 1e-6
HEAD_DIM = 64
WINDOW = 128
AB_HEADS = 6
C_HEADS = 4
C_NOPE = 64
C_ROPE = 32
C_V = 64
C_Q_RANK = 192
C_KV_RANK = 128

LANES = 128
HALF = LANES // 2
QUARTER = LANES // 4
N_SLABS = AB_HEADS // 2
AB_W = AB_HEADS * HEAD_DIM
C_Q_PAD = 256
NEG = -1e30
LOG2E = 1.4426950408889634

IN_AQ, IN_AK, IN_AV, IN_AG = 0, 384, 512, 640
IN_BQ, IN_BK, IN_BV, IN_BG = 1024, 1408, 1536, 1664
IN_CQ, IN_CKV, IN_CKR, IN_CG = 2048, 2240, 2368, 2400
IN_AB_W = 2048

AB_QA, AB_KA, AB_QB, AB_KB, AB_VA, AB_VB, AB_GA, AB_GB = 0, 384, 512, 896, 1024, 1152, 1280, 1664
C_CQ, C_CKV, C_CKR, C_G, PACKED_C_W = 0, 256, 384, 512, 768
MIX_W = 1024

PACK_TM = 512
PROJ_TM = 512
ATTN_A_TQ = 256
ATTN_A_SUB = 128
ATTN_B_TQ = 1024
ATTN_C_TQ = 1024
OUT_TM = 2048
VMEM_LIMIT = 56 * 1024 * 1024


_LANE = np.arange(LANES)
_HEAD_OF_LANE = (_LANE // QUARTER) % 2


def _dim_of_lane(first_half_dims, partner):
    f = np.asarray(first_half_dims)
    return np.where(_LANE < HALF, f[_LANE % QUARTER], f[_LANE % QUARTER] + partner)


_DIM_A = _dim_of_lane(list(range(16)) + list(range(32, 48)), 16)
_DIM_B = _dim_of_lane(list(range(32)), 32)

_C_ROPE_OF_LANE = np.full(LANES, -1)
_C_ROPE_OF_LANE[0:16] = np.arange(16)
_C_ROPE_OF_LANE[64:80] = 16 + np.arange(16)


def _permutation(src_of_dst):
    n = len(src_of_dst)
    p = np.zeros((n, n), np.float32)
    p[src_of_dst, np.arange(n)] = 1.0
    return jnp.asarray(p, BF16)


def _slab_sources(dim_of_lane, n_slabs):
    return np.concatenate([np.where(_HEAD_OF_LANE == 0, p, p + n_slabs) * HEAD_DIM + dim_of_lane
                           for p in range(n_slabs)])


def _pair_sources():
    d = np.arange(HEAD_DIM)
    return np.concatenate([np.concatenate([p * HEAD_DIM + d, (p + N_SLABS) * HEAD_DIM + d]) for p in range(N_SLABS)])


def _c_slab(x, rope):
    lead = x.shape[:-1]
    zeros = lambda n: jnp.zeros(lead + (n,), x.dtype)
    r0, r1 = (rope[..., :16], rope[..., 16:]) if rope is not None else (zeros(16), zeros(16))
    return jnp.concatenate([r0, x[..., :48], r1, x[..., 48:C_NOPE], zeros(LANES - C_NOPE - C_ROPE)], axis=-1)


def _pack_w_c(w):
    zeros = lambda n: jnp.zeros(w.shape[:-1] + (n,), w.dtype)
    ckr = w[..., IN_CKR:IN_CKR + C_ROPE]
    return jnp.concatenate([
        w[..., IN_CQ:IN_CQ + C_Q_RANK], zeros(C_Q_PAD - C_Q_RANK),
        w[..., IN_CKV:IN_CKV + C_KV_RANK],
        ckr[..., :16], zeros(48), ckr[..., 16:], zeros(48),
        w[..., IN_CG:IN_CG + C_HEADS * C_V]], axis=-1)


def _pack_w_uq(w):
    n = w.shape[0]
    w = w.astype(BF16).reshape(n, C_Q_RANK, C_HEADS, C_NOPE + C_ROPE)
    slab = _c_slab(w[..., :C_NOPE], w[..., C_NOPE:]).reshape(n, C_Q_RANK, C_HEADS * LANES)
    return jnp.pad(slab, ((0, 0), (0, C_Q_PAD - C_Q_RANK), (0, 0)))


def _pack_w_ukv(w):
    n = w.shape[0]
    w = w.astype(BF16).reshape(n, C_KV_RANK, C_HEADS, C_NOPE + C_V)
    k = _c_slab(w[..., :C_NOPE], None).reshape(n, C_KV_RANK, C_HEADS * LANES)
    v = w[..., C_NOPE:].reshape(n, C_KV_RANK, C_HEADS * C_V)
    return jnp.concatenate([k, v], axis=-1)


def _pack_w_out(w):
    n, _, dd = w.shape
    pair = lambda r: r.reshape(n, 2, N_SLABS, HEAD_DIM, dd).transpose(0, 2, 1, 3, 4).reshape(n, AB_W, dd)
    return jnp.concatenate([pair(w[:, :AB_W]), pair(w[:, AB_W:2 * AB_W]), w[:, 2 * AB_W:]], axis=1).astype(BF16)


def _group_sum_matrix():
    head = np.concatenate([_HEAD_OF_LANE, 2 + _HEAD_OF_LANE])
    return jnp.asarray(head[:, None] == head[None, :], BF16)


def _rope_tables(s_len):
    t = np.arange(s_len, dtype=np.float64)[:, None]
    sign = np.where(_LANE < HALF, -1.0, 1.0)[None, :]

    def tables(pos, freq, dim, active=None):
        ang = pos * (ROPE_THETA ** (-2.0 * freq / dim))[None, :]
        cos, sin = np.cos(ang), np.sin(ang) * sign
        if active is not None:
            cos, sin = np.where(active[None, :], cos, 1.0), np.where(active[None, :], sin, 0.0)
        return [cos, sin]

    pos_a = np.where((_DIM_A < HEAD_DIM // 2)[None, :], t // GRID_W, t % GRID_W)
    rope_c = _C_ROPE_OF_LANE >= 0
    tabs = (tables(pos_a, _DIM_A % 16, HEAD_DIM // 2) + tables(t, _DIM_B % 32, HEAD_DIM)
            + tables(t, np.maximum(_C_ROPE_OF_LANE, 0) % 16, C_ROPE, rope_c))
    return jnp.asarray(np.stack(tabs).astype(np.float32))


def _pack_body(w_ref, pqa_ref, pka_ref, pqb_ref, pkb_ref, pg_ref, o_ref):
    def move(src, dst, width, perm_ref):
        x = w_ref[:, src:src + width]
        if perm_ref is not None:
            x = jnp.dot(x, perm_ref[...], preferred_element_type=F32).astype(BF16)
        o_ref[:, dst:dst + width] = x

    move(IN_AQ, AB_QA, AB_W, pqa_ref)
    move(IN_AK, AB_KA, LANES, pka_ref)
    move(IN_BQ, AB_QB, AB_W, pqb_ref)
    move(IN_BK, AB_KB, LANES, pkb_ref)
    move(IN_AV, AB_VA, LANES, None)
    move(IN_BV, AB_VB, LANES, None)
    move(IN_AG, AB_GA, AB_W, pg_ref)
    move(IN_BG, AB_GB, AB_W, pg_ref)


def _pack_w_ab(w_in):
    depth, d, _ = w_in.shape
    tm = min(PACK_TM, d)
    perms = [_permutation(_slab_sources(_DIM_A, N_SLABS)), _permutation(_slab_sources(_DIM_A, 1)),
             _permutation(_slab_sources(_DIM_B, N_SLABS)), _permutation(_slab_sources(_DIM_B, 1)),
             _permutation(_pair_sources())]
    return pl.pallas_call(
        _pack_body,
        grid=(depth, d // tm),
        in_specs=[pl.BlockSpec((None, tm, IN_AB_W), lambda l, r: (l, r, 0))]
                 + [pl.BlockSpec(p.shape, lambda l, r: (0, 0)) for p in perms],
        out_specs=pl.BlockSpec((None, tm, IN_AB_W), lambda l, r: (l, r, 0)),
        out_shape=jax.ShapeDtypeStruct((depth, d, IN_AB_W), BF16),
        compiler_params=pltpu.CompilerParams(
            dimension_semantics=("arbitrary", "arbitrary"), vmem_limit_bytes=VMEM_LIMIT),
        name="pack",
    )(w_in, *perms)


def _proj_body(x_ref, *refs):
    _proj_compute(x_ref[...], *refs)


def _out_proj_body(x_ref, oa_ref, ob_ref, oc_ref, wo_ref, *refs):
    proj_refs, x_out_ref = refs[:-1], refs[-1]
    o = jnp.concatenate([oa_ref[...], ob_ref[...], oc_ref[...]], axis=1)
    x = x_ref[...] + jnp.dot(o, wo_ref[...], preferred_element_type=F32)
    x_out_ref[...] = x
    _proj_compute(x, *proj_refs)


def _proj_compute(x, g_ref, wc_ref, wab_ref, aqn_ref, akn_ref, cqn_ref, ckvn_ref, wuq_ref, wukv_ref, e_ref, tab_ref,
                  qab_ref, ka_ref, va_ref, kb_ref, vb_ref, sg_ref, qc_ref, kc_ref, vc_ref):
    h = (x * lax.rsqrt(jnp.mean(x * x, axis=-1, keepdims=True) + EPS) * g_ref[...]).astype(BF16)
    zc = jnp.dot(h, wc_ref[...], preferred_element_type=F32)
    zab = jnp.dot(h, wab_ref[...], preferred_element_type=F32)

    def rope(xs, mixer):
        return xs * tab_ref[2 * mixer] + pltpu.roll(xs, HALF, 1) * tab_ref[2 * mixer + 1]

    def slab(z, off):
        return z[:, off:off + LANES]

    def head_sumsq(s0, s1):
        t = jnp.concatenate([s0 * s0, s1 * s1], axis=1).astype(BF16)
        ss = jnp.dot(t, e_ref[...], preferred_element_type=F32)
        return ss[:, :LANES], ss[:, LANES:]

    def head_norm(xs, ss, gain):
        return xs * lax.rsqrt(ss * (1.0 / HEAD_DIM) + EPS) * gain

    def silu(gz):
        return (gz * (0.5 * jnp.tanh(0.5 * gz) + 0.5)).astype(BF16)

    cq = zc[:, C_CQ:C_CQ + C_Q_PAD]
    cq = cq * lax.rsqrt(jnp.sum(cq * cq, axis=-1, keepdims=True) * (1.0 / C_Q_RANK) + EPS) * cqn_ref[...]
    qc = jnp.dot(cq.astype(BF16), wuq_ref[...], preferred_element_type=F32)
    scale_c = (C_NOPE + C_ROPE) ** -0.5 * LOG2E
    for hh in range(C_HEADS):
        q = rope(qc[:, hh * LANES:(hh + 1) * LANES], 2)
        qc_ref[:, hh * LANES:(hh + 1) * LANES] = (q * scale_c).astype(BF16)
    ckv = zc[:, C_CKV:C_CKV + C_KV_RANK]
    ckv = ckv * lax.rsqrt(jnp.mean(ckv * ckv, axis=-1, keepdims=True) + EPS) * ckvn_ref[...]
    kv = jnp.dot(ckv.astype(BF16), wukv_ref[...], preferred_element_type=F32)
    kr = rope(slab(zc, C_CKR), 2)
    for hh in range(C_HEADS):
        kc_ref[hh] = (kv[:, hh * LANES:(hh + 1) * LANES] + kr).astype(BF16)
    ones = jnp.ones((x.shape[0], LANES), BF16)
    for p in range(C_HEADS // 2):
        v_pair = kv[:, (C_HEADS + p) * LANES:(C_HEADS + p + 1) * LANES].astype(BF16)
        vc_ref[:, 2 * p * LANES:(2 * p + 2) * LANES] = jnp.concatenate([v_pair, ones], axis=1)
    sg_ref[:, 2 * AB_W:] = silu(zc[:, C_G:])

    scale_ab = HEAD_DIM ** -0.5 * LOG2E
    qa = [slab(zab, AB_QA + j * LANES) for j in range(N_SLABS)]
    ka = slab(zab, AB_KA)
    ss0, ss1 = head_sumsq(qa[0], qa[1])
    ss2, ssk = head_sumsq(qa[2], ka)
    for j, ss in enumerate((ss0, ss1, ss2)):
        q = rope(head_norm(qa[j], ss, aqn_ref[...]), 0)
        qab_ref[:, j * LANES:(j + 1) * LANES] = (q * scale_ab).astype(BF16)
    ka_ref[...] = rope(head_norm(ka, ssk, akn_ref[...]), 0).astype(BF16)
    for j in range(N_SLABS):
        q = rope(slab(zab, AB_QB + j * LANES), 1)
        qab_ref[:, (N_SLABS + j) * LANES:(N_SLABS + j + 1) * LANES] = (q * scale_ab).astype(BF16)
    kb_ref[...] = rope(slab(zab, AB_KB), 1).astype(BF16)
    va_ref[...] = jnp.concatenate([slab(zab, AB_VA).astype(BF16), ones], axis=1)
    vb_ref[...] = jnp.concatenate([slab(zab, AB_VB).astype(BF16), ones], axis=1)
    sg_ref[:, :2 * AB_W] = silu(zab[:, AB_GA:AB_GA + 2 * AB_W])


def _layer_spec(a, layer):
    return pl.BlockSpec((None,) + a.shape[1:], lambda *_: (layer,) + (0,) * (a.ndim - 1))


def _proj_call(x, layer, weights, e_mat, tabs, prev_out=None):
    bsz, s_len, d = x.shape
    tm = min(PROJ_TM, s_len)
    grid = (s_len // tm, bsz)
    tok = lambda w: pl.BlockSpec((None, tm, w), lambda s, b: (b, s, 0))
    bf = lambda *shape: jax.ShapeDtypeStruct(shape, BF16)
    widths = [2 * N_SLABS * LANES, LANES, 2 * LANES, LANES, 2 * LANES, MIX_W, C_HEADS * LANES]
    in_specs = [tok(d)]
    args = [x]
    out_specs = ([tok(w) for w in widths]
                 + [pl.BlockSpec((None, C_HEADS, tm, LANES), lambda s, b: (b, 0, s, 0)), tok(C_HEADS * LANES)])
    out_shape = ([bf(bsz, s_len, w) for w in widths]
                 + [bf(bsz, C_HEADS, s_len, LANES), bf(bsz, s_len, C_HEADS * LANES)])
    body = _proj_body
    if prev_out is not None:
        oa, ob, oc, w_out = prev_out
        in_specs += [tok(oa.shape[-1]), tok(ob.shape[-1]), tok(oc.shape[-1]), _layer_spec(w_out, layer - 1)]
        args += [oa, ob, oc, w_out]
        out_specs.append(tok(d))
        out_shape.append(jax.ShapeDtypeStruct(x.shape, F32))
        body = _out_proj_body
    return pl.pallas_call(
        body,
        grid=grid,
        in_specs=in_specs + [_layer_spec(a, layer) for a in weights]
                 + [pl.BlockSpec(e_mat.shape, lambda s, b: (0, 0)),
                    pl.BlockSpec((6, tm, LANES), lambda s, b: (0, s, 0))],
        out_specs=out_specs,
        out_shape=out_shape,
        compiler_params=pltpu.CompilerParams(
            dimension_semantics=("arbitrary", "arbitrary"), vmem_limit_bytes=VMEM_LIMIT),
        name="proj" if prev_out is None else "out_proj",
    )(*args, *weights, e_mat, tabs)


def _divmod(t, n):
    if n & (n - 1) == 0:
        return lax.shift_right_logical(t, n.bit_length() - 1), t & (n - 1)
    return lax.div(t, n), lax.rem(t, n)


def _scores(q, k):
    return lax.dot_general(q, k, (((1,), (1,)), ((), ())), preferred_element_type=F32)


def _define_first_slot(t, s_b, m_b):
    @pl.when(t == 0)
    def _():
        s_b[...] = jnp.zeros(s_b.shape, F32)
        m_b[...] = jnp.zeros(m_b.shape, F32)


def _run_parity(t, step, s_a, m_a, s_b, m_b):
    parity = t & 1

    @pl.when(parity == 0)
    def _():
        step(s_a, m_a, s_b, m_b)

    @pl.when(parity == 1)
    def _():
        step(s_b, m_b, s_a, m_a)


def _softmax_numerators(s_r, m_r, rows, width):
    m = m_r[rows, :]
    return jnp.concatenate(
        [jnp.exp2(s_r[rows, j * LANES:(j + 1) * LANES] - m).astype(BF16) for j in range(width // LANES)], axis=1)


def _stack_queries(q, head_a):
    slabs = [q[:, j * LANES:(j + 1) * LANES] for j in range(N_SLABS)]
    return jnp.concatenate([jnp.where(head_a, s, 0.0) for s in slabs]
                           + [jnp.where(head_a, 0.0, s) for s in slabs], axis=0).astype(BF16)


def _stack_pipe_body(q_ref, k_ref, v_ref, sg_ref, o_ref, s_a, s_b, m_a, m_b):
    t = pl.program_id(0)
    tq = q_ref.shape[0]
    s_len = k_ref.shape[0]
    _define_first_slot(t, s_b, m_b)

    sub = min(ATTN_A_SUB, tq)
    lane = lax.broadcasted_iota(jnp.int32, (sub, LANES), 1)
    out_low = lane < HALF

    def step(s_w, m_w, s_r, m_r):
        q = q_ref[...].astype(F32)
        for u in range(tq // sub):
            s = _scores(_stack_queries(q[u * sub:(u + 1) * sub], (lane & QUARTER) == 0), k_ref[...])
            for i in range(AB_HEADS):
                blk = s[i * sub:(i + 1) * sub]
                rows = pl.ds((u * AB_HEADS + i) * sub, sub)
                s_w[rows, :] = blk
                m_w[rows, :] = jnp.broadcast_to(jnp.max(blk, axis=1, keepdims=True), (sub, LANES))
        for u in range(tq // sub):
            rows = [pl.ds((u * AB_HEADS + i) * sub, sub) for i in range(AB_HEADS)]
            p = jnp.concatenate([_softmax_numerators(s_r, m_r, rows[i], s_len) for i in range(AB_HEADS)], axis=0)
            a = jnp.dot(p, v_ref[...], preferred_element_type=F32)
            for j in range(N_SLABS):
                lo, hi = a[j * sub:(j + 1) * sub], a[(j + N_SLABS) * sub:(j + N_SLABS + 1) * sub]
                o = jnp.where(out_low, lo[:, :LANES] / lo[:, LANES:], hi[:, :LANES] / hi[:, LANES:])
                cols = slice(j * LANES, (j + 1) * LANES)
                gate = sg_ref[u * sub:(u + 1) * sub, cols].astype(F32)
                o_ref[u * sub:(u + 1) * sub, cols] = (o * gate).astype(BF16)

    _run_parity(t, step, s_a, m_a, s_b, m_b)


def _window_pipe_body(sink_ref, q_ref, k_ref, v_ref, sg_ref, o_ref, s_a, s_b, m_a, m_b, *, n_q, n_items, layer):
    t = pl.program_id(0)
    tq = q_ref.shape[0]
    s_len = k_ref.shape[0]
    sub = WINDOW
    kw = 3 * WINDOW
    n_sub = tq // sub
    t_cur = jnp.minimum(t, n_items - 1)
    t_prev = jnp.maximum(t - 1, 0)
    _define_first_slot(t, s_b, m_b)

    lane = lax.broadcasted_iota(jnp.int32, (sub, LANES), 1)
    row = lax.broadcasted_iota(jnp.int32, (sub, LANES), 0)
    head_a = (lane & QUARTER) == 0
    out_low = lane < HALF
    row0_cur = _divmod(t_cur, n_q)[1] * tq
    row0_prev = _divmod(t_prev, n_q)[1] * tq
    sinks = [sink_ref[layer, i] * LOG2E for i in range(AB_HEADS)]

    def win_start(r0):
        return pl.multiple_of(jnp.clip(r0 - WINDOW, 0, s_len - kw), WINDOW)

    def step(s_w, m_w, s_r, m_r):
        q = q_ref[...].astype(F32)
        for u in range(n_sub):
            r0 = row0_cur + u * sub
            ws = win_start(r0)
            dist = (r0 - ws) + row - lane
            bias = [jnp.where(jnp.abs(dist - c * LANES) <= WINDOW, 0.0, NEG) for c in range(3)]
            s = _scores(_stack_queries(q[u * sub:(u + 1) * sub], head_a), k_ref[pl.ds(ws, kw), :])
            for i in range(AB_HEADS):
                blk = s[i * sub:(i + 1) * sub]
                cols = [blk[:, c * LANES:(c + 1) * LANES] + bias[c] for c in range(3)]
                mx = jnp.max(jnp.maximum(jnp.maximum(cols[0], cols[1]), cols[2]), axis=1, keepdims=True)
                rows = pl.ds((u * AB_HEADS + i) * sub, sub)
                for c in range(3):
                    s_w[rows, c * LANES:(c + 1) * LANES] = cols[c]
                m_w[rows, :] = jnp.broadcast_to(jnp.maximum(mx, sinks[i]), (sub, LANES))
        for u in range(n_sub):
            v_win = v_ref[pl.ds(win_start(row0_prev + u * sub), kw), :]
            rows = [pl.ds((u * AB_HEADS + i) * sub, sub) for i in range(AB_HEADS)]
            p = jnp.concatenate([_softmax_numerators(s_r, m_r, rows[i], kw) for i in range(AB_HEADS)], axis=0)
            a = jnp.dot(p, v_win, preferred_element_type=F32)
            for j in range(N_SLABS):
                parts = []
                for i in (j, j + N_SLABS):
                    blk = a[i * sub:(i + 1) * sub]
                    den = blk[:, LANES:] + jnp.exp2(sinks[i] - m_r[rows[i], :])
                    parts.append(blk[:, :LANES] / den)
                cols = slice(j * LANES, (j + 1) * LANES)
                gate = sg_ref[u * sub:(u + 1) * sub, cols].astype(F32)
                o_ref[u * sub:(u + 1) * sub, cols] = (jnp.where(out_low, parts[0], parts[1]) * gate).astype(BF16)

    _run_parity(t, step, s_a, m_a, s_b, m_b)


def _stack_call(name, qab, k, v, sg, blk, tq, sink=None, layer=0):
    bsz, s_len, _ = qab.shape
    tq = min(tq, s_len)
    n_q = s_len // tq
    n_items = bsz * n_q
    windowed = sink is not None
    cur = lambda t: _divmod(jnp.minimum(t, n_items - 1), n_q)
    prev = lambda t: _divmod(jnp.maximum(t - 1, 0), n_q)
    in_specs = [
        pl.BlockSpec((None, tq, AB_W), lambda t: (*cur(t), blk)),
        pl.BlockSpec((None, s_len, LANES), lambda t: (cur(t)[0], 0, 0)),
        pl.BlockSpec((None, s_len, 2 * LANES), lambda t: (prev(t)[0], 0, 0)),
        pl.BlockSpec((None, tq, AB_W), lambda t: (*prev(t), blk)),
    ]
    args = [qab, k, v, sg]
    rows = AB_HEADS * tq
    width = s_len
    body = _stack_pipe_body
    if windowed:
        in_specs = [pl.BlockSpec(memory_space=pltpu.SMEM)] + in_specs
        args = [sink] + args
        width = 3 * WINDOW
        body = functools.partial(_window_pipe_body, n_q=n_q, n_items=n_items, layer=layer)
    scratch = [pltpu.VMEM((rows, width), F32), pltpu.VMEM((rows, width), F32),
               pltpu.VMEM((rows, LANES), F32), pltpu.VMEM((rows, LANES), F32)]
    return pl.pallas_call(
        body,
        grid=(n_items + 1,),
        in_specs=in_specs,
        out_specs=pl.BlockSpec((None, tq, AB_W), lambda t: (*prev(t), 0)),
        out_shape=jax.ShapeDtypeStruct((bsz, s_len, AB_W), BF16),
        scratch_shapes=scratch,
        compiler_params=pltpu.CompilerParams(dimension_semantics=("arbitrary",), vmem_limit_bytes=VMEM_LIMIT),
        name=name,
    )(*args)


def _pair_pipe_body(q0_ref, q1_ref, k0_ref, k1_ref, v_ref, sg_ref, o_ref, s_a, s_b, m_a, m_b):
    t = pl.program_id(0)
    tq, s_len = s_a.shape[1], s_a.shape[2]
    _define_first_slot(t, s_b, m_b)
    out_low = lax.broadcasted_iota(jnp.int32, (tq, LANES), 1) < HALF
    qs, ks = (q0_ref, q1_ref), (k0_ref, k1_ref)

    def step(s_w, m_w, s_r, m_r):
        for hd in range(2):
            s = _scores(qs[hd][...], ks[hd][...])
            s_w[hd] = s
            m_w[hd] = jnp.broadcast_to(jnp.max(s, axis=1, keepdims=True), (tq, LANES))
        outs = []
        for hd in range(2):
            p = _softmax_numerators(s_r.at[hd], m_r.at[hd], slice(None), s_len)
            a = jnp.dot(p, v_ref[...], preferred_element_type=F32)
            outs.append(a[:, :LANES] / a[:, LANES:])
        o_ref[...] = (jnp.where(out_low, outs[0], outs[1]) * sg_ref[...].astype(F32)).astype(BF16)

    _run_parity(t, step, s_a, m_a, s_b, m_b)


def _attn_c_call(qc, kc, vc, sg, g_off):
    bsz, s_len, _ = qc.shape
    tq = min(ATTN_C_TQ, s_len)
    n_q = s_len // tq
    n_pairs = C_HEADS // 2
    n_items = bsz * n_pairs * n_q

    def item(t):
        bp, i = _divmod(t, n_q)
        b, p = _divmod(bp, n_pairs)
        return b, p, i

    cur = lambda t: item(jnp.minimum(t, n_items - 1))
    prev = lambda t: item(jnp.maximum(t - 1, 0))

    def q_spec(o):
        def imap(t):
            b, p, i = cur(t)
            return b, i, 2 * p + o
        return pl.BlockSpec((None, tq, LANES), imap)

    def k_spec(o):
        def imap(t):
            b, p, i = cur(t)
            return b, 2 * p + o, 0, 0
        return pl.BlockSpec((None, None, s_len, LANES), imap)

    def v_map(t):
        b, p, i = prev(t)
        return b, 0, p

    def sg_map(t):
        b, p, i = prev(t)
        return b, i, g_off + p

    def o_map(t):
        b, p, i = prev(t)
        return b, i, p

    return pl.pallas_call(
        _pair_pipe_body,
        grid=(n_items + 1,),
        in_specs=[q_spec(0), q_spec(1), k_spec(0), k_spec(1),
                  pl.BlockSpec((None, s_len, 2 * LANES), v_map), pl.BlockSpec((None, tq, LANES), sg_map)],
        out_specs=pl.BlockSpec((None, tq, LANES), o_map),
        out_shape=jax.ShapeDtypeStruct((bsz, s_len, n_pairs * LANES), BF16),
        scratch_shapes=[pltpu.VMEM((2, tq, s_len), F32), pltpu.VMEM((2, tq, s_len), F32),
                        pltpu.VMEM((2, tq, LANES), F32), pltpu.VMEM((2, tq, LANES), F32)],
        compiler_params=pltpu.CompilerParams(dimension_semantics=("arbitrary",), vmem_limit_bytes=VMEM_LIMIT),
        name="attn_c",
    )(qc, qc, kc, kc, vc, sg)


def _out_body(x_ref, oa_ref, ob_ref, oc_ref, w_ref, fg_ref, y_ref):
    o = jnp.concatenate([oa_ref[...], ob_ref[...], oc_ref[...]], axis=1)
    r = x_ref[...] + jnp.dot(o, w_ref[...], preferred_element_type=F32)
    y_ref[...] = r * lax.rsqrt(jnp.mean(r * r, axis=-1, keepdims=True) + EPS) * fg_ref[...]


def _out_call(x, oa, ob, oc, w_out, layer, fg):
    bsz, s_len, d = x.shape
    tm = min(OUT_TM, s_len)
    tok = lambda a: pl.BlockSpec((None, tm, a.shape[-1]), lambda b, s: (b, s, 0))
    return pl.pallas_call(
        _out_body,
        grid=(bsz, s_len // tm),
        in_specs=[tok(x), tok(oa), tok(ob), tok(oc), _layer_spec(w_out, layer),
                  pl.BlockSpec(fg.shape, lambda b, s: (0, 0))],
        out_specs=tok(x),
        out_shape=jax.ShapeDtypeStruct(x.shape, F32),
        compiler_params=pltpu.CompilerParams(
            dimension_semantics=("arbitrary", "arbitrary"), vmem_limit_bytes=VMEM_LIMIT),
        name="out_final",
    )(x, oa, ob, oc, w_out, fg)


def kernel(x, norm_g, w_in, a_q_norm, a_k_norm, b_sink, c_q_norm, c_kv_norm, c_w_uq, c_w_ukv, w_out, final_g):
    depth = w_in.shape[0]
    rows = lambda v: v.astype(F32)[:, None, :]
    w_in = w_in.astype(BF16)
    weights = [rows(norm_g), _pack_w_c(w_in), _pack_w_ab(w_in),
               rows(a_q_norm[:, _DIM_A]), rows(a_k_norm[:, _DIM_A]),
               rows(jnp.pad(c_q_norm, ((0, 0), (0, C_Q_PAD - C_Q_RANK)))), rows(c_kv_norm),
               _pack_w_uq(c_w_uq), _pack_w_ukv(c_w_ukv)]
    w_out_packed = _pack_w_out(w_out)
    sink = b_sink.astype(F32)
    tabs = _rope_tables(x.shape[1])
    e_mat = _group_sum_matrix()
    prev_out = None
    for l in range(depth):
        outs = _proj_call(x, l, weights, e_mat, tabs, prev_out=prev_out)
        if prev_out is not None:
            x = outs[-1]
        qab, ka, va, kb, vb, sg, qc, kc, vc = outs[:9]
        oa = _stack_call("attn_a", qab, ka, va, sg, 0, ATTN_A_TQ)
        ob = _stack_call("attn_b", qab, kb, vb, sg, 1, ATTN_B_TQ, sink=sink, layer=l)
        oc = _attn_c_call(qc, kc, vc, sg, 2 * N_SLABS)
        prev_out = (oa, ob, oc, w_out_packed)
    return _out_call(x, *prev_out, depth - 1, final_g.astype(F32)[None, :])
```

```python
import functools

import numpy as np
import jax
import jax.numpy as jnp
from jax import lax
from jax.experimental import pallas as pl
from jax.experimental.pallas import tpu as pltpu

F32 = jnp.float32
BF16 = jnp.bfloat16

GRID_W = 64
ROPE_THETA = 10000.0
EPS = 1e-6
HEAD_DIM = 64
WINDOW = 128
AB_HEADS = 6
C_HEADS = 4
C_NOPE = 64
C_ROPE = 32
C_V = 64
C_Q_RANK = 192
C_KV_RANK = 128

LANES = 128
HALF = LANES // 2
QUARTER = LANES // 4
N_SLABS = AB_HEADS // 2
AB_W = AB_HEADS * HEAD_DIM
C_Q_PAD = 256
NEG = -1e30
LOG2E = 1.4426950408889634

IN_AQ, IN_AK, IN_AV, IN_AG = 0, 384, 512, 640
IN_BQ, IN_BK, IN_BV, IN_BG = 1024, 1408, 1536, 1664
IN_CQ, IN_CKV, IN_CKR, IN_CG = 2048, 2240, 2368, 2400
IN_AB_W = 2048

AB_QA, AB_KA, AB_QB, AB_KB, AB_VA, AB_VB, AB_GA, AB_GB = 0, 384, 512, 896, 1024, 1152, 1280, 1664
C_CQ, C_CKV, C_CKR, C_G, PACKED_C_W = 0, 256, 384, 512, 768
MIX_W = 1024

PACK_TM = 512
PROJ_TM = 512
ATTN_A_TQ = 256
ATTN_A_SUB = 128
ATTN_B_TQ = 1024
ATTN_B_FINAL_TQ = 512
ATTN_C_TQ = 1024
VMEM_LIMIT = 56 * 1024 * 1024


_LANE = np.arange(LANES)
_HEAD_OF_LANE = (_LANE // QUARTER) % 2


def _dim_of_lane(first_half_dims, partner):
    f = np.asarray(first_half_dims)
    return np.where(_LANE < HALF, f[_LANE % QUARTER], f[_LANE % QUARTER] + partner)


_DIM_A = _dim_of_lane(list(range(16)) + list(range(32, 48)), 16)
_DIM_B = _dim_of_lane(list(range(32)), 32)

_C_ROPE_OF_LANE = np.full(LANES, -1)
_C_ROPE_OF_LANE[0:16] = np.arange(16)
_C_ROPE_OF_LANE[64:80] = 16 + np.arange(16)


def _permutation(src_of_dst):
    n = len(src_of_dst)
    p = np.zeros((n, n), np.float32)
    p[src_of_dst, np.arange(n)] = 1.0
    return jnp.asarray(p, BF16)


def _slab_sources(dim_of_lane, n_slabs):
    return np.concatenate([np.where(_HEAD_OF_LANE == 0, p, p + n_slabs) * HEAD_DIM + dim_of_lane
                           for p in range(n_slabs)])


def _pair_sources():
    d = np.arange(HEAD_DIM)
    return np.concatenate([np.concatenate([p * HEAD_DIM + d, (p + N_SLABS) * HEAD_DIM + d]) for p in range(N_SLABS)])


def _c_slab(x, rope):
    lead = x.shape[:-1]
    zeros = lambda n: jnp.zeros(lead + (n,), x.dtype)
    r0, r1 = (rope[..., :16], rope[..., 16:]) if rope is not None else (zeros(16), zeros(16))
    return jnp.concatenate([r0, x[..., :48], r1, x[..., 48:C_NOPE], zeros(LANES - C_NOPE - C_ROPE)], axis=-1)


def _pack_w_c(w):
    zeros = lambda n: jnp.zeros(w.shape[:-1] + (n,), w.dtype)
    ckr = w[..., IN_CKR:IN_CKR + C_ROPE]
    return jnp.concatenate([
        w[..., IN_CQ:IN_CQ + C_Q_RANK], zeros(C_Q_PAD - C_Q_RANK),
        w[..., IN_CKV:IN_CKV + C_KV_RANK],
        ckr[..., :16], zeros(48), ckr[..., 16:], zeros(48),
        w[..., IN_CG:IN_CG + C_HEADS * C_V]], axis=-1)


def _pack_w_uq(w):
    n = w.shape[0]
    w = w.astype(BF16).reshape(n, C_Q_RANK, C_HEADS, C_NOPE + C_ROPE)
    slab = _c_slab(w[..., :C_NOPE], w[..., C_NOPE:]).reshape(n, C_Q_RANK, C_HEADS * LANES)
    return jnp.pad(slab, ((0, 0), (0, C_Q_PAD - C_Q_RANK), (0, 0)))


def _pack_w_ukv(w):
    n = w.shape[0]
    w = w.astype(BF16).reshape(n, C_KV_RANK, C_HEADS, C_NOPE + C_V)
    k = _c_slab(w[..., :C_NOPE], None).reshape(n, C_KV_RANK, C_HEADS * LANES)
    v = w[..., C_NOPE:].reshape(n, C_KV_RANK, C_HEADS * C_V)
    return jnp.concatenate([k, v], axis=-1)


def _pack_w_out(w):
    n, _, dd = w.shape
    pair = lambda r: r.reshape(n, 2, N_SLABS, HEAD_DIM, dd).transpose(0, 2, 1, 3, 4).reshape(n, AB_W, dd)
    return jnp.concatenate([pair(w[:, :AB_W]), pair(w[:, AB_W:2 * AB_W]), w[:, 2 * AB_W:]], axis=1).astype(BF16)


def _group_sum_matrix():
    head = np.concatenate([_HEAD_OF_LANE, 2 + _HEAD_OF_LANE])
    return jnp.asarray(head[:, None] == head[None, :], BF16)


def _rope_tables(s_len):
    t = np.arange(s_len, dtype=np.float64)[:, None]
    sign = np.where(_LANE < HALF, -1.0, 1.0)[None, :]

    def tables(pos, freq, dim, active=None):
        ang = pos * (ROPE_THETA ** (-2.0 * freq / dim))[None, :]
        cos, sin = np.cos(ang), np.sin(ang) * sign
        if active is not None:
            cos, sin = np.where(active[None, :], cos, 1.0), np.where(active[None, :], sin, 0.0)
        return [cos, sin]

    pos_a = np.where((_DIM_A < HEAD_DIM // 2)[None, :], t // GRID_W, t % GRID_W)
    rope_c = _C_ROPE_OF_LANE >= 0
    tabs = (tables(pos_a, _DIM_A % 16, HEAD_DIM // 2) + tables(t, _DIM_B % 32, HEAD_DIM)
            + tables(t, np.maximum(_C_ROPE_OF_LANE, 0) % 16, C_ROPE, rope_c))
    return jnp.asarray(np.stack(tabs).astype(np.float32))


def _pack_body(w_ref, pqa_ref, pka_ref, pqb_ref, pkb_ref, pg_ref, o_ref):
    def move(src, dst, width, perm_ref):
        x = w_ref[:, src:src + width]
        if perm_ref is not None:
            x = jnp.dot(x, perm_ref[...], preferred_element_type=F32).astype(BF16)
        o_ref[:, dst:dst + width] = x

    move(IN_AQ, AB_QA, AB_W, pqa_ref)
    move(IN_AK, AB_KA, LANES, pka_ref)
    move(IN_BQ, AB_QB, AB_W, pqb_ref)
    move(IN_BK, AB_KB, LANES, pkb_ref)
    move(IN_AV, AB_VA, LANES, None)
    move(IN_BV, AB_VB, LANES, None)
    move(IN_AG, AB_GA, AB_W, pg_ref)
    move(IN_BG, AB_GB, AB_W, pg_ref)


def _pack_w_ab(w_in):
    depth, d, _ = w_in.shape
    tm = min(PACK_TM, d)
    perms = [_permutation(_slab_sources(_DIM_A, N_SLABS)), _permutation(_slab_sources(_DIM_A, 1)),
             _permutation(_slab_sources(_DIM_B, N_SLABS)), _permutation(_slab_sources(_DIM_B, 1)),
             _permutation(_pair_sources())]
    return pl.pallas_call(
        _pack_body,
        grid=(depth, d // tm),
        in_specs=[pl.BlockSpec((None, tm, IN_AB_W), lambda l, r: (l, r, 0))]
                 + [pl.BlockSpec(p.shape, lambda l, r: (0, 0)) for p in perms],
        out_specs=pl.BlockSpec((None, tm, IN_AB_W), lambda l, r: (l, r, 0)),
        out_shape=jax.ShapeDtypeStruct((depth, d, IN_AB_W), BF16),
        compiler_params=pltpu.CompilerParams(
            dimension_semantics=("arbitrary", "arbitrary"), vmem_limit_bytes=VMEM_LIMIT),
        name="pack",
    )(w_in, *perms)


def _proj_body(x_ref, *refs):
    _proj_compute(x_ref[...], *refs)


def _out_proj_body(x_ref, oa_ref, ob_ref, oc_ref, wo_ref, *refs):
    proj_refs, x_out_ref = refs[:-1], refs[-1]
    o = jnp.concatenate([oa_ref[...], ob_ref[...], oc_ref[...]], axis=1)
    x = x_ref[...] + jnp.dot(o, wo_ref[...], preferred_element_type=F32)
    x_out_ref[...] = x
    _proj_compute(x, *proj_refs)


def _proj_compute(x, g_ref, wc_ref, wab_ref, aqn_ref, akn_ref, cqn_ref, ckvn_ref, wuq_ref, wukv_ref, e_ref, tab_ref,
                  qab_ref, ka_ref, va_ref, kb_ref, vb_ref, sg_ref, qc_ref, kc_ref, vc_ref):
    h = (x * lax.rsqrt(jnp.mean(x * x, axis=-1, keepdims=True) + EPS) * g_ref[...]).astype(BF16)
    zc = jnp.dot(h, wc_ref[...], preferred_element_type=F32)
    zab = jnp.dot(h, wab_ref[...], preferred_element_type=F32)

    def rope(xs, mixer):
        return xs * tab_ref[2 * mixer] + pltpu.roll(xs, HALF, 1) * tab_ref[2 * mixer + 1]

    def slab(z, off):
        return z[:, off:off + LANES]

    def head_sumsq(s0, s1):
        t = jnp.concatenate([s0 * s0, s1 * s1], axis=1).astype(BF16)
        ss = jnp.dot(t, e_ref[...], preferred_element_type=F32)
        return ss[:, :LANES], ss[:, LANES:]

    def head_norm(xs, ss, gain):
        return xs * lax.rsqrt(ss * (1.0 / HEAD_DIM) + EPS) * gain

    def silu(gz):
        return (gz * (0.5 * jnp.tanh(0.5 * gz) + 0.5)).astype(BF16)

    cq = zc[:, C_CQ:C_CQ + C_Q_PAD]
    cq = cq * lax.rsqrt(jnp.sum(cq * cq, axis=-1, keepdims=True) * (1.0 / C_Q_RANK) + EPS) * cqn_ref[...]
    qc = jnp.dot(cq.astype(BF16), wuq_ref[...], preferred_element_type=F32)
    scale_c = (C_NOPE + C_ROPE) ** -0.5 * LOG2E
    for hh in range(C_HEADS):
        q = rope(qc[:, hh * LANES:(hh + 1) * LANES], 2)
        qc_ref[:, hh * LANES:(hh + 1) * LANES] = (q * scale_c).astype(BF16)
    ckv = zc[:, C_CKV:C_CKV + C_KV_RANK]
    ckv = ckv * lax.rsqrt(jnp.mean(ckv * ckv, axis=-1, keepdims=True) + EPS) * ckvn_ref[...]
    kv = jnp.dot(ckv.astype(BF16), wukv_ref[...], preferred_element_type=F32)
    kr = rope(slab(zc, C_CKR), 2)
    for hh in range(C_HEADS):
        kc_ref[hh] = (kv[:, hh * LANES:(hh + 1) * LANES] + kr).astype(BF16)
    ones = jnp.ones((x.shape[0], LANES), BF16)
    for p in range(C_HEADS // 2):
        v_pair = kv[:, (C_HEADS + p) * LANES:(C_HEADS + p + 1) * LANES].astype(BF16)
        vc_ref[:, 2 * p * LANES:(2 * p + 2) * LANES] = jnp.concatenate([v_pair, ones], axis=1)
    sg_ref[:, 2 * AB_W:] = silu(zc[:, C_G:])

    scale_ab = HEAD_DIM ** -0.5 * LOG2E
    qa = [slab(zab, AB_QA + j * LANES) for j in range(N_SLABS)]
    ka = slab(zab, AB_KA)
    ss0, ss1 = head_sumsq(qa[0], qa[1])
    ss2, ssk = head_sumsq(qa[2], ka)
    for j, ss in enumerate((ss0, ss1, ss2)):
        q = rope(head_norm(qa[j], ss, aqn_ref[...]), 0)
        qab_ref[:, j * LANES:(j + 1) * LANES] = (q * scale_ab).astype(BF16)
    ka_ref[...] = rope(head_norm(ka, ssk, akn_ref[...]), 0).astype(BF16)
    for j in range(N_SLABS):
        q = rope(slab(zab, AB_QB + j * LANES), 1)
        qab_ref[:, (N_SLABS + j) * LANES:(N_SLABS + j + 1) * LANES] = (q * scale_ab).astype(BF16)
    kb_ref[...] = rope(slab(zab, AB_KB), 1).astype(BF16)
    va_ref[...] = jnp.concatenate([slab(zab, AB_VA).astype(BF16), ones], axis=1)
    vb_ref[...] = jnp.concatenate([slab(zab, AB_VB).astype(BF16), ones], axis=1)
    sg_ref[:, :2 * AB_W] = silu(zab[:, AB_GA:AB_GA + 2 * AB_W])


def _layer_spec(a, layer):
    return pl.BlockSpec((None,) + a.shape[1:], lambda *_: (layer,) + (0,) * (a.ndim - 1))


def _proj_call(x, layer, weights, e_mat, tabs, prev_out=None):
    bsz, s_len, d = x.shape
    tm = min(PROJ_TM, s_len)
    grid = (s_len // tm, bsz)
    tok = lambda w: pl.BlockSpec((None, tm, w), lambda s, b: (b, s, 0))
    bf = lambda *shape: jax.ShapeDtypeStruct(shape, BF16)
    widths = [2 * N_SLABS * LANES, LANES, 2 * LANES, LANES, 2 * LANES, MIX_W, C_HEADS * LANES]
    in_specs = [tok(d)]
    args = [x]
    out_specs = ([tok(w) for w in widths]
                 + [pl.BlockSpec((None, C_HEADS, tm, LANES), lambda s, b: (b, 0, s, 0)), tok(C_HEADS * LANES)])
    out_shape = ([bf(bsz, s_len, w) for w in widths]
                 + [bf(bsz, C_HEADS, s_len, LANES), bf(bsz, s_len, C_HEADS * LANES)])
    body = _proj_body
    if prev_out is not None:
        oa, ob, oc, w_out = prev_out
        in_specs += [tok(oa.shape[-1]), tok(ob.shape[-1]), tok(oc.shape[-1]), _layer_spec(w_out, layer - 1)]
        args += [oa, ob, oc, w_out]
        out_specs.append(tok(d))
        out_shape.append(jax.ShapeDtypeStruct(x.shape, F32))
        body = _out_proj_body
    return pl.pallas_call(
        body,
        grid=grid,
        in_specs=in_specs + [_layer_spec(a, layer) for a in weights]
                 + [pl.BlockSpec(e_mat.shape, lambda s, b: (0, 0)),
                    pl.BlockSpec((6, tm, LANES), lambda s, b: (0, s, 0))],
        out_specs=out_specs,
        out_shape=out_shape,
        compiler_params=pltpu.CompilerParams(
            dimension_semantics=("arbitrary", "arbitrary"), vmem_limit_bytes=VMEM_LIMIT),
        name="proj" if prev_out is None else "out_proj",
    )(*args, *weights, e_mat, tabs)


def _divmod(t, n):
    if n & (n - 1) == 0:
        return lax.shift_right_logical(t, n.bit_length() - 1), t & (n - 1)
    return lax.div(t, n), lax.rem(t, n)


def _scores(q, k):
    return lax.dot_general(q, k, (((1,), (1,)), ((), ())), preferred_element_type=F32)


def _define_first_slot(t, s_b, m_b):
    @pl.when(t == 0)
    def _():
        s_b[...] = jnp.zeros(s_b.shape, F32)
        m_b[...] = jnp.zeros(m_b.shape, F32)


def _run_parity(t, step, s_a, m_a, s_b, m_b):
    parity = t & 1

    @pl.when(parity == 0)
    def _():
        step(s_a, m_a, s_b, m_b)

    @pl.when(parity == 1)
    def _():
        step(s_b, m_b, s_a, m_a)


def _softmax_numerators(s_r, m_r, rows, width):
    m = m_r[rows, :]
    return jnp.concatenate(
        [jnp.exp2(s_r[rows, j * LANES:(j + 1) * LANES] - m).astype(BF16) for j in range(width // LANES)], axis=1)


def _stack_queries(q, head_a):
    slabs = [q[:, j * LANES:(j + 1) * LANES] for j in range(N_SLABS)]
    return jnp.concatenate([jnp.where(head_a, s, 0.0) for s in slabs]
                           + [jnp.where(head_a, 0.0, s) for s in slabs], axis=0).astype(BF16)


def _stack_pipe_body(q_ref, k_ref, v_ref, sg_ref, o_ref, s_a, s_b, m_a, m_b):
    t = pl.program_id(0)
    tq = q_ref.shape[0]
    s_len = k_ref.shape[0]
    _define_first_slot(t, s_b, m_b)

    sub = min(ATTN_A_SUB, tq)
    lane = lax.broadcasted_iota(jnp.int32, (sub, LANES), 1)
    out_low = lane < HALF

    def step(s_w, m_w, s_r, m_r):
        q = q_ref[...].astype(F32)
        for u in range(tq // sub):
            s = _scores(_stack_queries(q[u * sub:(u + 1) * sub], (lane & QUARTER) == 0), k_ref[...])
            for i in range(AB_HEADS):
                blk = s[i * sub:(i + 1) * sub]
                rows = pl.ds((u * AB_HEADS + i) * sub, sub)
                s_w[rows, :] = blk
                m_w[rows, :] = jnp.broadcast_to(jnp.max(blk, axis=1, keepdims=True), (sub, LANES))
        for u in range(tq // sub):
            rows = [pl.ds((u * AB_HEADS + i) * sub, sub) for i in range(AB_HEADS)]
            p = jnp.concatenate([_softmax_numerators(s_r, m_r, rows[i], s_len) for i in range(AB_HEADS)], axis=0)
            a = jnp.dot(p, v_ref[...], preferred_element_type=F32)
            for j in range(N_SLABS):
                lo, hi = a[j * sub:(j + 1) * sub], a[(j + N_SLABS) * sub:(j + N_SLABS + 1) * sub]
                o = jnp.where(out_low, lo[:, :LANES] / lo[:, LANES:], hi[:, :LANES] / hi[:, LANES:])
                cols = slice(j * LANES, (j + 1) * LANES)
                gate = sg_ref[u * sub:(u + 1) * sub, cols].astype(F32)
                o_ref[u * sub:(u + 1) * sub, cols] = (o * gate).astype(BF16)

    _run_parity(t, step, s_a, m_a, s_b, m_b)


def _window_pipe_body(sink_ref, q_ref, k_ref, v_ref, sg_ref, *refs, n_q, n_items, layer, final):
    if final:
        x_ref, oa_ref, oc_ref, wo_ref, fg_ref, y_ref, s_a, s_b, m_a, m_b, o_ref = refs
    else:
        o_ref, s_a, s_b, m_a, m_b = refs
    t = pl.program_id(0)
    tq = q_ref.shape[0]
    s_len = k_ref.shape[0]
    sub = WINDOW
    kw = 3 * WINDOW
    n_sub = tq // sub
    t_cur = jnp.minimum(t, n_items - 1)
    t_prev = jnp.maximum(t - 1, 0)
    _define_first_slot(t, s_b, m_b)

    lane = lax.broadcasted_iota(jnp.int32, (sub, LANES), 1)
    row = lax.broadcasted_iota(jnp.int32, (sub, LANES), 0)
    head_a = (lane & QUARTER) == 0
    out_low = lane < HALF
    row0_cur = _divmod(t_cur, n_q)[1] * tq
    row0_prev = _divmod(t_prev, n_q)[1] * tq
    sinks = [sink_ref[layer, i] * LOG2E for i in range(AB_HEADS)]

    def win_start(r0):
        return pl.multiple_of(jnp.clip(r0 - WINDOW, 0, s_len - kw), WINDOW)

    def step(s_w, m_w, s_r, m_r):
        q = q_ref[...].astype(F32)
        for u in range(n_sub):
            r0 = row0_cur + u * sub
            ws = win_start(r0)
            dist = (r0 - ws) + row - lane
            bias = [jnp.where(jnp.abs(dist - c * LANES) <= WINDOW, 0.0, NEG) for c in range(3)]
            s = _scores(_stack_queries(q[u * sub:(u + 1) * sub], head_a), k_ref[pl.ds(ws, kw), :])
            for i in range(AB_HEADS):
                blk = s[i * sub:(i + 1) * sub]
                cols = [blk[:, c * LANES:(c + 1) * LANES] + bias[c] for c in range(3)]
                mx = jnp.max(jnp.maximum(jnp.maximum(cols[0], cols[1]), cols[2]), axis=1, keepdims=True)
                rows = pl.ds((u * AB_HEADS + i) * sub, sub)
                for c in range(3):
                    s_w[rows, c * LANES:(c + 1) * LANES] = cols[c]
                m_w[rows, :] = jnp.broadcast_to(jnp.maximum(mx, sinks[i]), (sub, LANES))
        for u in range(n_sub):
            v_win = v_ref[pl.ds(win_start(row0_prev + u * sub), kw), :]
            rows = [pl.ds((u * AB_HEADS + i) * sub, sub) for i in range(AB_HEADS)]
            p = jnp.concatenate([_softmax_numerators(s_r, m_r, rows[i], kw) for i in range(AB_HEADS)], axis=0)
            a = jnp.dot(p, v_win, preferred_element_type=F32)
            for j in range(N_SLABS):
                parts = []
                for i in (j, j + N_SLABS):
                    blk = a[i * sub:(i + 1) * sub]
                    den = blk[:, LANES:] + jnp.exp2(sinks[i] - m_r[rows[i], :])
                    parts.append(blk[:, :LANES] / den)
                cols = slice(j * LANES, (j + 1) * LANES)
                gate = sg_ref[u * sub:(u + 1) * sub, cols].astype(F32)
                o_ref[u * sub:(u + 1) * sub, cols] = (jnp.where(out_low, parts[0], parts[1]) * gate).astype(BF16)
        if final:
            o = jnp.concatenate([oa_ref[...], o_ref[...], oc_ref[...]], axis=1)
            r = x_ref[...] + jnp.dot(o, wo_ref[...], preferred_element_type=F32)
            y_ref[...] = r * lax.rsqrt(jnp.mean(r * r, axis=-1, keepdims=True) + EPS) * fg_ref[...]

    _run_parity(t, step, s_a, m_a, s_b, m_b)


def _stack_call(name, qab, k, v, sg, blk, tq, sink=None, layer=0, final=None):
    bsz, s_len, _ = qab.shape
    tq = min(tq, s_len)
    n_q = s_len // tq
    n_items = bsz * n_q
    windowed = sink is not None
    cur = lambda t: _divmod(jnp.minimum(t, n_items - 1), n_q)
    prev = lambda t: _divmod(jnp.maximum(t - 1, 0), n_q)
    in_specs = [
        pl.BlockSpec((None, tq, AB_W), lambda t: (*cur(t), blk)),
        pl.BlockSpec((None, s_len, LANES), lambda t: (cur(t)[0], 0, 0)),
        pl.BlockSpec((None, s_len, 2 * LANES), lambda t: (prev(t)[0], 0, 0)),
        pl.BlockSpec((None, tq, AB_W), lambda t: (*prev(t), blk)),
    ]
    args = [qab, k, v, sg]
    rows = AB_HEADS * tq
    width = s_len
    body = _stack_pipe_body
    if windowed:
        in_specs = [pl.BlockSpec(memory_space=pltpu.SMEM)] + in_specs
        args = [sink] + args
        width = 3 * WINDOW
        body = functools.partial(_window_pipe_body, n_q=n_q, n_items=n_items, layer=layer, final=final is not None)
    scratch = [pltpu.VMEM((rows, width), F32), pltpu.VMEM((rows, width), F32),
               pltpu.VMEM((rows, LANES), F32), pltpu.VMEM((rows, LANES), F32)]
    out_specs = pl.BlockSpec((None, tq, AB_W), lambda t: (*prev(t), 0))
    out_shape = jax.ShapeDtypeStruct((bsz, s_len, AB_W), BF16)
    if final is not None:
        x, oa, oc, w_out, fg = final
        tok = lambda a: pl.BlockSpec((None, tq, a.shape[-1]), lambda t: (*prev(t), 0))
        in_specs += [tok(x), tok(oa), tok(oc), _layer_spec(w_out, layer), pl.BlockSpec(fg.shape, lambda t: (0, 0))]
        args += [x, oa, oc, w_out, fg]
        out_specs, out_shape = tok(x), jax.ShapeDtypeStruct(x.shape, F32)
        scratch.append(pltpu.VMEM((tq, AB_W), BF16))
    return pl.pallas_call(
        body,
        grid=(n_items + 1,),
        in_specs=in_specs,
        out_specs=out_specs,
        out_shape=out_shape,
        scratch_shapes=scratch,
        compiler_params=pltpu.CompilerParams(dimension_semantics=("arbitrary",), vmem_limit_bytes=VMEM_LIMIT),
        name=name,
    )(*args)


def _pair_pipe_body(q0_ref, q1_ref, k0_ref, k1_ref, v_ref, sg_ref, o_ref, s_a, s_b, m_a, m_b):
    t = pl.program_id(0)
    tq, s_len = s_a.shape[1], s_a.shape[2]
    _define_first_slot(t, s_b, m_b)
    out_low = lax.broadcasted_iota(jnp.int32, (tq, LANES), 1) < HALF
    qs, ks = (q0_ref, q1_ref), (k0_ref, k1_ref)

    def step(s_w, m_w, s_r, m_r):
        for hd in range(2):
            s = _scores(qs[hd][...], ks[hd][...])
            s_w[hd] = s
            m_w[hd] = jnp.broadcast_to(jnp.max(s, axis=1, keepdims=True), (tq, LANES))
        outs = []
        for hd in range(2):
            p = _softmax_numerators(s_r.at[hd], m_r.at[hd], slice(None), s_len)
            a = jnp.dot(p, v_ref[...], preferred_element_type=F32)
            outs.append(a[:, :LANES] / a[:, LANES:])
        o_ref[...] = (jnp.where(out_low, outs[0], outs[1]) * sg_ref[...].astype(F32)).astype(BF16)

    _run_parity(t, step, s_a, m_a, s_b, m_b)


def _attn_c_call(qc, kc, vc, sg, g_off):
    bsz, s_len, _ = qc.shape
    tq = min(ATTN_C_TQ, s_len)
    n_q = s_len // tq
    n_pairs = C_HEADS // 2
    n_items = bsz * n_pairs * n_q

    def item(t):
        bp, i = _divmod(t, n_q)
        b, p = _divmod(bp, n_pairs)
        return b, p, i

    cur = lambda t: item(jnp.minimum(t, n_items - 1))
    prev = lambda t: item(jnp.maximum(t - 1, 0))

    def q_spec(o):
        def imap(t):
            b, p, i = cur(t)
            return b, i, 2 * p + o
        return pl.BlockSpec((None, tq, LANES), imap)

    def k_spec(o):
        def imap(t):
            b, p, i = cur(t)
            return b, 2 * p + o, 0, 0
        return pl.BlockSpec((None, None, s_len, LANES), imap)

    def v_map(t):
        b, p, i = prev(t)
        return b, 0, p

    def sg_map(t):
        b, p, i = prev(t)
        return b, i, g_off + p

    def o_map(t):
        b, p, i = prev(t)
        return b, i, p

    return pl.pallas_call(
        _pair_pipe_body,
        grid=(n_items + 1,),
        in_specs=[q_spec(0), q_spec(1), k_spec(0), k_spec(1),
                  pl.BlockSpec((None, s_len, 2 * LANES), v_map), pl.BlockSpec((None, tq, LANES), sg_map)],
        out_specs=pl.BlockSpec((None, tq, LANES), o_map),
        out_shape=jax.ShapeDtypeStruct((bsz, s_len, n_pairs * LANES), BF16),
        scratch_shapes=[pltpu.VMEM((2, tq, s_len), F32), pltpu.VMEM((2, tq, s_len), F32),
                        pltpu.VMEM((2, tq, LANES), F32), pltpu.VMEM((2, tq, LANES), F32)],
        compiler_params=pltpu.CompilerParams(dimension_semantics=("arbitrary",), vmem_limit_bytes=VMEM_LIMIT),
        name="attn_c",
    )(qc, qc, kc, kc, vc, sg)


def kernel(x, norm_g, w_in, a_q_norm, a_k_norm, b_sink, c_q_norm, c_kv_norm, c_w_uq, c_w_ukv, w_out, final_g):
    depth = w_in.shape[0]
    rows = lambda v: v.astype(F32)[:, None, :]
    w_in = w_in.astype(BF16)
    weights = [rows(norm_g), _pack_w_c(w_in), _pack_w_ab(w_in),
               rows(a_q_norm[:, _DIM_A]), rows(a_k_norm[:, _DIM_A]),
               rows(jnp.pad(c_q_norm, ((0, 0), (0, C_Q_PAD - C_Q_RANK)))), rows(c_kv_norm),
               _pack_w_uq(c_w_uq), _pack_w_ukv(c_w_ukv)]
    w_out_packed = _pack_w_out(w_out)
    sink = b_sink.astype(F32)
    tabs = _rope_tables(x.shape[1])
    e_mat = _group_sum_matrix()
    prev_out = None
    for l in range(depth):
        outs = _proj_call(x, l, weights, e_mat, tabs, prev_out=prev_out)
        if prev_out is not None:
            x = outs[-1]
        qab, ka, va, kb, vb, sg, qc, kc, vc = outs[:9]
        oa = _stack_call("attn_a", qab, ka, va, sg, 0, ATTN_A_TQ)
        oc = _attn_c_call(qc, kc, vc, sg, 2 * N_SLABS)
        if l == depth - 1:
            return _stack_call("attn_b_out", qab, kb, vb, sg, 1, ATTN_B_FINAL_TQ, sink=sink, layer=l,
                               final=(x, oa, oc, w_out_packed, final_g.astype(F32)[None, :]))
        ob = _stack_call("attn_b", qab, kb, vb, sg, 1, ATTN_B_TQ, sink=sink, layer=l)
        prev_out = (oa, ob, oc, w_out_packed)
```

```python
import functools

import numpy as np
import jax
import jax.numpy as jnp
from jax import lax
from jax.experimental import pallas as pl
from jax.experimental.pallas import tpu as pltpu

F32 = jnp.float32
BF16 = jnp.bfloat16

GRID_W = 64
ROPE_THETA = 10000.0
EPS = 1e-6
HEAD_DIM = 64
WINDOW = 128
AB_HEADS = 6
C_HEADS = 4
C_NOPE = 64
C_ROPE = 32
C_V = 64
C_Q_RANK = 192
C_KV_RANK = 128

LANES = 128
HALF = LANES // 2
QUARTER = LANES // 4
N_SLABS = AB_HEADS // 2
AB_W = AB_HEADS * HEAD_DIM
C_Q_PAD = 256
NEG = -1e30
LOG2E = 1.4426950408889634

IN_AQ, IN_AK, IN_AV, IN_AG = 0, 384, 512, 640
IN_BQ, IN_BK, IN_BV, IN_BG = 1024, 1408, 1536, 1664
IN_CQ, IN_CKV, IN_CKR, IN_CG = 2048, 2240, 2368, 2400
IN_AB_W = 2048

AB_QA, AB_KA, AB_QB, AB_KB, AB_VA, AB_VB, AB_GA, AB_GB = 0, 384, 512, 896, 1024, 1152, 1280, 1664
C_CQ, C_CKV, C_CKR, C_G, PACKED_C_W = 0, 256, 384, 512, 768
MIX_W = 1024

PACK_TM = 512
PROJ_TM = 512
ATTN_A_TQ = 512
ATTN_A_SUB = 128
ATTN_B_TQ = 1024
ATTN_B_FINAL_TQ = 512
ATTN_C_TQ = 1024
VMEM_LIMIT = 56 * 1024 * 1024
ATTN_A_VMEM_LIMIT = 60 * 1024 * 1024


_LANE = np.arange(LANES)
_HEAD_OF_LANE = (_LANE // QUARTER) % 2


def _dim_of_lane(first_half_dims, partner):
    f = np.asarray(first_half_dims)
    return np.where(_LANE < HALF, f[_LANE % QUARTER], f[_LANE % QUARTER] + partner)


_DIM_A = _dim_of_lane(list(range(16)) + list(range(32, 48)), 16)
_DIM_B = _dim_of_lane(list(range(32)), 32)

_C_ROPE_OF_LANE = np.full(LANES, -1)
_C_ROPE_OF_LANE[0:16] = np.arange(16)
_C_ROPE_OF_LANE[64:80] = 16 + np.arange(16)


def _permutation(src_of_dst):
    n = len(src_of_dst)
    p = np.zeros((n, n), np.float32)
    p[src_of_dst, np.arange(n)] = 1.0
    return jnp.asarray(p, BF16)


def _slab_sources(dim_of_lane, n_slabs):
    return np.concatenate([np.where(_HEAD_OF_LANE == 0, p, p + n_slabs) * HEAD_DIM + dim_of_lane
                           for p in range(n_slabs)])


def _pair_sources():
    d = np.arange(HEAD_DIM)
    return np.concatenate([np.concatenate([p * HEAD_DIM + d, (p + N_SLABS) * HEAD_DIM + d]) for p in range(N_SLABS)])


def _c_slab(x, rope):
    lead = x.shape[:-1]
    zeros = lambda n: jnp.zeros(lead + (n,), x.dtype)
    r0, r1 = (rope[..., :16], rope[..., 16:]) if rope is not None else (zeros(16), zeros(16))
    return jnp.concatenate([r0, x[..., :48], r1, x[..., 48:C_NOPE], zeros(LANES - C_NOPE - C_ROPE)], axis=-1)


def _pack_w_c(w):
    zeros = lambda n: jnp.zeros(w.shape[:-1] + (n,), w.dtype)
    ckr = w[..., IN_CKR:IN_CKR + C_ROPE]
    return jnp.concatenate([
        w[..., IN_CQ:IN_CQ + C_Q_RANK], zeros(C_Q_PAD - C_Q_RANK),
        w[..., IN_CKV:IN_CKV + C_KV_RANK],
        ckr[..., :16], zeros(48), ckr[..., 16:], zeros(48),
        w[..., IN_CG:IN_CG + C_HEADS * C_V]], axis=-1)


def _pack_w_uq(w):
    n = w.shape[0]
    w = w.astype(BF16).reshape(n, C_Q_RANK, C_HEADS, C_NOPE + C_ROPE)
    slab = _c_slab(w[..., :C_NOPE], w[..., C_NOPE:]).reshape(n, C_Q_RANK, C_HEADS * LANES)
    return jnp.pad(slab, ((0, 0), (0, C_Q_PAD - C_Q_RANK), (0, 0)))


def _pack_w_ukv(w):
    n = w.shape[0]
    w = w.astype(BF16).reshape(n, C_KV_RANK, C_HEADS, C_NOPE + C_V)
    k = _c_slab(w[..., :C_NOPE], None).reshape(n, C_KV_RANK, C_HEADS * LANES)
    v = w[..., C_NOPE:].reshape(n, C_KV_RANK, C_HEADS * C_V)
    return jnp.concatenate([k, v], axis=-1)


def _pack_w_out(w):
    n, _, dd = w.shape
    pair = lambda r: r.reshape(n, 2, N_SLABS, HEAD_DIM, dd).transpose(0, 2, 1, 3, 4).reshape(n, AB_W, dd)
    return jnp.concatenate([pair(w[:, :AB_W]), pair(w[:, AB_W:2 * AB_W]), w[:, 2 * AB_W:]], axis=1).astype(BF16)


def _group_sum_matrix():
    head = np.concatenate([_HEAD_OF_LANE, 2 + _HEAD_OF_LANE])
    return jnp.asarray(head[:, None] == head[None, :], BF16)


def _rope_tables(s_len):
    t = np.arange(s_len, dtype=np.float64)[:, None]
    sign = np.where(_LANE < HALF, -1.0, 1.0)[None, :]

    def tables(pos, freq, dim, active=None):
        ang = pos * (ROPE_THETA ** (-2.0 * freq / dim))[None, :]
        cos, sin = np.cos(ang), np.sin(ang) * sign
        if active is not None:
            cos, sin = np.where(active[None, :], cos, 1.0), np.where(active[None, :], sin, 0.0)
        return [cos, sin]

    pos_a = np.where((_DIM_A < HEAD_DIM // 2)[None, :], t // GRID_W, t % GRID_W)
    rope_c = _C_ROPE_OF_LANE >= 0
    tabs = (tables(pos_a, _DIM_A % 16, HEAD_DIM // 2) + tables(t, _DIM_B % 32, HEAD_DIM)
            + tables(t, np.maximum(_C_ROPE_OF_LANE, 0) % 16, C_ROPE, rope_c))
    return jnp.asarray(np.stack(tabs).astype(np.float32))


def _pack_body(w_ref, pqa_ref, pka_ref, pqb_ref, pkb_ref, pg_ref, o_ref):
    def move(src, dst, width, perm_ref):
        x = w_ref[:, src:src + width]
        if perm_ref is not None:
            x = jnp.dot(x, perm_ref[...], preferred_element_type=F32).astype(BF16)
        o_ref[:, dst:dst + width] = x

    move(IN_AQ, AB_QA, AB_W, pqa_ref)
    move(IN_AK, AB_KA, LANES, pka_ref)
    move(IN_BQ, AB_QB, AB_W, pqb_ref)
    move(IN_BK, AB_KB, LANES, pkb_ref)
    move(IN_AV, AB_VA, LANES, None)
    move(IN_BV, AB_VB, LANES, None)
    move(IN_AG, AB_GA, AB_W, pg_ref)
    move(IN_BG, AB_GB, AB_W, pg_ref)


def _pack_w_ab(w_in):
    depth, d, _ = w_in.shape
    tm = min(PACK_TM, d)
    perms = [_permutation(_slab_sources(_DIM_A, N_SLABS)), _permutation(_slab_sources(_DIM_A, 1)),
             _permutation(_slab_sources(_DIM_B, N_SLABS)), _permutation(_slab_sources(_DIM_B, 1)),
             _permutation(_pair_sources())]
    return pl.pallas_call(
        _pack_body,
        grid=(depth, d // tm),
        in_specs=[pl.BlockSpec((None, tm, IN_AB_W), lambda l, r: (l, r, 0))]
                 + [pl.BlockSpec(p.shape, lambda l, r: (0, 0)) for p in perms],
        out_specs=pl.BlockSpec((None, tm, IN_AB_W), lambda l, r: (l, r, 0)),
        out_shape=jax.ShapeDtypeStruct((depth, d, IN_AB_W), BF16),
        compiler_params=pltpu.CompilerParams(
            dimension_semantics=("arbitrary", "arbitrary"), vmem_limit_bytes=VMEM_LIMIT),
        name="pack",
    )(w_in, *perms)


def _proj_body(x_ref, *refs):
    _proj_compute(x_ref[...], *refs)


def _out_proj_body(x_ref, oa_ref, ob_ref, oc_ref, wo_ref, *refs):
    proj_refs, x_out_ref = refs[:-1], refs[-1]
    o = jnp.concatenate([oa_ref[...], ob_ref[...], oc_ref[...]], axis=1)
    x = x_ref[...] + jnp.dot(o, wo_ref[...], preferred_element_type=F32)
    x_out_ref[...] = x
    _proj_compute(x, *proj_refs)


def _proj_compute(x, g_ref, wc_ref, wab_ref, aqn_ref, akn_ref, cqn_ref, ckvn_ref, wuq_ref, wukv_ref, e_ref, tab_ref,
                  qab_ref, ka_ref, va_ref, kb_ref, vb_ref, sg_ref, qc_ref, kc_ref, vc_ref):
    h = (x * lax.rsqrt(jnp.mean(x * x, axis=-1, keepdims=True) + EPS) * g_ref[...]).astype(BF16)
    zc = jnp.dot(h, wc_ref[...], preferred_element_type=F32)
    zab = jnp.dot(h, wab_ref[...], preferred_element_type=F32)

    def rope(xs, mixer):
        return xs * tab_ref[2 * mixer] + pltpu.roll(xs, HALF, 1) * tab_ref[2 * mixer + 1]

    def slab(z, off):
        return z[:, off:off + LANES]

    def head_sumsq(s0, s1):
        t = jnp.concatenate([s0 * s0, s1 * s1], axis=1).astype(BF16)
        ss = jnp.dot(t, e_ref[...], preferred_element_type=F32)
        return ss[:, :LANES], ss[:, LANES:]

    def head_norm(xs, ss, gain):
        return xs * lax.rsqrt(ss * (1.0 / HEAD_DIM) + EPS) * gain

    def silu(gz):
        return (gz * (0.5 * jnp.tanh(0.5 * gz) + 0.5)).astype(BF16)

    cq = zc[:, C_CQ:C_CQ + C_Q_PAD]
    cq = cq * lax.rsqrt(jnp.sum(cq * cq, axis=-1, keepdims=True) * (1.0 / C_Q_RANK) + EPS) * cqn_ref[...]
    qc = jnp.dot(cq.astype(BF16), wuq_ref[...], preferred_element_type=F32)
    scale_c = (C_NOPE + C_ROPE) ** -0.5 * LOG2E
    for hh in range(C_HEADS):
        q = rope(qc[:, hh * LANES:(hh + 1) * LANES], 2)
        qc_ref[:, hh * LANES:(hh + 1) * LANES] = (q * scale_c).astype(BF16)
    ckv = zc[:, C_CKV:C_CKV + C_KV_RANK]
    ckv = ckv * lax.rsqrt(jnp.mean(ckv * ckv, axis=-1, keepdims=True) + EPS) * ckvn_ref[...]
    kv = jnp.dot(ckv.astype(BF16), wukv_ref[...], preferred_element_type=F32)
    kr = rope(slab(zc, C_CKR), 2)
    for hh in range(C_HEADS):
        kc_ref[hh] = (kv[:, hh * LANES:(hh + 1) * LANES] + kr).astype(BF16)
    ones = jnp.ones((x.shape[0], LANES), BF16)
    for p in range(C_HEADS // 2):
        v_pair = kv[:, (C_HEADS + p) * LANES:(C_HEADS + p + 1) * LANES].astype(BF16)
        vc_ref[:, 2 * p * LANES:(2 * p + 2) * LANES] = jnp.concatenate([v_pair, ones], axis=1)
    sg_ref[:, 2 * AB_W:] = silu(zc[:, C_G:])

    scale_ab = HEAD_DIM ** -0.5 * LOG2E
    qa = [slab(zab, AB_QA + j * LANES) for j in range(N_SLABS)]
    ka = slab(zab, AB_KA)
    ss0, ss1 = head_sumsq(qa[0], qa[1])
    ss2, ssk = head_sumsq(qa[2], ka)
    for j, ss in enumerate((ss0, ss1, ss2)):
        q = rope(head_norm(qa[j], ss, aqn_ref[...]), 0)
        qab_ref[:, j * LANES:(j + 1) * LANES] = (q * scale_ab).astype(BF16)
    ka_ref[...] = rope(head_norm(ka, ssk, akn_ref[...]), 0).astype(BF16)
    for j in range(N_SLABS):
        q = rope(slab(zab, AB_QB + j * LANES), 1)
        qab_ref[:, (N_SLABS + j) * LANES:(N_SLABS + j + 1) * LANES] = (q * scale_ab).astype(BF16)
    kb_ref[...] = rope(slab(zab, AB_KB), 1).astype(BF16)
    va_ref[...] = jnp.concatenate([slab(zab, AB_VA).astype(BF16), ones], axis=1)
    vb_ref[...] = jnp.concatenate([slab(zab, AB_VB).astype(BF16), ones], axis=1)
    sg_ref[:, :2 * AB_W] = silu(zab[:, AB_GA:AB_GA + 2 * AB_W])


def _layer_spec(a, layer):
    return pl.BlockSpec((None,) + a.shape[1:], lambda *_: (layer,) + (0,) * (a.ndim - 1))


def _proj_call(x, layer, weights, e_mat, tabs, prev_out=None):
    bsz, s_len, d = x.shape
    tm = min(PROJ_TM, s_len)
    grid = (s_len // tm, bsz)
    tok = lambda w: pl.BlockSpec((None, tm, w), lambda s, b: (b, s, 0))
    bf = lambda *shape: jax.ShapeDtypeStruct(shape, BF16)
    widths = [2 * N_SLABS * LANES, LANES, 2 * LANES, LANES, 2 * LANES, MIX_W, C_HEADS * LANES]
    in_specs = [tok(d)]
    args = [x]
    out_specs = ([tok(w) for w in widths]
                 + [pl.BlockSpec((None, C_HEADS, tm, LANES), lambda s, b: (b, 0, s, 0)), tok(C_HEADS * LANES)])
    out_shape = ([bf(bsz, s_len, w) for w in widths]
                 + [bf(bsz, C_HEADS, s_len, LANES), bf(bsz, s_len, C_HEADS * LANES)])
    body = _proj_body
    if prev_out is not None:
        oa, ob, oc, w_out = prev_out
        in_specs += [tok(oa.shape[-1]), tok(ob.shape[-1]), tok(oc.shape[-1]), _layer_spec(w_out, layer - 1)]
        args += [oa, ob, oc, w_out]
        out_specs.append(tok(d))
        out_shape.append(jax.ShapeDtypeStruct(x.shape, F32))
        body = _out_proj_body
    return pl.pallas_call(
        body,
        grid=grid,
        in_specs=in_specs + [_layer_spec(a, layer) for a in weights]
                 + [pl.BlockSpec(e_mat.shape, lambda s, b: (0, 0)),
                    pl.BlockSpec((6, tm, LANES), lambda s, b: (0, s, 0))],
        out_specs=out_specs,
        out_shape=out_shape,
        compiler_params=pltpu.CompilerParams(
            dimension_semantics=("arbitrary", "arbitrary"), vmem_limit_bytes=VMEM_LIMIT),
        name="proj" if prev_out is None else "out_proj",
    )(*args, *weights, e_mat, tabs)


def _divmod(t, n):
    if n & (n - 1) == 0:
        return lax.shift_right_logical(t, n.bit_length() - 1), t & (n - 1)
    return lax.div(t, n), lax.rem(t, n)


def _scores(q, k):
    return lax.dot_general(q, k, (((1,), (1,)), ((), ())), preferred_element_type=F32)


def _define_first_slot(t, s_b, m_b):
    @pl.when(t == 0)
    def _():
        s_b[...] = jnp.zeros(s_b.shape, F32)
        m_b[...] = jnp.zeros(m_b.shape, F32)


def _run_parity(t, step, s_a, m_a, s_b, m_b):
    parity = t & 1

    @pl.when(parity == 0)
    def _():
        step(s_a, m_a, s_b, m_b)

    @pl.when(parity == 1)
    def _():
        step(s_b, m_b, s_a, m_a)


def _softmax_numerators(s_r, m_r, rows, width):
    m = m_r[rows, :]
    return jnp.concatenate(
        [jnp.exp2(s_r[rows, j * LANES:(j + 1) * LANES] - m).astype(BF16) for j in range(width // LANES)], axis=1)


def _stack_queries(q, head_a):
    slabs = [q[:, j * LANES:(j + 1) * LANES] for j in range(N_SLABS)]
    return jnp.concatenate([jnp.where(head_a, s, 0.0) for s in slabs]
                           + [jnp.where(head_a, 0.0, s) for s in slabs], axis=0).astype(BF16)


def _stack_pipe_body(q_ref, k_ref, v_ref, sg_ref, o_ref, s_a, s_b, m_a, m_b):
    t = pl.program_id(0)
    tq = q_ref.shape[0]
    s_len = k_ref.shape[0]
    _define_first_slot(t, s_b, m_b)

    sub = min(ATTN_A_SUB, tq)
    lane = lax.broadcasted_iota(jnp.int32, (sub, LANES), 1)
    out_low = lane < HALF

    def step(s_w, m_w, s_r, m_r):
        q = q_ref[...].astype(F32)
        for u in range(tq // sub):
            s = _scores(_stack_queries(q[u * sub:(u + 1) * sub], (lane & QUARTER) == 0), k_ref[...])
            for i in range(AB_HEADS):
                blk = s[i * sub:(i + 1) * sub]
                rows = pl.ds((u * AB_HEADS + i) * sub, sub)
                s_w[rows, :] = blk
                m_w[rows, :] = jnp.broadcast_to(jnp.max(blk, axis=1, keepdims=True), (sub, LANES))
        for u in range(tq // sub):
            rows = [pl.ds((u * AB_HEADS + i) * sub, sub) for i in range(AB_HEADS)]
            p = jnp.concatenate([_softmax_numerators(s_r, m_r, rows[i], s_len) for i in range(AB_HEADS)], axis=0)
            a = jnp.dot(p, v_ref[...], preferred_element_type=F32)
            for j in range(N_SLABS):
                lo, hi = a[j * sub:(j + 1) * sub], a[(j + N_SLABS) * sub:(j + N_SLABS + 1) * sub]
                o = jnp.where(out_low, lo[:, :LANES] / lo[:, LANES:], hi[:, :LANES] / hi[:, LANES:])
                cols = slice(j * LANES, (j + 1) * LANES)
                gate = sg_ref[u * sub:(u + 1) * sub, cols].astype(F32)
                o_ref[u * sub:(u + 1) * sub, cols] = (o * gate).astype(BF16)

    _run_parity(t, step, s_a, m_a, s_b, m_b)


def _window_pipe_body(sink_ref, q_ref, k_ref, v_ref, sg_ref, *refs, n_q, n_items, layer, final):
    if final:
        x_ref, oa_ref, oc_ref, wo_ref, fg_ref, y_ref, s_a, s_b, m_a, m_b, o_ref = refs
    else:
        o_ref, s_a, s_b, m_a, m_b = refs
    t = pl.program_id(0)
    tq = q_ref.shape[0]
    s_len = k_ref.shape[0]
    sub = WINDOW
    kw = 3 * WINDOW
    n_sub = tq // sub
    t_cur = jnp.minimum(t, n_items - 1)
    t_prev = jnp.maximum(t - 1, 0)
    _define_first_slot(t, s_b, m_b)

    lane = lax.broadcasted_iota(jnp.int32, (sub, LANES), 1)
    row = lax.broadcasted_iota(jnp.int32, (sub, LANES), 0)
    head_a = (lane & QUARTER) == 0
    out_low = lane < HALF
    row0_cur = _divmod(t_cur, n_q)[1] * tq
    row0_prev = _divmod(t_prev, n_q)[1] * tq
    sinks = [sink_ref[layer, i] * LOG2E for i in range(AB_HEADS)]

    def win_start(r0):
        return pl.multiple_of(jnp.clip(r0 - WINDOW, 0, s_len - kw), WINDOW)

    def step(s_w, m_w, s_r, m_r):
        q = q_ref[...].astype(F32)
        for u in range(n_sub):
            r0 = row0_cur + u * sub
            ws = win_start(r0)
            dist = (r0 - ws) + row - lane
            bias = [jnp.where(jnp.abs(dist - c * LANES) <= WINDOW, 0.0, NEG) for c in range(3)]
            s = _scores(_stack_queries(q[u * sub:(u + 1) * sub], head_a), k_ref[pl.ds(ws, kw), :])
            for i in range(AB_HEADS):
                blk = s[i * sub:(i + 1) * sub]
                cols = [blk[:, c * LANES:(c + 1) * LANES] + bias[c] for c in range(3)]
                mx = jnp.max(jnp.maximum(jnp.maximum(cols[0], cols[1]), cols[2]), axis=1, keepdims=True)
                rows = pl.ds((u * AB_HEADS + i) * sub, sub)
                for c in range(3):
                    s_w[rows, c * LANES:(c + 1) * LANES] = cols[c]
                m_w[rows, :] = jnp.broadcast_to(jnp.maximum(mx, sinks[i]), (sub, LANES))
        for u in range(n_sub):
            v_win = v_ref[pl.ds(win_start(row0_prev + u * sub), kw), :]
            rows = [pl.ds((u * AB_HEADS + i) * sub, sub) for i in range(AB_HEADS)]
            p = jnp.concatenate([_softmax_numerators(s_r, m_r, rows[i], kw) for i in range(AB_HEADS)], axis=0)
            a = jnp.dot(p, v_win, preferred_element_type=F32)
            for j in range(N_SLABS):
                parts = []
                for i in (j, j + N_SLABS):
                    blk = a[i * sub:(i + 1) * sub]
                    den = blk[:, LANES:] + jnp.exp2(sinks[i] - m_r[rows[i], :])
                    parts.append(blk[:, :LANES] / den)
                cols = slice(j * LANES, (j + 1) * LANES)
                gate = sg_ref[u * sub:(u + 1) * sub, cols].astype(F32)
                o_ref[u * sub:(u + 1) * sub, cols] = (jnp.where(out_low, parts[0], parts[1]) * gate).astype(BF16)
        if final:
            o = jnp.concatenate([oa_ref[...], o_ref[...], oc_ref[...]], axis=1)
            r = x_ref[...] + jnp.dot(o, wo_ref[...], preferred_element_type=F32)
            y_ref[...] = r * lax.rsqrt(jnp.mean(r * r, axis=-1, keepdims=True) + EPS) * fg_ref[...]

    _run_parity(t, step, s_a, m_a, s_b, m_b)


def _stack_call(name, qab, k, v, sg, blk, tq, sink=None, layer=0, final=None):
    bsz, s_len, _ = qab.shape
    tq = min(tq, s_len)
    n_q = s_len // tq
    n_items = bsz * n_q
    windowed = sink is not None
    cur = lambda t: _divmod(jnp.minimum(t, n_items - 1), n_q)
    prev = lambda t: _divmod(jnp.maximum(t - 1, 0), n_q)
    in_specs = [
        pl.BlockSpec((None, tq, AB_W), lambda t: (*cur(t), blk)),
        pl.BlockSpec((None, s_len, LANES), lambda t: (cur(t)[0], 0, 0)),
        pl.BlockSpec((None, s_len, 2 * LANES), lambda t: (prev(t)[0], 0, 0)),
        pl.BlockSpec((None, tq, AB_W), lambda t: (*prev(t), blk)),
    ]
    args = [qab, k, v, sg]
    rows = AB_HEADS * tq
    width = s_len
    body = _stack_pipe_body
    if windowed:
        in_specs = [pl.BlockSpec(memory_space=pltpu.SMEM)] + in_specs
        args = [sink] + args
        width = 3 * WINDOW
        body = functools.partial(_window_pipe_body, n_q=n_q, n_items=n_items, layer=layer, final=final is not None)
    scratch = [pltpu.VMEM((rows, width), F32), pltpu.VMEM((rows, width), F32),
               pltpu.VMEM((rows, LANES), F32), pltpu.VMEM((rows, LANES), F32)]
    out_specs = pl.BlockSpec((None, tq, AB_W), lambda t: (*prev(t), 0))
    out_shape = jax.ShapeDtypeStruct((bsz, s_len, AB_W), BF16)
    if final is not None:
        x, oa, oc, w_out, fg = final
        tok = lambda a: pl.BlockSpec((None, tq, a.shape[-1]), lambda t: (*prev(t), 0))
        in_specs += [tok(x), tok(oa), tok(oc), _layer_spec(w_out, layer), pl.BlockSpec(fg.shape, lambda t: (0, 0))]
        args += [x, oa, oc, w_out, fg]
        out_specs, out_shape = tok(x), jax.ShapeDtypeStruct(x.shape, F32)
        scratch.append(pltpu.VMEM((tq, AB_W), BF16))
    return pl.pallas_call(
        body,
        grid=(n_items + 1,),
        in_specs=in_specs,
        out_specs=out_specs,
        out_shape=out_shape,
        scratch_shapes=scratch,
        compiler_params=pltpu.CompilerParams(
            dimension_semantics=("arbitrary",),
            vmem_limit_bytes=VMEM_LIMIT if windowed else ATTN_A_VMEM_LIMIT),
        name=name,
    )(*args)


def _pair_pipe_body(q0_ref, q1_ref, k0_ref, k1_ref, v_ref, sg_ref, o_ref, s_a, s_b, m_a, m_b):
    t = pl.program_id(0)
    tq, s_len = s_a.shape[1], s_a.shape[2]
    _define_first_slot(t, s_b, m_b)
    out_low = lax.broadcasted_iota(jnp.int32, (tq, LANES), 1) < HALF
    qs, ks = (q0_ref, q1_ref), (k0_ref, k1_ref)

    def step(s_w, m_w, s_r, m_r):
        for hd in range(2):
            s = _scores(qs[hd][...], ks[hd][...])
            s_w[hd] = s
            m_w[hd] = jnp.broadcast_to(jnp.max(s, axis=1, keepdims=True), (tq, LANES))
        outs = []
        for hd in range(2):
            p = _softmax_numerators(s_r.at[hd], m_r.at[hd], slice(None), s_len)
            a = jnp.dot(p, v_ref[...], preferred_element_type=F32)
            outs.append(a[:, :LANES] / a[:, LANES:])
        o_ref[...] = (jnp.where(out_low, outs[0], outs[1]) * sg_ref[...].astype(F32)).astype(BF16)

    _run_parity(t, step, s_a, m_a, s_b, m_b)


def _attn_c_call(qc, kc, vc, sg, g_off):
    bsz, s_len, _ = qc.shape
    tq = min(ATTN_C_TQ, s_len)
    n_q = s_len // tq
    n_pairs = C_HEADS // 2
    n_items = bsz * n_pairs * n_q

    def item(t):
        bp, i = _divmod(t, n_q)
        b, p = _divmod(bp, n_pairs)
        return b, p, i

    cur = lambda t: item(jnp.minimum(t, n_items - 1))
    prev = lambda t: item(jnp.maximum(t - 1, 0))

    def q_spec(o):
        def imap(t):
            b, p, i = cur(t)
            return b, i, 2 * p + o
        return pl.BlockSpec((None, tq, LANES), imap)

    def k_spec(o):
        def imap(t):
            b, p, i = cur(t)
            return b, 2 * p + o, 0, 0
        return pl.BlockSpec((None, None, s_len, LANES), imap)

    def v_map(t):
        b, p, i = prev(t)
        return b, 0, p

    def sg_map(t):
        b, p, i = prev(t)
        return b, i, g_off + p

    def o_map(t):
        b, p, i = prev(t)
        return b, i, p

    return pl.pallas_call(
        _pair_pipe_body,
        grid=(n_items + 1,),
        in_specs=[q_spec(0), q_spec(1), k_spec(0), k_spec(1),
                  pl.BlockSpec((None, s_len, 2 * LANES), v_map), pl.BlockSpec((None, tq, LANES), sg_map)],
        out_specs=pl.BlockSpec((None, tq, LANES), o_map),
        out_shape=jax.ShapeDtypeStruct((bsz, s_len, n_pairs * LANES), BF16),
        scratch_shapes=[pltpu.VMEM((2, tq, s_len), F32), pltpu.VMEM((2, tq, s_len), F32),
                        pltpu.VMEM((2, tq, LANES), F32), pltpu.VMEM((2, tq, LANES), F32)],
        compiler_params=pltpu.CompilerParams(dimension_semantics=("arbitrary",), vmem_limit_bytes=VMEM_LIMIT),
        name="attn_c",
    )(qc, qc, kc, kc, vc, sg)


def kernel(x, norm_g, w_in, a_q_norm, a_k_norm, b_sink, c_q_norm, c_kv_norm, c_w_uq, c_w_ukv, w_out, final_g):
    depth = w_in.shape[0]
    rows = lambda v: v.astype(F32)[:, None, :]
    w_in = w_in.astype(BF16)
    weights = [rows(norm_g), _pack_w_c(w_in), _pack_w_ab(w_in),
               rows(a_q_norm[:, _DIM_A]), rows(a_k_norm[:, _DIM_A]),
               rows(jnp.pad(c_q_norm, ((0, 0), (0, C_Q_PAD - C_Q_RANK)))), rows(c_kv_norm),
               _pack_w_uq(c_w_uq), _pack_w_ukv(c_w_ukv)]
    w_out_packed = _pack_w_out(w_out)
    sink = b_sink.astype(F32)
    tabs = _rope_tables(x.shape[1])
    e_mat = _group_sum_matrix()
    prev_out = None
    for l in range(depth):
        outs = _proj_call(x, l, weights, e_mat, tabs, prev_out=prev_out)
        if prev_out is not None:
            x = outs[-1]
        qab, ka, va, kb, vb, sg, qc, kc, vc = outs[:9]
        oa = _stack_call("attn_a", qab, ka, va, sg, 0, ATTN_A_TQ)
        oc = _attn_c_call(qc, kc, vc, sg, 2 * N_SLABS)
        if l == depth - 1:
            return _stack_call("attn_b_out", qab, kb, vb, sg, 1, ATTN_B_FINAL_TQ, sink=sink, layer=l,
                               final=(x, oa, oc, w_out_packed, final_g.astype(F32)[None, :]))
        ob = _stack_call("attn_b", qab, kb, vb, sg, 1, ATTN_B_TQ, sink=sink, layer=l)
        prev_out = (oa, ob, oc, w_out_packed)
```

```python
import functools

import numpy as np
import jax
import jax.numpy as jnp
from jax import lax
from jax.experimental import pallas as pl
from jax.experimental.pallas import tpu as pltpu

F32 = jnp.float32
BF16 = jnp.bfloat16

GRID_W = 64
ROPE_THETA = 10000.0
EPS = 1e-6
HEAD_DIM = 64
WINDOW = 128
AB_HEADS = 6
C_HEADS = 4
C_NOPE = 64
C_ROPE = 32
C_V = 64
C_Q_RANK = 192
C_KV_RANK = 128

LANES = 128
HALF = LANES // 2
QUARTER = LANES // 4
N_SLABS = AB_HEADS // 2
AB_W = AB_HEADS * HEAD_DIM
C_Q_PAD = 256
NEG = -1e30
LOG2E = 1.4426950408889634

IN_AQ, IN_AK, IN_AV, IN_AG = 0, 384, 512, 640
IN_BQ, IN_BK, IN_BV, IN_BG = 1024, 1408, 1536, 1664
IN_CQ, IN_CKV, IN_CKR, IN_CG = 2048, 2240, 2368, 2400
IN_AB_W = 2048

AB_QA, AB_KA, AB_QB, AB_KB, AB_VA, AB_VB, AB_GA, AB_GB = 0, 384, 512, 896, 1024, 1152, 1280, 1664
C_CQ, C_CKV, C_CKR, C_G, PACKED_C_W = 0, 256, 384, 512, 768
MIX_W = 1024

PACK_TM = 512
PROJ_TM = 1024
ATTN_A_TQ = 512
ATTN_A_SUB = 128
ATTN_B_TQ = 1024
ATTN_B_FINAL_TQ = 1024
ATTN_C_TQ = 1024
VMEM_LIMIT = 56 * 1024 * 1024
ATTN_A_VMEM_LIMIT = 60 * 1024 * 1024


_LANE = np.arange(LANES)
_HEAD_OF_LANE = (_LANE // QUARTER) % 2


def _dim_of_lane(first_half_dims, partner):
    f = np.asarray(first_half_dims)
    return np.where(_LANE < HALF, f[_LANE % QUARTER], f[_LANE % QUARTER] + partner)


_DIM_A = _dim_of_lane(list(range(16)) + list(range(32, 48)), 16)
_DIM_B = _dim_of_lane(list(range(32)), 32)

_C_ROPE_OF_LANE = np.full(LANES, -1)
_C_ROPE_OF_LANE[0:16] = np.arange(16)
_C_ROPE_OF_LANE[64:80] = 16 + np.arange(16)


def _permutation(src_of_dst):
    n = len(src_of_dst)
    p = np.zeros((n, n), np.float32)
    p[src_of_dst, np.arange(n)] = 1.0
    return jnp.asarray(p, BF16)


def _slab_sources(dim_of_lane, n_slabs):
    return np.concatenate([np.where(_HEAD_OF_LANE == 0, p, p + n_slabs) * HEAD_DIM + dim_of_lane
                           for p in range(n_slabs)])


def _pair_sources():
    d = np.arange(HEAD_DIM)
    return np.concatenate([np.concatenate([p * HEAD_DIM + d, (p + N_SLABS) * HEAD_DIM + d]) for p in range(N_SLABS)])


def _c_slab(x, rope):
    lead = x.shape[:-1]
    zeros = lambda n: jnp.zeros(lead + (n,), x.dtype)
    r0, r1 = (rope[..., :16], rope[..., 16:]) if rope is not None else (zeros(16), zeros(16))
    return jnp.concatenate([r0, x[..., :48], r1, x[..., 48:C_NOPE], zeros(LANES - C_NOPE - C_ROPE)], axis=-1)


def _pack_w_c(w):
    zeros = lambda n: jnp.zeros(w.shape[:-1] + (n,), w.dtype)
    ckr = w[..., IN_CKR:IN_CKR + C_ROPE]
    return jnp.concatenate([
        w[..., IN_CQ:IN_CQ + C_Q_RANK], zeros(C_Q_PAD - C_Q_RANK),
        w[..., IN_CKV:IN_CKV + C_KV_RANK],
        ckr[..., :16], zeros(48), ckr[..., 16:], zeros(48),
        w[..., IN_CG:IN_CG + C_HEADS * C_V]], axis=-1)


def _pack_w_uq(w):
    n = w.shape[0]
    w = w.astype(BF16).reshape(n, C_Q_RANK, C_HEADS, C_NOPE + C_ROPE)
    slab = _c_slab(w[..., :C_NOPE], w[..., C_NOPE:]).reshape(n, C_Q_RANK, C_HEADS * LANES)
    return jnp.pad(slab, ((0, 0), (0, C_Q_PAD - C_Q_RANK), (0, 0)))


def _pack_w_ukv(w):
    n = w.shape[0]
    w = w.astype(BF16).reshape(n, C_KV_RANK, C_HEADS, C_NOPE + C_V)
    k = _c_slab(w[..., :C_NOPE], None).reshape(n, C_KV_RANK, C_HEADS * LANES)
    v = w[..., C_NOPE:].reshape(n, C_KV_RANK, C_HEADS * C_V)
    return jnp.concatenate([k, v], axis=-1)


def _pack_w_out(w):
    n, _, dd = w.shape
    pair = lambda r: r.reshape(n, 2, N_SLABS, HEAD_DIM, dd).transpose(0, 2, 1, 3, 4).reshape(n, AB_W, dd)
    return jnp.concatenate([pair(w[:, :AB_W]), pair(w[:, AB_W:2 * AB_W]), w[:, 2 * AB_W:]], axis=1).astype(BF16)


def _group_sum_matrix():
    head = np.concatenate([_HEAD_OF_LANE, 2 + _HEAD_OF_LANE])
    return jnp.asarray(head[:, None] == head[None, :], BF16)


def _rope_tables(s_len):
    t = np.arange(s_len, dtype=np.float64)[:, None]
    sign = np.where(_LANE < HALF, -1.0, 1.0)[None, :]

    def tables(pos, freq, dim, active=None):
        ang = pos * (ROPE_THETA ** (-2.0 * freq / dim))[None, :]
        cos, sin = np.cos(ang), np.sin(ang) * sign
        if active is not None:
            cos, sin = np.where(active[None, :], cos, 1.0), np.where(active[None, :], sin, 0.0)
        return [cos, sin]

    pos_a = np.where((_DIM_A < HEAD_DIM // 2)[None, :], t // GRID_W, t % GRID_W)
    rope_c = _C_ROPE_OF_LANE >= 0
    tabs = (tables(pos_a, _DIM_A % 16, HEAD_DIM // 2) + tables(t, _DIM_B % 32, HEAD_DIM)
            + tables(t, np.maximum(_C_ROPE_OF_LANE, 0) % 16, C_ROPE, rope_c))
    return jnp.asarray(np.stack(tabs).astype(np.float32))


def _pack_body(w_ref, pqa_ref, pka_ref, pqb_ref, pkb_ref, pg_ref, o_ref):
    def move(src, dst, width, perm_ref):
        x = w_ref[:, src:src + width]
        if perm_ref is not None:
            x = jnp.dot(x, perm_ref[...], preferred_element_type=F32).astype(BF16)
        o_ref[:, dst:dst + width] = x

    move(IN_AQ, AB_QA, AB_W, pqa_ref)
    move(IN_AK, AB_KA, LANES, pka_ref)
    move(IN_BQ, AB_QB, AB_W, pqb_ref)
    move(IN_BK, AB_KB, LANES, pkb_ref)
    move(IN_AV, AB_VA, LANES, None)
    move(IN_BV, AB_VB, LANES, None)
    move(IN_AG, AB_GA, AB_W, pg_ref)
    move(IN_BG, AB_GB, AB_W, pg_ref)


def _pack_w_ab(w_in):
    depth, d, _ = w_in.shape
    tm = min(PACK_TM, d)
    perms = [_permutation(_slab_sources(_DIM_A, N_SLABS)), _permutation(_slab_sources(_DIM_A, 1)),
             _permutation(_slab_sources(_DIM_B, N_SLABS)), _permutation(_slab_sources(_DIM_B, 1)),
             _permutation(_pair_sources())]
    return pl.pallas_call(
        _pack_body,
        grid=(depth, d // tm),
        in_specs=[pl.BlockSpec((None, tm, IN_AB_W), lambda l, r: (l, r, 0))]
                 + [pl.BlockSpec(p.shape, lambda l, r: (0, 0)) for p in perms],
        out_specs=pl.BlockSpec((None, tm, IN_AB_W), lambda l, r: (l, r, 0)),
        out_shape=jax.ShapeDtypeStruct((depth, d, IN_AB_W), BF16),
        compiler_params=pltpu.CompilerParams(
            dimension_semantics=("arbitrary", "arbitrary"), vmem_limit_bytes=VMEM_LIMIT),
        name="pack",
    )(w_in, *perms)


def _proj_body(x_ref, *refs):
    _proj_compute(x_ref[...], *refs)


def _out_proj_body(x_ref, oa_ref, ob_ref, oc_ref, wo_ref, *refs):
    proj_refs, x_out_ref = refs[:-1], refs[-1]
    o = jnp.concatenate([oa_ref[...], ob_ref[...], oc_ref[...]], axis=1)
    x = x_ref[...] + jnp.dot(o, wo_ref[...], preferred_element_type=F32)
    x_out_ref[...] = x
    _proj_compute(x, *proj_refs)


def _proj_compute(x, g_ref, wc_ref, wab_ref, aqn_ref, akn_ref, cqn_ref, ckvn_ref, wuq_ref, wukv_ref, e_ref, tab_ref,
                  qab_ref, ka_ref, va_ref, kb_ref, vb_ref, sg_ref, qc_ref, kc_ref, vc_ref):
    h = (x * lax.rsqrt(jnp.mean(x * x, axis=-1, keepdims=True) + EPS) * g_ref[...]).astype(BF16)
    zc = jnp.dot(h, wc_ref[...], preferred_element_type=F32)
    zab = jnp.dot(h, wab_ref[...], preferred_element_type=F32)

    def rope(xs, mixer):
        return xs * tab_ref[2 * mixer] + pltpu.roll(xs, HALF, 1) * tab_ref[2 * mixer + 1]

    def slab(z, off):
        return z[:, off:off + LANES]

    def head_sumsq(s0, s1):
        t = jnp.concatenate([s0 * s0, s1 * s1], axis=1).astype(BF16)
        ss = jnp.dot(t, e_ref[...], preferred_element_type=F32)
        return ss[:, :LANES], ss[:, LANES:]

    def head_norm(xs, ss, gain):
        return xs * lax.rsqrt(ss * (1.0 / HEAD_DIM) + EPS) * gain

    def silu(gz):
        return (gz * (0.5 * jnp.tanh(0.5 * gz) + 0.5)).astype(BF16)

    cq = zc[:, C_CQ:C_CQ + C_Q_PAD]
    cq = cq * lax.rsqrt(jnp.sum(cq * cq, axis=-1, keepdims=True) * (1.0 / C_Q_RANK) + EPS) * cqn_ref[...]
    qc = jnp.dot(cq.astype(BF16), wuq_ref[...], preferred_element_type=F32)
    scale_c = (C_NOPE + C_ROPE) ** -0.5 * LOG2E
    for hh in range(C_HEADS):
        q = rope(qc[:, hh * LANES:(hh + 1) * LANES], 2)
        qc_ref[:, hh * LANES:(hh + 1) * LANES] = (q * scale_c).astype(BF16)
    ckv = zc[:, C_CKV:C_CKV + C_KV_RANK]
    ckv = ckv * lax.rsqrt(jnp.mean(ckv * ckv, axis=-1, keepdims=True) + EPS) * ckvn_ref[...]
    kv = jnp.dot(ckv.astype(BF16), wukv_ref[...], preferred_element_type=F32)
    kr = rope(slab(zc, C_CKR), 2)
    for hh in range(C_HEADS):
        kc_ref[hh] = (kv[:, hh * LANES:(hh + 1) * LANES] + kr).astype(BF16)
    ones = jnp.ones((x.shape[0], LANES), BF16)
    for p in range(C_HEADS // 2):
        v_pair = kv[:, (C_HEADS + p) * LANES:(C_HEADS + p + 1) * LANES].astype(BF16)
        vc_ref[:, 2 * p * LANES:(2 * p + 2) * LANES] = jnp.concatenate([v_pair, ones], axis=1)
    sg_ref[:, 2 * AB_W:] = silu(zc[:, C_G:])

    scale_ab = HEAD_DIM ** -0.5 * LOG2E
    qa = [slab(zab, AB_QA + j * LANES) for j in range(N_SLABS)]
    ka = slab(zab, AB_KA)
    ss0, ss1 = head_sumsq(qa[0], qa[1])
    ss2, ssk = head_sumsq(qa[2], ka)
    for j, ss in enumerate((ss0, ss1, ss2)):
        q = rope(head_norm(qa[j], ss, aqn_ref[...]), 0)
        qab_ref[:, j * LANES:(j + 1) * LANES] = (q * scale_ab).astype(BF16)
    ka_ref[...] = rope(head_norm(ka, ssk, akn_ref[...]), 0).astype(BF16)
    for j in range(N_SLABS):
        q = rope(slab(zab, AB_QB + j * LANES), 1)
        qab_ref[:, (N_SLABS + j) * LANES:(N_SLABS + j + 1) * LANES] = (q * scale_ab).astype(BF16)
    kb_ref[...] = rope(slab(zab, AB_KB), 1).astype(BF16)
    va_ref[...] = jnp.concatenate([slab(zab, AB_VA).astype(BF16), ones], axis=1)
    vb_ref[...] = jnp.concatenate([slab(zab, AB_VB).astype(BF16), ones], axis=1)
    sg_ref[:, :2 * AB_W] = silu(zab[:, AB_GA:AB_GA + 2 * AB_W])


def _layer_spec(a, layer):
    return pl.BlockSpec((None,) + a.shape[1:], lambda *_: (layer,) + (0,) * (a.ndim - 1))


def _proj_call(x, layer, weights, e_mat, tabs, prev_out=None):
    bsz, s_len, d = x.shape
    tm = min(PROJ_TM, s_len)
    grid = (s_len // tm, bsz)
    tok = lambda w: pl.BlockSpec((None, tm, w), lambda s, b: (b, s, 0))
    bf = lambda *shape: jax.ShapeDtypeStruct(shape, BF16)
    widths = [2 * N_SLABS * LANES, LANES, 2 * LANES, LANES, 2 * LANES, MIX_W, C_HEADS * LANES]
    in_specs = [tok(d)]
    args = [x]
    out_specs = ([tok(w) for w in widths]
                 + [pl.BlockSpec((None, C_HEADS, tm, LANES), lambda s, b: (b, 0, s, 0)), tok(C_HEADS * LANES)])
    out_shape = ([bf(bsz, s_len, w) for w in widths]
                 + [bf(bsz, C_HEADS, s_len, LANES), bf(bsz, s_len, C_HEADS * LANES)])
    body = _proj_body
    if prev_out is not None:
        oa, ob, oc, w_out = prev_out
        in_specs += [tok(oa.shape[-1]), tok(ob.shape[-1]), tok(oc.shape[-1]), _layer_spec(w_out, layer - 1)]
        args += [oa, ob, oc, w_out]
        out_specs.append(tok(d))
        out_shape.append(jax.ShapeDtypeStruct(x.shape, F32))
        body = _out_proj_body
    return pl.pallas_call(
        body,
        grid=grid,
        in_specs=in_specs + [_layer_spec(a, layer) for a in weights]
                 + [pl.BlockSpec(e_mat.shape, lambda s, b: (0, 0)),
                    pl.BlockSpec((6, tm, LANES), lambda s, b: (0, s, 0))],
        out_specs=out_specs,
        out_shape=out_shape,
        compiler_params=pltpu.CompilerParams(
            dimension_semantics=("arbitrary", "arbitrary"), vmem_limit_bytes=ATTN_A_VMEM_LIMIT),
        name="proj" if prev_out is None else "out_proj",
    )(*args, *weights, e_mat, tabs)


def _divmod(t, n):
    if n & (n - 1) == 0:
        return lax.shift_right_logical(t, n.bit_length() - 1), t & (n - 1)
    return lax.div(t, n), lax.rem(t, n)


def _scores(q, k):
    return lax.dot_general(q, k, (((1,), (1,)), ((), ())), preferred_element_type=F32)


def _define_first_slot(t, s_b, m_b):
    @pl.when(t == 0)
    def _():
        s_b[...] = jnp.zeros(s_b.shape, F32)
        m_b[...] = jnp.zeros(m_b.shape, F32)


def _run_parity(t, step, s_a, m_a, s_b, m_b):
    parity = t & 1

    @pl.when(parity == 0)
    def _():
        step(s_a, m_a, s_b, m_b)

    @pl.when(parity == 1)
    def _():
        step(s_b, m_b, s_a, m_a)


def _softmax_numerators(s_r, m_r, rows, width):
    m = m_r[rows, :]
    return jnp.concatenate(
        [jnp.exp2(s_r[rows, j * LANES:(j + 1) * LANES] - m).astype(BF16) for j in range(width // LANES)], axis=1)


def _stack_queries(q, head_a):
    slabs = [q[:, j * LANES:(j + 1) * LANES] for j in range(N_SLABS)]
    return jnp.concatenate([jnp.where(head_a, s, 0.0) for s in slabs]
                           + [jnp.where(head_a, 0.0, s) for s in slabs], axis=0).astype(BF16)


def _stack_pipe_body(q_ref, k_ref, v_ref, sg_ref, o_ref, s_a, s_b, m_a, m_b):
    t = pl.program_id(0)
    tq = q_ref.shape[0]
    s_len = k_ref.shape[0]
    _define_first_slot(t, s_b, m_b)

    sub = min(ATTN_A_SUB, tq)
    lane = lax.broadcasted_iota(jnp.int32, (sub, LANES), 1)
    out_low = lane < HALF

    def step(s_w, m_w, s_r, m_r):
        q = q_ref[...].astype(F32)
        for u in range(tq // sub):
            s = _scores(_stack_queries(q[u * sub:(u + 1) * sub], (lane & QUARTER) == 0), k_ref[...])
            for i in range(AB_HEADS):
                blk = s[i * sub:(i + 1) * sub]
                rows = pl.ds((u * AB_HEADS + i) * sub, sub)
                s_w[rows, :] = blk
                m_w[rows, :] = jnp.broadcast_to(jnp.max(blk, axis=1, keepdims=True), (sub, LANES))
        for u in range(tq // sub):
            rows = [pl.ds((u * AB_HEADS + i) * sub, sub) for i in range(AB_HEADS)]
            p = jnp.concatenate([_softmax_numerators(s_r, m_r, rows[i], s_len) for i in range(AB_HEADS)], axis=0)
            a = jnp.dot(p, v_ref[...], preferred_element_type=F32)
            for j in range(N_SLABS):
                lo, hi = a[j * sub:(j + 1) * sub], a[(j + N_SLABS) * sub:(j + N_SLABS + 1) * sub]
                o = jnp.where(out_low, lo[:, :LANES] / lo[:, LANES:], hi[:, :LANES] / hi[:, LANES:])
                cols = slice(j * LANES, (j + 1) * LANES)
                gate = sg_ref[u * sub:(u + 1) * sub, cols].astype(F32)
                o_ref[u * sub:(u + 1) * sub, cols] = (o * gate).astype(BF16)

    _run_parity(t, step, s_a, m_a, s_b, m_b)


def _window_pipe_body(sink_ref, q_ref, k_ref, v_ref, sg_ref, *refs, n_q, n_items, layer, final):
    if final:
        x_ref, oa_ref, oc_ref, wo_ref, fg_ref, y_ref, s_a, s_b, m_a, m_b, o_ref = refs
    else:
        o_ref, s_a, s_b, m_a, m_b = refs
    t = pl.program_id(0)
    tq = q_ref.shape[0]
    s_len = k_ref.shape[0]
    sub = WINDOW
    kw = 3 * WINDOW
    n_sub = tq // sub
    t_cur = jnp.minimum(t, n_items - 1)
    t_prev = jnp.maximum(t - 1, 0)
    _define_first_slot(t, s_b, m_b)

    lane = lax.broadcasted_iota(jnp.int32, (sub, LANES), 1)
    row = lax.broadcasted_iota(jnp.int32, (sub, LANES), 0)
    head_a = (lane & QUARTER) == 0
    out_low = lane < HALF
    row0_cur = _divmod(t_cur, n_q)[1] * tq
    row0_prev = _divmod(t_prev, n_q)[1] * tq
    sinks = [sink_ref[layer, i] * LOG2E for i in range(AB_HEADS)]

    def win_start(r0):
        return pl.multiple_of(jnp.clip(r0 - WINDOW, 0, s_len - kw), WINDOW)

    def step(s_w, m_w, s_r, m_r):
        q = q_ref[...].astype(F32)
        for u in range(n_sub):
            r0 = row0_cur + u * sub
            ws = win_start(r0)
            dist = (r0 - ws) + row - lane
            bias = [jnp.where(jnp.abs(dist - c * LANES) <= WINDOW, 0.0, NEG) for c in range(3)]
            s = _scores(_stack_queries(q[u * sub:(u + 1) * sub], head_a), k_ref[pl.ds(ws, kw), :])
            for i in range(AB_HEADS):
                blk = s[i * sub:(i + 1) * sub]
                cols = [blk[:, c * LANES:(c + 1) * LANES] + bias[c] for c in range(3)]
                mx = jnp.max(jnp.maximum(jnp.maximum(cols[0], cols[1]), cols[2]), axis=1, keepdims=True)
                rows = pl.ds((u * AB_HEADS + i) * sub, sub)
                for c in range(3):
                    s_w[rows, c * LANES:(c + 1) * LANES] = cols[c]
                m_w[rows, :] = jnp.broadcast_to(jnp.maximum(mx, sinks[i]), (sub, LANES))
        for u in range(n_sub):
            v_win = v_ref[pl.ds(win_start(row0_prev + u * sub), kw), :]
            rows = [pl.ds((u * AB_HEADS + i) * sub, sub) for i in range(AB_HEADS)]
            p = jnp.concatenate([_softmax_numerators(s_r, m_r, rows[i], kw) for i in range(AB_HEADS)], axis=0)
            a = jnp.dot(p, v_win, preferred_element_type=F32)
            for j in range(N_SLABS):
                parts = []
                for i in (j, j + N_SLABS):
                    blk = a[i * sub:(i + 1) * sub]
                    den = blk[:, LANES:] + jnp.exp2(sinks[i] - m_r[rows[i], :])
                    parts.append(blk[:, :LANES] / den)
                cols = slice(j * LANES, (j + 1) * LANES)
                gate = sg_ref[u * sub:(u + 1) * sub, cols].astype(F32)
                o_ref[u * sub:(u + 1) * sub, cols] = (jnp.where(out_low, parts[0], parts[1]) * gate).astype(BF16)
        if final:
            o = jnp.concatenate([oa_ref[...], o_ref[...], oc_ref[...]], axis=1)
            r = x_ref[...] + jnp.dot(o, wo_ref[...], preferred_element_type=F32)
            y_ref[...] = r * lax.rsqrt(jnp.mean(r * r, axis=-1, keepdims=True) + EPS) * fg_ref[...]

    _run_parity(t, step, s_a, m_a, s_b, m_b)


def _stack_call(name, qab, k, v, sg, blk, tq, sink=None, layer=0, final=None):
    bsz, s_len, _ = qab.shape
    tq = min(tq, s_len)
    n_q = s_len // tq
    n_items = bsz * n_q
    windowed = sink is not None
    cur = lambda t: _divmod(jnp.minimum(t, n_items - 1), n_q)
    prev = lambda t: _divmod(jnp.maximum(t - 1, 0), n_q)
    in_specs = [
        pl.BlockSpec((None, tq, AB_W), lambda t: (*cur(t), blk)),
        pl.BlockSpec((None, s_len, LANES), lambda t: (cur(t)[0], 0, 0)),
        pl.BlockSpec((None, s_len, 2 * LANES), lambda t: (prev(t)[0], 0, 0)),
        pl.BlockSpec((None, tq, AB_W), lambda t: (*prev(t), blk)),
    ]
    args = [qab, k, v, sg]
    rows = AB_HEADS * tq
    width = s_len
    body = _stack_pipe_body
    if windowed:
        in_specs = [pl.BlockSpec(memory_space=pltpu.SMEM)] + in_specs
        args = [sink] + args
        width = 3 * WINDOW
        body = functools.partial(_window_pipe_body, n_q=n_q, n_items=n_items, layer=layer, final=final is not None)
    scratch = [pltpu.VMEM((rows, width), F32), pltpu.VMEM((rows, width), F32),
               pltpu.VMEM((rows, LANES), F32), pltpu.VMEM((rows, LANES), F32)]
    out_specs = pl.BlockSpec((None, tq, AB_W), lambda t: (*prev(t), 0))
    out_shape = jax.ShapeDtypeStruct((bsz, s_len, AB_W), BF16)
    if final is not None:
        x, oa, oc, w_out, fg = final
        tok = lambda a: pl.BlockSpec((None, tq, a.shape[-1]), lambda t: (*prev(t), 0))
        in_specs += [tok(x), tok(oa), tok(oc), _layer_spec(w_out, layer), pl.BlockSpec(fg.shape, lambda t: (0, 0))]
        args += [x, oa, oc, w_out, fg]
        out_specs, out_shape = tok(x), jax.ShapeDtypeStruct(x.shape, F32)
        scratch.append(pltpu.VMEM((tq, AB_W), BF16))
    return pl.pallas_call(
        body,
        grid=(n_items + 1,),
        in_specs=in_specs,
        out_specs=out_specs,
        out_shape=out_shape,
        scratch_shapes=scratch,
        compiler_params=pltpu.CompilerParams(
            dimension_semantics=("arbitrary",),
            vmem_limit_bytes=VMEM_LIMIT if (windowed and final is None) else ATTN_A_VMEM_LIMIT),
        name=name,
    )(*args)


def _pair_pipe_body(q0_ref, q1_ref, k0_ref, k1_ref, v_ref, sg_ref, o_ref, s_a, s_b, m_a, m_b):
    t = pl.program_id(0)
    tq, s_len = s_a.shape[1], s_a.shape[2]
    _define_first_slot(t, s_b, m_b)
    out_low = lax.broadcasted_iota(jnp.int32, (tq, LANES), 1) < HALF
    qs, ks = (q0_ref, q1_ref), (k0_ref, k1_ref)

    def step(s_w, m_w, s_r, m_r):
        for hd in range(2):
            s = _scores(qs[hd][...], ks[hd][...])
            s_w[hd] = s
            m_w[hd] = jnp.broadcast_to(jnp.max(s, axis=1, keepdims=True), (tq, LANES))
        outs = []
        for hd in range(2):
            p = _softmax_numerators(s_r.at[hd], m_r.at[hd], slice(None), s_len)
            a = jnp.dot(p, v_ref[...], preferred_element_type=F32)
            outs.append(a[:, :LANES] / a[:, LANES:])
        o_ref[...] = (jnp.where(out_low, outs[0], outs[1]) * sg_ref[...].astype(F32)).astype(BF16)

    _run_parity(t, step, s_a, m_a, s_b, m_b)


def _attn_c_call(qc, kc, vc, sg, g_off):
    bsz, s_len, _ = qc.shape
    tq = min(ATTN_C_TQ, s_len)
    n_q = s_len // tq
    n_pairs = C_HEADS // 2
    n_items = bsz * n_pairs * n_q

    def item(t):
        bp, i = _divmod(t, n_q)
        b, p = _divmod(bp, n_pairs)
        return b, p, i

    cur = lambda t: item(jnp.minimum(t, n_items - 1))
    prev = lambda t: item(jnp.maximum(t - 1, 0))

    def q_spec(o):
        def imap(t):
            b, p, i = cur(t)
            return b, i, 2 * p + o
        return pl.BlockSpec((None, tq, LANES), imap)

    def k_spec(o):
        def imap(t):
            b, p, i = cur(t)
            return b, 2 * p + o, 0, 0
        return pl.BlockSpec((None, None, s_len, LANES), imap)

    def v_map(t):
        b, p, i = prev(t)
        return b, 0, p

    def sg_map(t):
        b, p, i = prev(t)
        return b, i, g_off + p

    def o_map(t):
        b, p, i = prev(t)
        return b, i, p

    return pl.pallas_call(
        _pair_pipe_body,
        grid=(n_items + 1,),
        in_specs=[q_spec(0), q_spec(1), k_spec(0), k_spec(1),
                  pl.BlockSpec((None, s_len, 2 * LANES), v_map), pl.BlockSpec((None, tq, LANES), sg_map)],
        out_specs=pl.BlockSpec((None, tq, LANES), o_map),
        out_shape=jax.ShapeDtypeStruct((bsz, s_len, n_pairs * LANES), BF16),
        scratch_shapes=[pltpu.VMEM((2, tq, s_len), F32), pltpu.VMEM((2, tq, s_len), F32),
                        pltpu.VMEM((2, tq, LANES), F32), pltpu.VMEM((2, tq, LANES), F32)],
        compiler_params=pltpu.CompilerParams(dimension_semantics=("arbitrary",), vmem_limit_bytes=VMEM_LIMIT),
        name="attn_c",
    )(qc, qc, kc, kc, vc, sg)


def kernel(x, norm_g, w_in, a_q_norm, a_k_norm, b_sink, c_q_norm, c_kv_norm, c_w_uq, c_w_ukv, w_out, final_g):
    depth = w_in.shape[0]
    rows = lambda v: v.astype(F32)[:, None, :]
    w_in = w_in.astype(BF16)
    weights = [rows(norm_g), _pack_w_c(w_in), _pack_w_ab(w_in),
               rows(a_q_norm[:, _DIM_A]), rows(a_k_norm[:, _DIM_A]),
               rows(jnp.pad(c_q_norm, ((0, 0), (0, C_Q_PAD - C_Q_RANK)))), rows(c_kv_norm),
               _pack_w_uq(c_w_uq), _pack_w_ukv(c_w_ukv)]
    w_out_packed = _pack_w_out(w_out)
    sink = b_sink.astype(F32)
    tabs = _rope_tables(x.shape[1])
    e_mat = _group_sum_matrix()
    prev_out = None
    for l in range(depth):
        outs = _proj_call(x, l, weights, e_mat, tabs, prev_out=prev_out)
        if prev_out is not None:
            x = outs[-1]
        qab, ka, va, kb, vb, sg, qc, kc, vc = outs[:9]
        oa = _stack_call("attn_a", qab, ka, va, sg, 0, ATTN_A_TQ)
        oc = _attn_c_call(qc, kc, vc, sg, 2 * N_SLABS)
        if l == depth - 1:
            return _stack_call("attn_b_out", qab, kb, vb, sg, 1, ATTN_B_FINAL_TQ, sink=sink, layer=l,
                               final=(x, oa, oc, w_out_packed, final_g.astype(F32)[None, :]))
        ob = _stack_call("attn_b", qab, kb, vb, sg, 1, ATTN_B_TQ, sink=sink, layer=l)
        prev_out = (oa, ob, oc, w_out_packed)
```

```python
import functools

import numpy as np
import jax
import jax.numpy as jnp
from jax import lax
from jax.experimental import pallas as pl
from jax.experimental.pallas import tpu as pltpu

F32 = jnp.float32
BF16 = jnp.bfloat16

GRID_W = 64
ROPE_THETA = 10000.0
EPS = 1e-6
HEAD_DIM = 64
WINDOW = 128
AB_HEADS = 6
C_HEADS = 4
C_NOPE = 64
C_ROPE = 32
C_V = 64
C_Q_RANK = 192
C_KV_RANK = 128

LANES = 128
HALF = LANES // 2
QUARTER = LANES // 4
N_SLABS = AB_HEADS // 2
AB_W = AB_HEADS * HEAD_DIM
C_Q_PAD = 256
NEG = -1e30
LOG2E = 1.4426950408889634

IN_AQ, IN_AK, IN_AV, IN_AG = 0, 384, 512, 640
IN_BQ, IN_BK, IN_BV, IN_BG = 1024, 1408, 1536, 1664
IN_CQ, IN_CKV, IN_CKR, IN_CG = 2048, 2240, 2368, 2400
IN_AB_W = 2048

AB_QA, AB_KA, AB_QB, AB_KB, AB_VA, AB_VB, AB_GA, AB_GB = 0, 384, 512, 896, 1024, 1152, 1280, 1664
C_CQ, C_CKV, C_CKR, C_G, PACKED_C_W = 0, 256, 384, 512, 768
MIX_W = 1024

PACK_TM = 512
PROJ_TM = 1024
ATTN_A_TQ = 512
ATTN_A_SUB = 128
ATTN_B_TQ = 1024
ATTN_B_FINAL_TQ = 1024
ATTN_C_TQ = 1024
VMEM_LIMIT = 56 * 1024 * 1024
LARGE_VMEM_LIMIT = 60 * 1024 * 1024


_LANE = np.arange(LANES)
_HEAD_OF_LANE = (_LANE // QUARTER) % 2


def _dim_of_lane(first_half_dims, partner):
    f = np.asarray(first_half_dims)
    return np.where(_LANE < HALF, f[_LANE % QUARTER], f[_LANE % QUARTER] + partner)


_DIM_A = _dim_of_lane(list(range(16)) + list(range(32, 48)), 16)
_DIM_B = _dim_of_lane(list(range(32)), 32)

_C_ROPE_OF_LANE = np.full(LANES, -1)
_C_ROPE_OF_LANE[0:16] = np.arange(16)
_C_ROPE_OF_LANE[64:80] = 16 + np.arange(16)


def _permutation(src_of_dst):
    n = len(src_of_dst)
    p = np.zeros((n, n), np.float32)
    p[src_of_dst, np.arange(n)] = 1.0
    return jnp.asarray(p, BF16)


def _slab_sources(dim_of_lane, n_slabs):
    return np.concatenate([np.where(_HEAD_OF_LANE == 0, p, p + n_slabs) * HEAD_DIM + dim_of_lane
                           for p in range(n_slabs)])


def _pair_sources():
    d = np.arange(HEAD_DIM)
    return np.concatenate([np.concatenate([p * HEAD_DIM + d, (p + N_SLABS) * HEAD_DIM + d]) for p in range(N_SLABS)])


def _c_slab(x, rope):
    lead = x.shape[:-1]
    zeros = lambda n: jnp.zeros(lead + (n,), x.dtype)
    r0, r1 = (rope[..., :16], rope[..., 16:]) if rope is not None else (zeros(16), zeros(16))
    return jnp.concatenate([r0, x[..., :48], r1, x[..., 48:C_NOPE], zeros(LANES - C_NOPE - C_ROPE)], axis=-1)


def _pack_w_c(w):
    zeros = lambda n: jnp.zeros(w.shape[:-1] + (n,), w.dtype)
    ckr = w[..., IN_CKR:IN_CKR + C_ROPE]
    return jnp.concatenate([
        w[..., IN_CQ:IN_CQ + C_Q_RANK], zeros(C_Q_PAD - C_Q_RANK),
        w[..., IN_CKV:IN_CKV + C_KV_RANK],
        ckr[..., :16], zeros(48), ckr[..., 16:], zeros(48),
        w[..., IN_CG:IN_CG + C_HEADS * C_V]], axis=-1)


def _pack_w_uq(w):
    n = w.shape[0]
    w = w.astype(BF16).reshape(n, C_Q_RANK, C_HEADS, C_NOPE + C_ROPE)
    slab = _c_slab(w[..., :C_NOPE], w[..., C_NOPE:]).reshape(n, C_Q_RANK, C_HEADS * LANES)
    return jnp.pad(slab, ((0, 0), (0, C_Q_PAD - C_Q_RANK), (0, 0)))


def _pack_w_ukv(w):
    n = w.shape[0]
    w = w.astype(BF16).reshape(n, C_KV_RANK, C_HEADS, C_NOPE + C_V)
    k = _c_slab(w[..., :C_NOPE], None).reshape(n, C_KV_RANK, C_HEADS * LANES)
    v = w[..., C_NOPE:].reshape(n, C_KV_RANK, C_HEADS * C_V)
    return jnp.concatenate([k, v], axis=-1)


def _pack_w_out(w):
    n, _, dd = w.shape
    pair = lambda r: r.reshape(n, 2, N_SLABS, HEAD_DIM, dd).transpose(0, 2, 1, 3, 4).reshape(n, AB_W, dd)
    return jnp.concatenate([pair(w[:, :AB_W]), pair(w[:, AB_W:2 * AB_W]), w[:, 2 * AB_W:]], axis=1).astype(BF16)


def _group_sum_matrix():
    head = np.concatenate([_HEAD_OF_LANE, 2 + _HEAD_OF_LANE])
    return jnp.asarray(head[:, None] == head[None, :], BF16)


def _rope_tables(s_len):
    t = np.arange(s_len, dtype=np.float64)[:, None]
    sign = np.where(_LANE < HALF, -1.0, 1.0)[None, :]

    def tables(pos, freq, dim, active=None):
        ang = pos * (ROPE_THETA ** (-2.0 * freq / dim))[None, :]
        cos, sin = np.cos(ang), np.sin(ang) * sign
        if active is not None:
            cos, sin = np.where(active[None, :], cos, 1.0), np.where(active[None, :], sin, 0.0)
        return [cos, sin]

    pos_a = np.where((_DIM_A < HEAD_DIM // 2)[None, :], t // GRID_W, t % GRID_W)
    rope_c = _C_ROPE_OF_LANE >= 0
    tabs = (tables(pos_a, _DIM_A % 16, HEAD_DIM // 2) + tables(t, _DIM_B % 32, HEAD_DIM)
            + tables(t, np.maximum(_C_ROPE_OF_LANE, 0) % 16, C_ROPE, rope_c))
    return jnp.asarray(np.stack(tabs).astype(np.float32))


def _pack_body(w_ref, pqa_ref, pka_ref, pqb_ref, pkb_ref, pg_ref, o_ref):
    def move(src, dst, width, perm_ref):
        x = w_ref[:, src:src + width]
        if perm_ref is not None:
            x = jnp.dot(x, perm_ref[...], preferred_element_type=F32).astype(BF16)
        o_ref[:, dst:dst + width] = x

    move(IN_AQ, AB_QA, AB_W, pqa_ref)
    move(IN_AK, AB_KA, LANES, pka_ref)
    move(IN_BQ, AB_QB, AB_W, pqb_ref)
    move(IN_BK, AB_KB, LANES, pkb_ref)
    move(IN_AV, AB_VA, LANES, None)
    move(IN_BV, AB_VB, LANES, None)
    move(IN_AG, AB_GA, AB_W, pg_ref)
    move(IN_BG, AB_GB, AB_W, pg_ref)


def _pack_w_ab(w_in):
    depth, d, _ = w_in.shape
    tm = min(PACK_TM, d)
    perms = [_permutation(_slab_sources(_DIM_A, N_SLABS)), _permutation(_slab_sources(_DIM_A, 1)),
             _permutation(_slab_sources(_DIM_B, N_SLABS)), _permutation(_slab_sources(_DIM_B, 1)),
             _permutation(_pair_sources())]
    return pl.pallas_call(
        _pack_body,
        grid=(depth, d // tm),
        in_specs=[pl.BlockSpec((None, tm, IN_AB_W), lambda l, r: (l, r, 0))]
                 + [pl.BlockSpec(p.shape, lambda l, r: (0, 0)) for p in perms],
        out_specs=pl.BlockSpec((None, tm, IN_AB_W), lambda l, r: (l, r, 0)),
        out_shape=jax.ShapeDtypeStruct((depth, d, IN_AB_W), BF16),
        compiler_params=pltpu.CompilerParams(
            dimension_semantics=("arbitrary", "arbitrary"), vmem_limit_bytes=VMEM_LIMIT),
        name="pack",
    )(w_in, *perms)


def _proj_body(x_ref, *refs):
    _proj_compute(x_ref[...], *refs)


def _out_proj_body(x_ref, oa_ref, ob_ref, oc_ref, wo_ref, *refs):
    proj_refs, x_out_ref = refs[:-1], refs[-1]
    o = jnp.concatenate([oa_ref[...], ob_ref[...], oc_ref[...]], axis=1)
    x = x_ref[...] + jnp.dot(o, wo_ref[...], preferred_element_type=F32)
    x_out_ref[...] = x
    _proj_compute(x, *proj_refs)


def _proj_compute(x, g_ref, wc_ref, wab_ref, aqn_ref, akn_ref, cqn_ref, ckvn_ref, wuq_ref, wukv_ref, e_ref, tab_ref,
                  qab_ref, ka_ref, va_ref, kb_ref, vb_ref, sg_ref, qc_ref, kc_ref, vc_ref):
    h = (x * lax.rsqrt(jnp.mean(x * x, axis=-1, keepdims=True) + EPS) * g_ref[...]).astype(BF16)
    zc = jnp.dot(h, wc_ref[...], preferred_element_type=F32)
    zab = jnp.dot(h, wab_ref[...], preferred_element_type=F32)

    def rope(xs, mixer):
        return xs * tab_ref[2 * mixer] + pltpu.roll(xs, HALF, 1) * tab_ref[2 * mixer + 1]

    def slab(z, off):
        return z[:, off:off + LANES]

    def head_sumsq(s0, s1):
        t = jnp.concatenate([s0 * s0, s1 * s1], axis=1).astype(BF16)
        ss = jnp.dot(t, e_ref[...], preferred_element_type=F32)
        return ss[:, :LANES], ss[:, LANES:]

    def head_norm(xs, ss, gain):
        return xs * lax.rsqrt(ss * (1.0 / HEAD_DIM) + EPS) * gain

    def silu(gz):
        return (gz * (0.5 * jnp.tanh(0.5 * gz) + 0.5)).astype(BF16)

    cq = zc[:, C_CQ:C_CQ + C_Q_PAD]
    cq = cq * lax.rsqrt(jnp.sum(cq * cq, axis=-1, keepdims=True) * (1.0 / C_Q_RANK) + EPS) * cqn_ref[...]
    qc = jnp.dot(cq.astype(BF16), wuq_ref[...], preferred_element_type=F32)
    scale_c = (C_NOPE + C_ROPE) ** -0.5 * LOG2E
    for hh in range(C_HEADS):
        q = rope(qc[:, hh * LANES:(hh + 1) * LANES], 2)
        qc_ref[:, hh * LANES:(hh + 1) * LANES] = (q * scale_c).astype(BF16)
    ckv = zc[:, C_CKV:C_CKV + C_KV_RANK]
    ckv = ckv * lax.rsqrt(jnp.mean(ckv * ckv, axis=-1, keepdims=True) + EPS) * ckvn_ref[...]
    kv = jnp.dot(ckv.astype(BF16), wukv_ref[...], preferred_element_type=F32)
    kr = rope(slab(zc, C_CKR), 2)
    for hh in range(C_HEADS):
        kc_ref[hh] = (kv[:, hh * LANES:(hh + 1) * LANES] + kr).astype(BF16)
    ones = jnp.ones((x.shape[0], LANES), BF16)
    for p in range(C_HEADS // 2):
        v_pair = kv[:, (C_HEADS + p) * LANES:(C_HEADS + p + 1) * LANES].astype(BF16)
        vc_ref[:, 2 * p * LANES:(2 * p + 2) * LANES] = jnp.concatenate([v_pair, ones], axis=1)
    sg_ref[:, 2 * AB_W:] = silu(zc[:, C_G:])

    scale_ab = HEAD_DIM ** -0.5 * LOG2E
    qa = [slab(zab, AB_QA + j * LANES) for j in range(N_SLABS)]
    ka = slab(zab, AB_KA)
    ss0, ss1 = head_sumsq(qa[0], qa[1])
    ss2, ssk = head_sumsq(qa[2], ka)
    for j, ss in enumerate((ss0, ss1, ss2)):
        q = rope(head_norm(qa[j], ss, aqn_ref[...]), 0)
        qab_ref[:, j * LANES:(j + 1) * LANES] = (q * scale_ab).astype(BF16)
    ka_ref[...] = rope(head_norm(ka, ssk, akn_ref[...]), 0).astype(BF16)
    for j in range(N_SLABS):
        q = rope(slab(zab, AB_QB + j * LANES), 1)
        qab_ref[:, (N_SLABS + j) * LANES:(N_SLABS + j + 1) * LANES] = (q * scale_ab).astype(BF16)
    kb_ref[...] = rope(slab(zab, AB_KB), 1).astype(BF16)
    va_ref[...] = jnp.concatenate([slab(zab, AB_VA).astype(BF16), ones], axis=1)
    vb_ref[...] = jnp.concatenate([slab(zab, AB_VB).astype(BF16), ones], axis=1)
    sg_ref[:, :2 * AB_W] = silu(zab[:, AB_GA:AB_GA + 2 * AB_W])


def _layer_spec(a, layer):
    return pl.BlockSpec((None,) + a.shape[1:], lambda *_: (layer,) + (0,) * (a.ndim - 1))


def _proj_call(x, layer, weights, e_mat, tabs, prev_out=None):
    bsz, s_len, d = x.shape
    tm = min(PROJ_TM, s_len)
    grid = (s_len // tm, bsz)
    tok = lambda w: pl.BlockSpec((None, tm, w), lambda s, b: (b, s, 0))
    bf = lambda *shape: jax.ShapeDtypeStruct(shape, BF16)
    widths = [2 * N_SLABS * LANES, LANES, 2 * LANES, LANES, 2 * LANES, MIX_W, C_HEADS * LANES]
    in_specs = [tok(d)]
    args = [x]
    out_specs = ([tok(w) for w in widths]
                 + [pl.BlockSpec((None, C_HEADS, tm, LANES), lambda s, b: (b, 0, s, 0)), tok(C_HEADS * LANES)])
    out_shape = ([bf(bsz, s_len, w) for w in widths]
                 + [bf(bsz, C_HEADS, s_len, LANES), bf(bsz, s_len, C_HEADS * LANES)])
    body = _proj_body
    if prev_out is not None:
        oa, ob, oc, w_out = prev_out
        in_specs += [tok(oa.shape[-1]), tok(ob.shape[-1]), tok(oc.shape[-1]), _layer_spec(w_out, layer - 1)]
        args += [oa, ob, oc, w_out]
        out_specs.append(tok(d))
        out_shape.append(jax.ShapeDtypeStruct(x.shape, F32))
        body = _out_proj_body
    return pl.pallas_call(
        body,
        grid=grid,
        in_specs=in_specs + [_layer_spec(a, layer) for a in weights]
                 + [pl.BlockSpec(e_mat.shape, lambda s, b: (0, 0)),
                    pl.BlockSpec((6, tm, LANES), lambda s, b: (0, s, 0))],
        out_specs=out_specs,
        out_shape=out_shape,
        compiler_params=pltpu.CompilerParams(
            dimension_semantics=("arbitrary", "arbitrary"), vmem_limit_bytes=LARGE_VMEM_LIMIT),
        name="proj" if prev_out is None else "out_proj",
    )(*args, *weights, e_mat, tabs)


def _divmod(t, n):
    if n & (n - 1) == 0:
        return lax.shift_right_logical(t, n.bit_length() - 1), t & (n - 1)
    return lax.div(t, n), lax.rem(t, n)


def _scores(q, k):
    return lax.dot_general(q, k, (((1,), (1,)), ((), ())), preferred_element_type=F32)


def _define_first_slot(t, s_b, m_b):
    @pl.when(t == 0)
    def _():
        s_b[...] = jnp.zeros(s_b.shape, F32)
        m_b[...] = jnp.zeros(m_b.shape, F32)


def _run_parity(t, step, s_a, m_a, s_b, m_b):
    parity = t & 1

    @pl.when(parity == 0)
    def _():
        step(s_a, m_a, s_b, m_b)

    @pl.when(parity == 1)
    def _():
        step(s_b, m_b, s_a, m_a)


def _softmax_numerators(s_r, m_r, rows, width):
    m = m_r[rows, :]
    return jnp.concatenate(
        [jnp.exp2(s_r[rows, j * LANES:(j + 1) * LANES] - m).astype(BF16) for j in range(width // LANES)], axis=1)


def _stack_queries(q, head_a):
    slabs = [q[:, j * LANES:(j + 1) * LANES] for j in range(N_SLABS)]
    return jnp.concatenate([jnp.where(head_a, s, 0.0) for s in slabs]
                           + [jnp.where(head_a, 0.0, s) for s in slabs], axis=0).astype(BF16)


def _stack_pipe_body(q_ref, k_ref, v_ref, sg_ref, o_ref, s_a, s_b, m_a, m_b):
    t = pl.program_id(0)
    tq = q_ref.shape[0]
    s_len = k_ref.shape[0]
    _define_first_slot(t, s_b, m_b)

    sub = min(ATTN_A_SUB, tq)
    lane = lax.broadcasted_iota(jnp.int32, (sub, LANES), 1)
    out_low = lane < HALF

    def step(s_w, m_w, s_r, m_r):
        q = q_ref[...].astype(F32)
        for u in range(tq // sub):
            s = _scores(_stack_queries(q[u * sub:(u + 1) * sub], (lane & QUARTER) == 0), k_ref[...])
            for i in range(AB_HEADS):
                blk = s[i * sub:(i + 1) * sub]
                rows = pl.ds((u * AB_HEADS + i) * sub, sub)
                s_w[rows, :] = blk
                m_w[rows, :] = jnp.broadcast_to(jnp.max(blk, axis=1, keepdims=True), (sub, LANES))
        for u in range(tq // sub):
            rows = [pl.ds((u * AB_HEADS + i) * sub, sub) for i in range(AB_HEADS)]
            p = jnp.concatenate([_softmax_numerators(s_r, m_r, rows[i], s_len) for i in range(AB_HEADS)], axis=0)
            a = jnp.dot(p, v_ref[...], preferred_element_type=F32)
            for j in range(N_SLABS):
                lo, hi = a[j * sub:(j + 1) * sub], a[(j + N_SLABS) * sub:(j + N_SLABS + 1) * sub]
                o = jnp.where(out_low, lo[:, :LANES] / lo[:, LANES:], hi[:, :LANES] / hi[:, LANES:])
                cols = slice(j * LANES, (j + 1) * LANES)
                gate = sg_ref[u * sub:(u + 1) * sub, cols].astype(F32)
                o_ref[u * sub:(u + 1) * sub, cols] = (o * gate).astype(BF16)

    _run_parity(t, step, s_a, m_a, s_b, m_b)


def _window_pipe_body(sink_ref, q_ref, k_ref, v_ref, sg_ref, *refs, n_q, n_items, layer, final):
    if final:
        x_ref, oa_ref, oc_ref, wo_ref, fg_ref, y_ref, s_a, s_b, m_a, m_b, o_ref = refs
    else:
        o_ref, s_a, s_b, m_a, m_b = refs
    t = pl.program_id(0)
    tq = q_ref.shape[0]
    s_len = k_ref.shape[0]
    sub = WINDOW
    kw = 3 * WINDOW
    n_sub = tq // sub
    t_cur = jnp.minimum(t, n_items - 1)
    t_prev = jnp.maximum(t - 1, 0)
    _define_first_slot(t, s_b, m_b)

    lane = lax.broadcasted_iota(jnp.int32, (sub, LANES), 1)
    row = lax.broadcasted_iota(jnp.int32, (sub, LANES), 0)
    head_a = (lane & QUARTER) == 0
    out_low = lane < HALF
    row0_cur = _divmod(t_cur, n_q)[1] * tq
    row0_prev = _divmod(t_prev, n_q)[1] * tq
    sinks = [sink_ref[layer, i] * LOG2E for i in range(AB_HEADS)]

    def win_start(r0):
        return pl.multiple_of(jnp.clip(r0 - WINDOW, 0, s_len - kw), WINDOW)

    def step(s_w, m_w, s_r, m_r):
        q = q_ref[...].astype(F32)
        for u in range(n_sub):
            r0 = row0_cur + u * sub
            ws = win_start(r0)
            dist = (r0 - ws) + row - lane
            bias = [jnp.where(jnp.abs(dist - c * LANES) <= WINDOW, 0.0, NEG) for c in range(3)]
            s = _scores(_stack_queries(q[u * sub:(u + 1) * sub], head_a), k_ref[pl.ds(ws, kw), :])
            for i in range(AB_HEADS):
                blk = s[i * sub:(i + 1) * sub]
                cols = [blk[:, c * LANES:(c + 1) * LANES] + bias[c] for c in range(3)]
                mx = jnp.max(jnp.maximum(jnp.maximum(cols[0], cols[1]), cols[2]), axis=1, keepdims=True)
                rows = pl.ds((u * AB_HEADS + i) * sub, sub)
                for c in range(3):
                    s_w[rows, c * LANES:(c + 1) * LANES] = cols[c]
                m_w[rows, :] = jnp.broadcast_to(jnp.maximum(mx, sinks[i]), (sub, LANES))
        for u in range(n_sub):
            v_win = v_ref[pl.ds(win_start(row0_prev + u * sub), kw), :]
            rows = [pl.ds((u * AB_HEADS + i) * sub, sub) for i in range(AB_HEADS)]
            p = jnp.concatenate([_softmax_numerators(s_r, m_r, rows[i], kw) for i in range(AB_HEADS)], axis=0)
            a = jnp.dot(p, v_win, preferred_element_type=F32)
            for j in range(N_SLABS):
                parts = []
                for i in (j, j + N_SLABS):
                    blk = a[i * sub:(i + 1) * sub]
                    den = blk[:, LANES:] + jnp.exp2(sinks[i] - m_r[rows[i], :])
                    parts.append(blk[:, :LANES] / den)
                cols = slice(j * LANES, (j + 1) * LANES)
                gate = sg_ref[u * sub:(u + 1) * sub, cols].astype(F32)
                o_ref[u * sub:(u + 1) * sub, cols] = (jnp.where(out_low, parts[0], parts[1]) * gate).astype(BF16)
        if final:
            o = jnp.concatenate([oa_ref[...], o_ref[...], oc_ref[...]], axis=1)
            r = x_ref[...] + jnp.dot(o, wo_ref[...], preferred_element_type=F32)
            y_ref[...] = r * lax.rsqrt(jnp.mean(r * r, axis=-1, keepdims=True) + EPS) * fg_ref[...]

    _run_parity(t, step, s_a, m_a, s_b, m_b)


def _stack_call(name, qab, k, v, sg, blk, tq, sink=None, layer=0, final=None):
    bsz, s_len, _ = qab.shape
    tq = min(tq, s_len)
    n_q = s_len // tq
    n_items = bsz * n_q
    windowed = sink is not None
    cur = lambda t: _divmod(jnp.minimum(t, n_items - 1), n_q)
    prev = lambda t: _divmod(jnp.maximum(t - 1, 0), n_q)
    in_specs = [
        pl.BlockSpec((None, tq, AB_W), lambda t: (*cur(t), blk)),
        pl.BlockSpec((None, s_len, LANES), lambda t: (cur(t)[0], 0, 0)),
        pl.BlockSpec((None, s_len, 2 * LANES), lambda t: (prev(t)[0], 0, 0)),
        pl.BlockSpec((None, tq, AB_W), lambda t: (*prev(t), blk)),
    ]
    args = [qab, k, v, sg]
    rows = AB_HEADS * tq
    width = s_len
    body = _stack_pipe_body
    if windowed:
        in_specs = [pl.BlockSpec(memory_space=pltpu.SMEM)] + in_specs
        args = [sink] + args
        width = 3 * WINDOW
        body = functools.partial(_window_pipe_body, n_q=n_q, n_items=n_items, layer=layer, final=final is not None)
    scratch = [pltpu.VMEM((rows, width), F32), pltpu.VMEM((rows, width), F32),
               pltpu.VMEM((rows, LANES), F32), pltpu.VMEM((rows, LANES), F32)]
    out_specs = pl.BlockSpec((None, tq, AB_W), lambda t: (*prev(t), 0))
    out_shape = jax.ShapeDtypeStruct((bsz, s_len, AB_W), BF16)
    if final is not None:
        x, oa, oc, w_out, fg = final
        tok = lambda a: pl.BlockSpec((None, tq, a.shape[-1]), lambda t: (*prev(t), 0))
        in_specs += [tok(x), tok(oa), tok(oc), _layer_spec(w_out, layer), pl.BlockSpec(fg.shape, lambda t: (0, 0))]
        args += [x, oa, oc, w_out, fg]
        out_specs, out_shape = tok(x), jax.ShapeDtypeStruct(x.shape, F32)
        scratch.append(pltpu.VMEM((tq, AB_W), BF16))
    return pl.pallas_call(
        body,
        grid=(n_items + 1,),
        in_specs=in_specs,
        out_specs=out_specs,
        out_shape=out_shape,
        scratch_shapes=scratch,
        compiler_params=pltpu.CompilerParams(
            dimension_semantics=("arbitrary",),
            vmem_limit_bytes=VMEM_LIMIT if (windowed and final is None) else LARGE_VMEM_LIMIT),
        name=name,
    )(*args)


def _pair_pipe_body(q0_ref, q1_ref, k0_ref, k1_ref, v_ref, sg_ref, o_ref, s_a, s_b, m_a, m_b):
    t = pl.program_id(0)
    tq, s_len = s_a.shape[1], s_a.shape[2]
    _define_first_slot(t, s_b, m_b)
    out_low = lax.broadcasted_iota(jnp.int32, (tq, LANES), 1) < HALF
    qs, ks = (q0_ref, q1_ref), (k0_ref, k1_ref)

    def step(s_w, m_w, s_r, m_r):
        for hd in range(2):
            s = _scores(qs[hd][...], ks[hd][...])
            s_w[hd] = s
            m_w[hd] = jnp.broadcast_to(jnp.max(s, axis=1, keepdims=True), (tq, LANES))
        outs = []
        for hd in range(2):
            p = _softmax_numerators(s_r.at[hd], m_r.at[hd], slice(None), s_len)
            a = jnp.dot(p, v_ref[...], preferred_element_type=F32)
            outs.append(a[:, :LANES] / a[:, LANES:])
        o_ref[...] = (jnp.where(out_low, outs[0], outs[1]) * sg_ref[...].astype(F32)).astype(BF16)

    _run_parity(t, step, s_a, m_a, s_b, m_b)


def _attn_c_call(qc, kc, vc, sg, g_off):
    bsz, s_len, _ = qc.shape
    tq = min(ATTN_C_TQ, s_len)
    n_q = s_len // tq
    n_pairs = C_HEADS // 2
    n_items = bsz * n_pairs * n_q

    def item(t):
        bp, i = _divmod(t, n_q)
        b, p = _divmod(bp, n_pairs)
        return b, p, i

    cur = lambda t: item(jnp.minimum(t, n_items - 1))
    prev = lambda t: item(jnp.maximum(t - 1, 0))

    def q_spec(o):
        def imap(t):
            b, p, i = cur(t)
            return b, i, 2 * p + o
        return pl.BlockSpec((None, tq, LANES), imap)

    def k_spec(o):
        def imap(t):
            b, p, i = cur(t)
            return b, 2 * p + o, 0, 0
        return pl.BlockSpec((None, None, s_len, LANES), imap)

    def v_map(t):
        b, p, i = prev(t)
        return b, 0, p

    def sg_map(t):
        b, p, i = prev(t)
        return b, i, g_off + p

    def o_map(t):
        b, p, i = prev(t)
        return b, i, p

    return pl.pallas_call(
        _pair_pipe_body,
        grid=(n_items + 1,),
        in_specs=[q_spec(0), q_spec(1), k_spec(0), k_spec(1),
                  pl.BlockSpec((None, s_len, 2 * LANES), v_map), pl.BlockSpec((None, tq, LANES), sg_map)],
        out_specs=pl.BlockSpec((None, tq, LANES), o_map),
        out_shape=jax.ShapeDtypeStruct((bsz, s_len, n_pairs * LANES), BF16),
        scratch_shapes=[pltpu.VMEM((2, tq, s_len), F32), pltpu.VMEM((2, tq, s_len), F32),
                        pltpu.VMEM((2, tq, LANES), F32), pltpu.VMEM((2, tq, LANES), F32)],
        compiler_params=pltpu.CompilerParams(dimension_semantics=("arbitrary",), vmem_limit_bytes=VMEM_LIMIT),
        name="attn_c",
    )(qc, qc, kc, kc, vc, sg)


def kernel(x, norm_g, w_in, a_q_norm, a_k_norm, b_sink, c_q_norm, c_kv_norm, c_w_uq, c_w_ukv, w_out, final_g):
    depth = w_in.shape[0]
    rows = lambda v: v.astype(F32)[:, None, :]
    w_in = w_in.astype(BF16)
    weights = [rows(norm_g), _pack_w_c(w_in), _pack_w_ab(w_in),
               rows(a_q_norm[:, _DIM_A]), rows(a_k_norm[:, _DIM_A]),
               rows(jnp.pad(c_q_norm, ((0, 0), (0, C_Q_PAD - C_Q_RANK)))), rows(c_kv_norm),
               _pack_w_uq(c_w_uq), _pack_w_ukv(c_w_ukv)]
    w_out_packed = _pack_w_out(w_out)
    sink = b_sink.astype(F32)
    tabs = _rope_tables(x.shape[1])
    e_mat = _group_sum_matrix()
    prev_out = None
    for l in range(depth):
        outs = _proj_call(x, l, weights, e_mat, tabs, prev_out=prev_out)
        if prev_out is not None:
            x = outs[-1]
        qab, ka, va, kb, vb, sg, qc, kc, vc = outs[:9]
        oa = _stack_call("attn_a", qab, ka, va, sg, 0, ATTN_A_TQ)
        oc = _attn_c_call(qc, kc, vc, sg, 2 * N_SLABS)
        if l == depth - 1:
            return _stack_call("attn_b_out", qab, kb, vb, sg, 1, ATTN_B_FINAL_TQ, sink=sink, layer=l,
                               final=(x, oa, oc, w_out_packed, final_g.astype(F32)[None, :]))
        ob = _stack_call("attn_b", qab, kb, vb, sg, 1, ATTN_B_TQ, sink=sink, layer=l)
        prev_out = (oa, ob, oc, w_out_packed)
```

```python
import functools

import numpy as np
import jax
import jax.numpy as jnp
from jax import lax
from jax.experimental import pallas as pl
from jax.experimental.pallas import tpu as pltpu

F32 = jnp.float32
BF16 = jnp.bfloat16

GRID_W = 64
ROPE_THETA = 10000.0
EPS = 1e-6
HEAD_DIM = 64
WINDOW = 128
AB_HEADS = 6
C_HEADS = 4
C_NOPE = 64
C_ROPE = 32
C_V = 64
C_Q_RANK = 192
C_KV_RANK = 128

LANES = 128
HALF = LANES // 2
QUARTER = LANES // 4
N_SLABS = AB_HEADS // 2
AB_W = AB_HEADS * HEAD_DIM
C_Q_PAD = 256
NEG = -1e30
LOG2E = 1.4426950408889634

IN_AQ, IN_AK, IN_AV, IN_AG = 0, 384, 512, 640
IN_BQ, IN_BK, IN_BV, IN_BG = 1024, 1408, 1536, 1664
IN_CQ, IN_CKV, IN_CKR, IN_CG = 2048, 2240, 2368, 2400
IN_AB_W = 2048

AB_QA, AB_KA, AB_QB, AB_KB, AB_VA, AB_VB, AB_GA, AB_GB = 0, 384, 512, 896, 1024, 1152, 1280, 1664
C_CQ, C_CKV, C_CKR, C_G, PACKED_C_W = 0, 256, 384, 512, 768
MIX_W = 1024

PACK_TM = 512
PROJ_TM = 1024
ATTN_A_TQ = 512
ATTN_A_SUB = 128
ATTN_B_TQ = 1024
ATTN_B_FINAL_TQ = 1024
ATTN_C_TQ = 1024
VMEM_LIMIT = 56 * 1024 * 1024
LARGE_VMEM_LIMIT = 60 * 1024 * 1024


_LANE = np.arange(LANES)
_HEAD_OF_LANE = (_LANE // QUARTER) % 2


def _dim_of_lane(first_half_dims, partner):
    f = np.asarray(first_half_dims)
    return np.where(_LANE < HALF, f[_LANE % QUARTER], f[_LANE % QUARTER] + partner)


_DIM_A = _dim_of_lane(list(range(16)) + list(range(32, 48)), 16)
_DIM_B = _dim_of_lane(list(range(32)), 32)

_C_ROPE_OF_LANE = np.full(LANES, -1)
_C_ROPE_OF_LANE[0:16] = np.arange(16)
_C_ROPE_OF_LANE[64:80] = 16 + np.arange(16)


def _permutation(src_of_dst):
    n = len(src_of_dst)
    p = np.zeros((n, n), np.float32)
    p[src_of_dst, np.arange(n)] = 1.0
    return jnp.asarray(p, BF16)


def _slab_sources(dim_of_lane, n_slabs):
    return np.concatenate([np.where(_HEAD_OF_LANE == 0, p, p + n_slabs) * HEAD_DIM + dim_of_lane
                           for p in range(n_slabs)])


def _pair_sources():
    d = np.arange(HEAD_DIM)
    return np.concatenate([np.concatenate([p * HEAD_DIM + d, (p + N_SLABS) * HEAD_DIM + d]) for p in range(N_SLABS)])


def _c_slab(x, rope):
    lead = x.shape[:-1]
    zeros = lambda n: jnp.zeros(lead + (n,), x.dtype)
    r0, r1 = (rope[..., :16], rope[..., 16:]) if rope is not None else (zeros(16), zeros(16))
    return jnp.concatenate([r0, x[..., :48], r1, x[..., 48:C_NOPE], zeros(LANES - C_NOPE - C_ROPE)], axis=-1)


def _pack_w_c(w):
    zeros = lambda n: jnp.zeros(w.shape[:-1] + (n,), w.dtype)
    ckr = w[..., IN_CKR:IN_CKR + C_ROPE]
    return jnp.concatenate([
        w[..., IN_CQ:IN_CQ + C_Q_RANK], zeros(C_Q_PAD - C_Q_RANK),
        w[..., IN_CKV:IN_CKV + C_KV_RANK],
        ckr[..., :16], zeros(48), ckr[..., 16:], zeros(48),
        w[..., IN_CG:IN_CG + C_HEADS * C_V]], axis=-1)


def _pack_w_uq(w):
    n = w.shape[0]
    w = w.astype(BF16).reshape(n, C_Q_RANK, C_HEADS, C_NOPE + C_ROPE)
    slab = _c_slab(w[..., :C_NOPE], w[..., C_NOPE:]).reshape(n, C_Q_RANK, C_HEADS * LANES)
    return jnp.pad(slab, ((0, 0), (0, C_Q_PAD - C_Q_RANK), (0, 0)))


def _pack_w_ukv(w):
    n = w.shape[0]
    w = w.astype(BF16).reshape(n, C_KV_RANK, C_HEADS, C_NOPE + C_V)
    k = _c_slab(w[..., :C_NOPE], None).reshape(n, C_KV_RANK, C_HEADS * LANES)
    v = w[..., C_NOPE:].reshape(n, C_KV_RANK, C_HEADS * C_V)
    return jnp.concatenate([k, v], axis=-1)


def _pack_w_out(w):
    n, _, dd = w.shape
    pair = lambda r: r.reshape(n, 2, N_SLABS, HEAD_DIM, dd).transpose(0, 2, 1, 3, 4).reshape(n, AB_W, dd)
    return jnp.concatenate([pair(w[:, :AB_W]), pair(w[:, AB_W:2 * AB_W]), w[:, 2 * AB_W:]], axis=1).astype(BF16)


def _group_sum_matrix():
    head = np.concatenate([_HEAD_OF_LANE, 2 + _HEAD_OF_LANE])
    return jnp.asarray(head[:, None] == head[None, :], BF16)


def _rope_tables(s_len):
    t = np.arange(s_len, dtype=np.float64)[:, None]
    sign = np.where(_LANE < HALF, -1.0, 1.0)[None, :]

    def tables(pos, freq, dim, active=None):
        ang = pos * (ROPE_THETA ** (-2.0 * freq / dim))[None, :]
        cos, sin = np.cos(ang), np.sin(ang) * sign
        if active is not None:
            cos, sin = np.where(active[None, :], cos, 1.0), np.where(active[None, :], sin, 0.0)
        return [cos, sin]

    pos_a = np.where((_DIM_A < HEAD_DIM // 2)[None, :], t // GRID_W, t % GRID_W)
    rope_c = _C_ROPE_OF_LANE >= 0
    tabs = (tables(pos_a, _DIM_A % 16, HEAD_DIM // 2) + tables(t, _DIM_B % 32, HEAD_DIM)
            + tables(t, np.maximum(_C_ROPE_OF_LANE, 0) % 16, C_ROPE, rope_c))
    return jnp.asarray(np.stack(tabs).astype(np.float32))


def _pack_body(w_ref, pqa_ref, pka_ref, pqb_ref, pkb_ref, pg_ref, o_ref):
    def move(src, dst, width, perm_ref):
        x = w_ref[:, src:src + width].astype(BF16)
        if perm_ref is not None:
            x = jnp.dot(x, perm_ref[...], preferred_element_type=F32).astype(BF16)
        o_ref[:, dst:dst + width] = x

    move(IN_AQ, AB_QA, AB_W, pqa_ref)
    move(IN_AK, AB_KA, LANES, pka_ref)
    move(IN_BQ, AB_QB, AB_W, pqb_ref)
    move(IN_BK, AB_KB, LANES, pkb_ref)
    move(IN_AV, AB_VA, LANES, None)
    move(IN_BV, AB_VB, LANES, None)
    move(IN_AG, AB_GA, AB_W, pg_ref)
    move(IN_BG, AB_GB, AB_W, pg_ref)


def _pack_w_ab(w_in):
    depth, d, _ = w_in.shape
    tm = min(PACK_TM, d)
    perms = [_permutation(_slab_sources(_DIM_A, N_SLABS)), _permutation(_slab_sources(_DIM_A, 1)),
             _permutation(_slab_sources(_DIM_B, N_SLABS)), _permutation(_slab_sources(_DIM_B, 1)),
             _permutation(_pair_sources())]
    return pl.pallas_call(
        _pack_body,
        grid=(depth, d // tm),
        in_specs=[pl.BlockSpec((None, tm, IN_AB_W), lambda l, r: (l, r, 0))]
                 + [pl.BlockSpec(p.shape, lambda l, r: (0, 0)) for p in perms],
        out_specs=pl.BlockSpec((None, tm, IN_AB_W), lambda l, r: (l, r, 0)),
        out_shape=jax.ShapeDtypeStruct((depth, d, IN_AB_W), BF16),
        compiler_params=pltpu.CompilerParams(
            dimension_semantics=("arbitrary", "arbitrary"), vmem_limit_bytes=VMEM_LIMIT),
        name="pack",
    )(w_in, *perms)


def _proj_body(x_ref, *refs):
    _proj_compute(x_ref[...], *refs)


def _out_proj_body(x_ref, oa_ref, ob_ref, oc_ref, wo_ref, *refs):
    proj_refs, x_out_ref = refs[:-1], refs[-1]
    o = jnp.concatenate([oa_ref[...], ob_ref[...], oc_ref[...]], axis=1)
    x = x_ref[...] + jnp.dot(o, wo_ref[...], preferred_element_type=F32)
    x_out_ref[...] = x
    _proj_compute(x, *proj_refs)


def _proj_compute(x, g_ref, wc_ref, wab_ref, aqn_ref, akn_ref, cqn_ref, ckvn_ref, wuq_ref, wukv_ref, e_ref, tab_ref,
                  qab_ref, ka_ref, va_ref, kb_ref, vb_ref, sg_ref, qc_ref, kc_ref, vc_ref):
    h = (x * lax.rsqrt(jnp.mean(x * x, axis=-1, keepdims=True) + EPS) * g_ref[...]).astype(BF16)
    zc = jnp.dot(h, wc_ref[...], preferred_element_type=F32)
    zab = jnp.dot(h, wab_ref[...], preferred_element_type=F32)

    def rope(xs, mixer):
        return xs * tab_ref[2 * mixer] + pltpu.roll(xs, HALF, 1) * tab_ref[2 * mixer + 1]

    def slab(z, off):
        return z[:, off:off + LANES]

    def head_sumsq(s0, s1):
        t = jnp.concatenate([s0 * s0, s1 * s1], axis=1).astype(BF16)
        ss = jnp.dot(t, e_ref[...], preferred_element_type=F32)
        return ss[:, :LANES], ss[:, LANES:]

    def head_norm(xs, ss, gain):
        return xs * lax.rsqrt(ss * (1.0 / HEAD_DIM) + EPS) * gain

    def silu(gz):
        return (gz * (0.5 * jnp.tanh(0.5 * gz) + 0.5)).astype(BF16)

    cq = zc[:, C_CQ:C_CQ + C_Q_PAD]
    cq = cq * lax.rsqrt(jnp.sum(cq * cq, axis=-1, keepdims=True) * (1.0 / C_Q_RANK) + EPS) * cqn_ref[...]
    qc = jnp.dot(cq.astype(BF16), wuq_ref[...], preferred_element_type=F32)
    scale_c = (C_NOPE + C_ROPE) ** -0.5 * LOG2E
    for hh in range(C_HEADS):
        q = rope(qc[:, hh * LANES:(hh + 1) * LANES], 2)
        qc_ref[:, hh * LANES:(hh + 1) * LANES] = (q * scale_c).astype(BF16)
    ckv = zc[:, C_CKV:C_CKV + C_KV_RANK]
    ckv = ckv * lax.rsqrt(jnp.mean(ckv * ckv, axis=-1, keepdims=True) + EPS) * ckvn_ref[...]
    kv = jnp.dot(ckv.astype(BF16), wukv_ref[...], preferred_element_type=F32)
    kr = rope(slab(zc, C_CKR), 2)
    for hh in range(C_HEADS):
        kc_ref[hh] = (kv[:, hh * LANES:(hh + 1) * LANES] + kr).astype(BF16)
    ones = jnp.ones((x.shape[0], LANES), BF16)
    for p in range(C_HEADS // 2):
        v_pair = kv[:, (C_HEADS + p) * LANES:(C_HEADS + p + 1) * LANES].astype(BF16)
        vc_ref[:, 2 * p * LANES:(2 * p + 2) * LANES] = jnp.concatenate([v_pair, ones], axis=1)
    sg_ref[:, 2 * AB_W:] = silu(zc[:, C_G:])

    scale_ab = HEAD_DIM ** -0.5 * LOG2E
    qa = [slab(zab, AB_QA + j * LANES) for j in range(N_SLABS)]
    ka = slab(zab, AB_KA)
    ss0, ss1 = head_sumsq(qa[0], qa[1])
    ss2, ssk = head_sumsq(qa[2], ka)
    for j, ss in enumerate((ss0, ss1, ss2)):
        q = rope(head_norm(qa[j], ss, aqn_ref[...]), 0)
        qab_ref[:, j * LANES:(j + 1) * LANES] = (q * scale_ab).astype(BF16)
    ka_ref[...] = rope(head_norm(ka, ssk, akn_ref[...]), 0).astype(BF16)
    for j in range(N_SLABS):
        q = rope(slab(zab, AB_QB + j * LANES), 1)
        qab_ref[:, (N_SLABS + j) * LANES:(N_SLABS + j + 1) * LANES] = (q * scale_ab).astype(BF16)
    kb_ref[...] = rope(slab(zab, AB_KB), 1).astype(BF16)
    va_ref[...] = jnp.concatenate([slab(zab, AB_VA).astype(BF16), ones], axis=1)
    vb_ref[...] = jnp.concatenate([slab(zab, AB_VB).astype(BF16), ones], axis=1)
    sg_ref[:, :2 * AB_W] = silu(zab[:, AB_GA:AB_GA + 2 * AB_W])


def _layer_spec(a, layer):
    return pl.BlockSpec((None,) + a.shape[1:], lambda *_: (layer,) + (0,) * (a.ndim - 1))


def _proj_call(x, layer, weights, e_mat, tabs, prev_out=None):
    bsz, s_len, d = x.shape
    tm = min(PROJ_TM, s_len)
    grid = (s_len // tm, bsz)
    tok = lambda w: pl.BlockSpec((None, tm, w), lambda s, b: (b, s, 0))
    bf = lambda *shape: jax.ShapeDtypeStruct(shape, BF16)
    widths = [2 * N_SLABS * LANES, LANES, 2 * LANES, LANES, 2 * LANES, MIX_W, C_HEADS * LANES]
    in_specs = [tok(d)]
    args = [x]
    out_specs = ([tok(w) for w in widths]
                 + [pl.BlockSpec((None, C_HEADS, tm, LANES), lambda s, b: (b, 0, s, 0)), tok(C_HEADS * LANES)])
    out_shape = ([bf(bsz, s_len, w) for w in widths]
                 + [bf(bsz, C_HEADS, s_len, LANES), bf(bsz, s_len, C_HEADS * LANES)])
    body = _proj_body
    if prev_out is not None:
        oa, ob, oc, w_out = prev_out
        in_specs += [tok(oa.shape[-1]), tok(ob.shape[-1]), tok(oc.shape[-1]), _layer_spec(w_out, layer - 1)]
        args += [oa, ob, oc, w_out]
        out_specs.append(tok(d))
        out_shape.append(jax.ShapeDtypeStruct(x.shape, F32))
        body = _out_proj_body
    return pl.pallas_call(
        body,
        grid=grid,
        in_specs=in_specs + [_layer_spec(a, layer) for a in weights]
                 + [pl.BlockSpec(e_mat.shape, lambda s, b: (0, 0)),
                    pl.BlockSpec((6, tm, LANES), lambda s, b: (0, s, 0))],
        out_specs=out_specs,
        out_shape=out_shape,
        compiler_params=pltpu.CompilerParams(
            dimension_semantics=("arbitrary", "arbitrary"), vmem_limit_bytes=LARGE_VMEM_LIMIT),
        name="proj" if prev_out is None else "out_proj",
    )(*args, *weights, e_mat, tabs)


def _divmod(t, n):
    if n & (n - 1) == 0:
        return lax.shift_right_logical(t, n.bit_length() - 1), t & (n - 1)
    return lax.div(t, n), lax.rem(t, n)


def _scores(q, k):
    return lax.dot_general(q, k, (((1,), (1,)), ((), ())), preferred_element_type=F32)


def _define_first_slot(t, s_b, m_b):
    @pl.when(t == 0)
    def _():
        s_b[...] = jnp.zeros(s_b.shape, F32)
        m_b[...] = jnp.zeros(m_b.shape, F32)


def _run_parity(t, step, s_a, m_a, s_b, m_b):
    parity = t & 1

    @pl.when(parity == 0)
    def _():
        step(s_a, m_a, s_b, m_b)

    @pl.when(parity == 1)
    def _():
        step(s_b, m_b, s_a, m_a)


def _softmax_numerators(s_r, m_r, rows, width):
    m = m_r[rows, :]
    return jnp.concatenate(
        [jnp.exp2(s_r[rows, j * LANES:(j + 1) * LANES] - m).astype(BF16) for j in range(width // LANES)], axis=1)


def _stack_queries(q, head_a):
    slabs = [q[:, j * LANES:(j + 1) * LANES] for j in range(N_SLABS)]
    return jnp.concatenate([jnp.where(head_a, s, 0.0) for s in slabs]
                           + [jnp.where(head_a, 0.0, s) for s in slabs], axis=0).astype(BF16)


def _stack_pipe_body(q_ref, k_ref, v_ref, sg_ref, o_ref, s_a, s_b, m_a, m_b):
    t = pl.program_id(0)
    tq = q_ref.shape[0]
    s_len = k_ref.shape[0]
    _define_first_slot(t, s_b, m_b)

    sub = min(ATTN_A_SUB, tq)
    lane = lax.broadcasted_iota(jnp.int32, (sub, LANES), 1)
    out_low = lane < HALF

    def step(s_w, m_w, s_r, m_r):
        q = q_ref[...].astype(F32)
        for u in range(tq // sub):
            s = _scores(_stack_queries(q[u * sub:(u + 1) * sub], (lane & QUARTER) == 0), k_ref[...])
            for i in range(AB_HEADS):
                blk = s[i * sub:(i + 1) * sub]
                rows = pl.ds((u * AB_HEADS + i) * sub, sub)
                s_w[rows, :] = blk
                m_w[rows, :] = jnp.broadcast_to(jnp.max(blk, axis=1, keepdims=True), (sub, LANES))
        for u in range(tq // sub):
            rows = [pl.ds((u * AB_HEADS + i) * sub, sub) for i in range(AB_HEADS)]
            p = jnp.concatenate([_softmax_numerators(s_r, m_r, rows[i], s_len) for i in range(AB_HEADS)], axis=0)
            a = jnp.dot(p, v_ref[...], preferred_element_type=F32)
            for j in range(N_SLABS):
                lo, hi = a[j * sub:(j + 1) * sub], a[(j + N_SLABS) * sub:(j + N_SLABS + 1) * sub]
                o = jnp.where(out_low, lo[:, :LANES] / lo[:, LANES:], hi[:, :LANES] / hi[:, LANES:])
                cols = slice(j * LANES, (j + 1) * LANES)
                gate = sg_ref[u * sub:(u + 1) * sub, cols].astype(F32)
                o_ref[u * sub:(u + 1) * sub, cols] = (o * gate).astype(BF16)

    _run_parity(t, step, s_a, m_a, s_b, m_b)


def _window_pipe_body(sink_ref, q_ref, k_ref, v_ref, sg_ref, *refs, n_q, n_items, layer, final):
    if final:
        x_ref, oa_ref, oc_ref, wo_ref, fg_ref, y_ref, s_a, s_b, m_a, m_b, o_ref = refs
    else:
        o_ref, s_a, s_b, m_a, m_b = refs
    t = pl.program_id(0)
    tq = q_ref.shape[0]
    s_len = k_ref.shape[0]
    sub = WINDOW
    kw = 3 * WINDOW
    n_sub = tq // sub
    t_cur = jnp.minimum(t, n_items - 1)
    t_prev = jnp.maximum(t - 1, 0)
    _define_first_slot(t, s_b, m_b)

    lane = lax.broadcasted_iota(jnp.int32, (sub, LANES), 1)
    row = lax.broadcasted_iota(jnp.int32, (sub, LANES), 0)
    head_a = (lane & QUARTER) == 0
    out_low = lane < HALF
    row0_cur = _divmod(t_cur, n_q)[1] * tq
    row0_prev = _divmod(t_prev, n_q)[1] * tq
    sinks = [sink_ref[layer, i] * LOG2E for i in range(AB_HEADS)]

    def win_start(r0):
        return pl.multiple_of(jnp.clip(r0 - WINDOW, 0, s_len - kw), WINDOW)

    def step(s_w, m_w, s_r, m_r):
        q = q_ref[...].astype(F32)
        for u in range(n_sub):
            r0 = row0_cur + u * sub
            ws = win_start(r0)
            dist = (r0 - ws) + row - lane
            bias = [jnp.where(jnp.abs(dist - c * LANES) <= WINDOW, 0.0, NEG) for c in range(3)]
            s = _scores(_stack_queries(q[u * sub:(u + 1) * sub], head_a), k_ref[pl.ds(ws, kw), :])
            for i in range(AB_HEADS):
                blk = s[i * sub:(i + 1) * sub]
                cols = [blk[:, c * LANES:(c + 1) * LANES] + bias[c] for c in range(3)]
                mx = jnp.max(jnp.maximum(jnp.maximum(cols[0], cols[1]), cols[2]), axis=1, keepdims=True)
                rows = pl.ds((u * AB_HEADS + i) * sub, sub)
                for c in range(3):
                    s_w[rows, c * LANES:(c + 1) * LANES] = cols[c]
                m_w[rows, :] = jnp.broadcast_to(jnp.maximum(mx, sinks[i]), (sub, LANES))
        for u in range(n_sub):
            v_win = v_ref[pl.ds(win_start(row0_prev + u * sub), kw), :]
            rows = [pl.ds((u * AB_HEADS + i) * sub, sub) for i in range(AB_HEADS)]
            p = jnp.concatenate([_softmax_numerators(s_r, m_r, rows[i], kw) for i in range(AB_HEADS)], axis=0)
            a = jnp.dot(p, v_win, preferred_element_type=F32)
            for j in range(N_SLABS):
                parts = []
                for i in (j, j + N_SLABS):
                    blk = a[i * sub:(i + 1) * sub]
                    den = blk[:, LANES:] + jnp.exp2(sinks[i] - m_r[rows[i], :])
                    parts.append(blk[:, :LANES] / den)
                cols = slice(j * LANES, (j + 1) * LANES)
                gate = sg_ref[u * sub:(u + 1) * sub, cols].astype(F32)
                o_ref[u * sub:(u + 1) * sub, cols] = (jnp.where(out_low, parts[0], parts[1]) * gate).astype(BF16)
        if final:
            o = jnp.concatenate([oa_ref[...], o_ref[...], oc_ref[...]], axis=1)
            r = x_ref[...] + jnp.dot(o, wo_ref[...], preferred_element_type=F32)
            y_ref[...] = r * lax.rsqrt(jnp.mean(r * r, axis=-1, keepdims=True) + EPS) * fg_ref[...]

    _run_parity(t, step, s_a, m_a, s_b, m_b)


def _stack_call(name, qab, k, v, sg, blk, tq, sink=None, layer=0, final=None):
    bsz, s_len, _ = qab.shape
    tq = min(tq, s_len)
    n_q = s_len // tq
    n_items = bsz * n_q
    windowed = sink is not None
    cur = lambda t: _divmod(jnp.minimum(t, n_items - 1), n_q)
    prev = lambda t: _divmod(jnp.maximum(t - 1, 0), n_q)
    in_specs = [
        pl.BlockSpec((None, tq, AB_W), lambda t: (*cur(t), blk)),
        pl.BlockSpec((None, s_len, LANES), lambda t: (cur(t)[0], 0, 0)),
        pl.BlockSpec((None, s_len, 2 * LANES), lambda t: (prev(t)[0], 0, 0)),
        pl.BlockSpec((None, tq, AB_W), lambda t: (*prev(t), blk)),
    ]
    args = [qab, k, v, sg]
    rows = AB_HEADS * tq
    width = s_len
    body = _stack_pipe_body
    if windowed:
        in_specs = [pl.BlockSpec(memory_space=pltpu.SMEM)] + in_specs
        args = [sink] + args
        width = 3 * WINDOW
        body = functools.partial(_window_pipe_body, n_q=n_q, n_items=n_items, layer=layer, final=final is not None)
    scratch = [pltpu.VMEM((rows, width), F32), pltpu.VMEM((rows, width), F32),
               pltpu.VMEM((rows, LANES), F32), pltpu.VMEM((rows, LANES), F32)]
    out_specs = pl.BlockSpec((None, tq, AB_W), lambda t: (*prev(t), 0))
    out_shape = jax.ShapeDtypeStruct((bsz, s_len, AB_W), BF16)
    if final is not None:
        x, oa, oc, w_out, fg = final
        tok = lambda a: pl.BlockSpec((None, tq, a.shape[-1]), lambda t: (*prev(t), 0))
        in_specs += [tok(x), tok(oa), tok(oc), _layer_spec(w_out, layer), pl.BlockSpec(fg.shape, lambda t: (0, 0))]
        args += [x, oa, oc, w_out, fg]
        out_specs, out_shape = tok(x), jax.ShapeDtypeStruct(x.shape, F32)
        scratch.append(pltpu.VMEM((tq, AB_W), BF16))
    return pl.pallas_call(
        body,
        grid=(n_items + 1,),
        in_specs=in_specs,
        out_specs=out_specs,
        out_shape=out_shape,
        scratch_shapes=scratch,
        compiler_params=pltpu.CompilerParams(
            dimension_semantics=("arbitrary",),
            vmem_limit_bytes=VMEM_LIMIT if (windowed and final is None) else LARGE_VMEM_LIMIT),
        name=name,
    )(*args)


def _pair_pipe_body(q0_ref, q1_ref, k0_ref, k1_ref, v_ref, sg_ref, o_ref, s_a, s_b, m_a, m_b):
    t = pl.program_id(0)
    tq, s_len = s_a.shape[1], s_a.shape[2]
    _define_first_slot(t, s_b, m_b)
    out_low = lax.broadcasted_iota(jnp.int32, (tq, LANES), 1) < HALF
    qs, ks = (q0_ref, q1_ref), (k0_ref, k1_ref)

    def step(s_w, m_w, s_r, m_r):
        for hd in range(2):
            s = _scores(qs[hd][...], ks[hd][...])
            s_w[hd] = s
            m_w[hd] = jnp.broadcast_to(jnp.max(s, axis=1, keepdims=True), (tq, LANES))
        outs = []
        for hd in range(2):
            p = _softmax_numerators(s_r.at[hd], m_r.at[hd], slice(None), s_len)
            a = jnp.dot(p, v_ref[...], preferred_element_type=F32)
            outs.append(a[:, :LANES] / a[:, LANES:])
        o_ref[...] = (jnp.where(out_low, outs[0], outs[1]) * sg_ref[...].astype(F32)).astype(BF16)

    _run_parity(t, step, s_a, m_a, s_b, m_b)


def _attn_c_call(qc, kc, vc, sg, g_off):
    bsz, s_len, _ = qc.shape
    tq = min(ATTN_C_TQ, s_len)
    n_q = s_len // tq
    n_pairs = C_HEADS // 2
    n_items = bsz * n_pairs * n_q

    def item(t):
        bp, i = _divmod(t, n_q)
        b, p = _divmod(bp, n_pairs)
        return b, p, i

    cur = lambda t: item(jnp.minimum(t, n_items - 1))
    prev = lambda t: item(jnp.maximum(t - 1, 0))

    def q_spec(o):
        def imap(t):
            b, p, i = cur(t)
            return b, i, 2 * p + o
        return pl.BlockSpec((None, tq, LANES), imap)

    def k_spec(o):
        def imap(t):
            b, p, i = cur(t)
            return b, 2 * p + o, 0, 0
        return pl.BlockSpec((None, None, s_len, LANES), imap)

    def v_map(t):
        b, p, i = prev(t)
        return b, 0, p

    def sg_map(t):
        b, p, i = prev(t)
        return b, i, g_off + p

    def o_map(t):
        b, p, i = prev(t)
        return b, i, p

    return pl.pallas_call(
        _pair_pipe_body,
        grid=(n_items + 1,),
        in_specs=[q_spec(0), q_spec(1), k_spec(0), k_spec(1),
                  pl.BlockSpec((None, s_len, 2 * LANES), v_map), pl.BlockSpec((None, tq, LANES), sg_map)],
        out_specs=pl.BlockSpec((None, tq, LANES), o_map),
        out_shape=jax.ShapeDtypeStruct((bsz, s_len, n_pairs * LANES), BF16),
        scratch_shapes=[pltpu.VMEM((2, tq, s_len), F32), pltpu.VMEM((2, tq, s_len), F32),
                        pltpu.VMEM((2, tq, LANES), F32), pltpu.VMEM((2, tq, LANES), F32)],
        compiler_params=pltpu.CompilerParams(dimension_semantics=("arbitrary",), vmem_limit_bytes=VMEM_LIMIT),
        name="attn_c",
    )(qc, qc, kc, kc, vc, sg)


def kernel(x, norm_g, w_in, a_q_norm, a_k_norm, b_sink, c_q_norm, c_kv_norm, c_w_uq, c_w_ukv, w_out, final_g):
    depth = w_in.shape[0]
    rows = lambda v: v.astype(F32)[:, None, :]
    weights = [rows(norm_g), _pack_w_c(w_in).astype(BF16), _pack_w_ab(w_in),
               rows(a_q_norm[:, _DIM_A]), rows(a_k_norm[:, _DIM_A]),
               rows(jnp.pad(c_q_norm, ((0, 0), (0, C_Q_PAD - C_Q_RANK)))), rows(c_kv_norm),
               _pack_w_uq(c_w_uq), _pack_w_ukv(c_w_ukv)]
    w_out_packed = _pack_w_out(w_out)
    sink = b_sink.astype(F32)
    tabs = _rope_tables(x.shape[1])
    e_mat = _group_sum_matrix()
    prev_out = None
    for l in range(depth):
        outs = _proj_call(x, l, weights, e_mat, tabs, prev_out=prev_out)
        if prev_out is not None:
            x = outs[-1]
        qab, ka, va, kb, vb, sg, qc, kc, vc = outs[:9]
        oa = _stack_call("attn_a", qab, ka, va, sg, 0, ATTN_A_TQ)
        oc = _attn_c_call(qc, kc, vc, sg, 2 * N_SLABS)
        if l == depth - 1:
            return _stack_call("attn_b_out", qab, kb, vb, sg, 1, ATTN_B_FINAL_TQ, sink=sink, layer=l,
                               final=(x, oa, oc, w_out_packed, final_g.astype(F32)[None, :]))
        ob = _stack_call("attn_b", qab, kb, vb, sg, 1, ATTN_B_TQ, sink=sink, layer=l)
        prev_out = (oa, ob, oc, w_out_packed)
```
